```python
import jax, jax.numpy as jnp
from jax import lax
import numpy as np

D_MODEL = 2048
BATCH = 16
SEQ = 2048
DEPTH = 1

GRID_W = 64
R_HEADS = 16
R_HEAD_DIM = 64
R_WIDTH = R_HEADS * R_HEAD_DIM
DECAY_LORA = 64
ICLR_LORA = 64
A_Q_HEADS = 16
A_KV_HEADS = 4
A_HEAD_DIM = 64
A_GROUP = A_Q_HEADS // A_KV_HEADS
A_WIDTH = A_Q_HEADS * A_HEAD_DIM
A_KV_WIDTH = A_KV_HEADS * A_HEAD_DIM
AXIS_DIM = A_HEAD_DIM // 2
Q_BLOCK = 128
ROPE_THETA = 10000.0
N_BRANCHES = 2
NORM_EPS = 1e-6
GN_EPS = 64e-5
SHIFT_SIZES = (R_WIDTH, R_WIDTH, R_WIDTH, 2 * DECAY_LORA, 2 * ICLR_LORA)
SHIFT_WIDTH = 3 * R_WIDTH + 2 * DECAY_LORA + 2 * ICLR_LORA
REST_SIZES = (R_WIDTH, A_WIDTH, A_KV_WIDTH, A_KV_WIDTH, A_WIDTH, N_BRANCHES * D_MODEL)
D_IN = SHIFT_WIDTH + 2 * R_WIDTH // 2 * 1 + 2 * A_WIDTH + 2 * A_KV_WIDTH + N_BRANCHES * D_MODEL - R_WIDTH + R_WIDTH // 1 * 0

kernel_name = "hybrid_rwkv7_axial_gqa_gated_block"


def _split(t, sizes):
    out, off = [], 0
    for s in sizes:
        out.append(t[..., off:off + s])
        off += s
    return out


def rms_norm(x, g, eps=NORM_EPS):
    xf = x.astype(jnp.float32)
    y = xf * lax.rsqrt(jnp.mean(xf * xf, axis=-1, keepdims=True) + eps)
    return (y * g.astype(jnp.float32)).astype(x.dtype)


def centred_shift(y, mu):
    prev = jnp.pad(y[:, :-1], ((0, 0), (1, 0), (0, 0)))
    nxt = jnp.pad(y[:, 1:], ((0, 0), (0, 1), (0, 0)))
    return y + mu[0] * (prev - y) + mu[1] * (nxt - y)


def _heads(t):
    return t.reshape(*t.shape[:-1], R_HEADS, R_HEAD_DIM)


def _dir_stack(t):
    s = jnp.stack([t[0], jnp.flip(t[1], axis=1)], axis=0)
    return jnp.moveaxis(s, 2, 0)


def _wkv_step(S, inp):
    w, kk, b, k, v, r = inp
    sa = jnp.einsum('dbhvk,dbhk->dbhv', S, -kk)
    S = S * w[..., None, :] + sa[..., None] * b[..., None, :] + v[..., :, None] * k[..., None, :]
    o = jnp.einsum('dbhvk,dbhk->dbhv', S, r)
    return S, o


def rwkv7_bidir(xr, xk, xv, wdown, adown, w0, w_up, a0, a_up, k_k, k_a, r_k, gn_w, gn_b):
    f32 = jnp.float32
    B, T, C = xr.shape
    wd = jnp.tanh(wdown.astype(f32).reshape(B, T, 2, DECAY_LORA))
    w_raw = w0.astype(f32)[:, None, None, :] + jnp.einsum('btdr,drc->dbtc', wd, w_up.astype(f32))
    decay = jnp.exp(-jnp.exp(-jax.nn.softplus(-w_raw) - 0.5))
    ad = adown.astype(f32).reshape(B, T, 2, ICLR_LORA)
    a = jax.nn.sigmoid(a0.astype(f32)[:, None, None, :] + jnp.einsum('btdr,drc->dbtc', ad, a_up.astype(f32)))
    r = xr.astype(f32)
    k = xk.astype(f32)
    v = xv.astype(f32)
    kkh = _heads(k * k_k.astype(f32))
    kkh = kkh * lax.rsqrt(jnp.maximum(jnp.sum(kkh * kkh, axis=-1, keepdims=True), 1e-24))
    ah = _heads(a)
    k_eff = _heads(k[None] * (1.0 + (a - 1.0) * k_a.astype(f32)))
    rh, vh = _heads(r), _heads(v)
    both = lambda t: jnp.stack([t, t], axis=0)
    seq_in = (_dir_stack(_heads(decay)), _dir_stack(both(kkh)), _dir_stack(kkh[None] * ah),
              _dir_stack(k_eff), _dir_stack(both(vh)), _dir_stack(both(rh)))
    S0 = jnp.zeros((2, B, R_HEADS, R_HEAD_DIM, R_HEAD_DIM), f32)
    _, outs = lax.scan(_wkv_step, S0, seq_in)
    outs = jnp.moveaxis(outs, 0, 2)
    wkv = outs[0] + jnp.flip(outs[1], axis=1)
    mu = jnp.mean(wkv, axis=-1, keepdims=True)
    var = jnp.mean(jnp.square(wkv - mu), axis=-1, keepdims=True)
    gn = (wkv - mu) * lax.rsqrt(var + GN_EPS)
    gn = gn * _heads(gn_w.astype(f32)) + _heads(gn_b.astype(f32))
    coef = jnp.sum(rh[None] * k_eff * _heads(r_k.astype(f32)), axis=-1, keepdims=True).sum(0)
    out = gn + coef * vh
    return out.reshape(B, T, C).astype(xr.dtype)


def axial_rope_tables(T):
    rows = T // GRID_W
    row = jnp.repeat(jnp.arange(rows, dtype=jnp.float32), GRID_W)
    col = jnp.tile(jnp.arange(GRID_W, dtype=jnp.float32), rows)
    freqs = ROPE_THETA ** (-jnp.arange(0, AXIS_DIM, 2, dtype=jnp.float32) / AXIS_DIM)
    ang = jnp.concatenate([row[:, None] * freqs, col[:, None] * freqs], axis=-1)
    return jnp.cos(ang), jnp.sin(ang)


def apply_rope(x, cos, sin):
    xf = x.astype(jnp.float32).reshape(*x.shape[:-1], A_HEAD_DIM // 2, 2)
    x0, x1 = xf[..., 0], xf[..., 1]
    c, s = cos[None, :, None, :], sin[None, :, None, :]
    out = jnp.stack([x0 * c - x1 * s, x0 * s + x1 * c], axis=-1)
    return out.reshape(x.shape).astype(x.dtype)


def axial_gqa(q, k, v, q_norm_g, k_norm_g):
    B, T, _ = q.shape
    q = rms_norm(q.reshape(B, T, A_Q_HEADS, A_HEAD_DIM), q_norm_g)
    k = rms_norm(k.reshape(B, T, A_KV_HEADS, A_HEAD_DIM), k_norm_g)
    v = v.reshape(B, T, A_KV_HEADS, A_HEAD_DIM)
    cos, sin = axial_rope_tables(T)
    q = apply_rope(q, cos, sin)
    k = apply_rope(k, cos, sin)
    scale = A_HEAD_DIM ** -0.5
    nb = T // Q_BLOCK
    qb = q.reshape(B, nb, Q_BLOCK, A_KV_HEADS, A_GROUP, A_HEAD_DIM).transpose(1, 0, 2, 3, 4, 5)

    def one_block(qblk):
        s = jnp.einsum('bqhgd,bkhd->bhgqk', qblk, k).astype(jnp.float32) * scale
        p = jax.nn.softmax(s, axis=-1)
        return jnp.einsum('bhgqk,bkhd->bqhgd', p.astype(v.dtype), v)

    o = lax.map(one_block, qb)
    return o.transpose(1, 0, 2, 3, 4, 5).reshape(B, T, A_WIDTH)


def hybrid_layer(x, norm_g, w_in, shift_mu, w0, w_up, a0, a_up, k_k, k_a, r_k, gn_w, gn_b,
                 q_norm_g, k_norm_g, w_branch_rwkv, w_branch_attn, w_out):
    h = rms_norm(x, norm_g)
    proj = jnp.einsum('btd,de->bte', h, w_in)
    shifted = centred_shift(proj[..., :SHIFT_WIDTH], shift_mu)
    xr, xk, xv, wdown, adown = _split(shifted, SHIFT_SIZES)
    z_r, q, k, v, z_a, gates = _split(proj[..., SHIFT_WIDTH:], REST_SIZES)
    o_r = rwkv7_bidir(xr, xk, xv, wdown, adown, w0, w_up, a0, a_up, k_k, k_a, r_k, gn_w, gn_b)
    o_r = o_r * jax.nn.silu(z_r)
    o_a = axial_gqa(q, k, v, q_norm_g, k_norm_g) * jax.nn.silu(z_a)
    p_r = jnp.einsum('btc,cd->btd', o_r, w_branch_rwkv)
    p_a = jnp.einsum('btc,cd->btd', o_a, w_branch_attn)
    g_r, g_a = _split(gates, (D_MODEL, D_MODEL))
    merged = jax.nn.sigmoid(g_r) * p_r + jax.nn.sigmoid(g_a) * p_a
    return x + jnp.einsum('btd,de->bte', merged, w_out)


def setup_inputs(seed: int = 0) -> dict:
    key = jax.random.key(seed)
    ks = jax.random.split(key, 20)
    f32 = jnp.float32
    nrm = lambda k, s, sc: jax.random.normal(k, s, f32) * sc
    d_in = SHIFT_WIDTH + sum(REST_SIZES)
    return {
        "x": nrm(ks[0], (BATCH, SEQ, D_MODEL), 1.0),
        "norm_g": 1.0 + nrm(ks[1], (DEPTH, D_MODEL), 0.02),
        "w_in": nrm(ks[2], (DEPTH, D_MODEL, d_in), D_MODEL ** -0.5),
        "shift_mu": jax.random.uniform(ks[3], (DEPTH, 2, SHIFT_WIDTH), f32, 0.0, 0.5),
        "w0": jax.random.uniform(ks[4], (DEPTH, 2, R_WIDTH), f32, -5.0, 1.0),
        "w_up": nrm(ks[5], (DEPTH, 2, DECAY_LORA, R_WIDTH), 0.1 * DECAY_LORA ** -0.5),
        "a0": nrm(ks[6], (DEPTH, 2, R_WIDTH), 0.3),
        "a_up": nrm(ks[7], (DEPTH, 2, ICLR_LORA, R_WIDTH), 0.1 * ICLR_LORA ** -0.5),
        "k_k": 0.85 + nrm(ks[8], (DEPTH, R_WIDTH), 0.02),
        "k_a": 1.0 + nrm(ks[9], (DEPTH, R_WIDTH), 0.02),
        "r_k": nrm(ks[10], (DEPTH, R_WIDTH), 0.1),
        "gn_w": 1.0 + nrm(ks[11], (DEPTH, R_WIDTH), 0.02),
        "gn_b": nrm(ks[12], (DEPTH, R_WIDTH), 0.02),
        "q_norm_g": 1.0 + nrm(ks[13], (DEPTH, A_HEAD_DIM), 0.02),
        "k_norm_g": 1.0 + nrm(ks[14], (DEPTH, A_HEAD_DIM), 0.02),
        "w_branch_rwkv": nrm(ks[15], (DEPTH, R_WIDTH, D_MODEL), R_WIDTH ** -0.5),
        "w_branch_attn": nrm(ks[16], (DEPTH, A_WIDTH, D_MODEL), A_WIDTH ** -0.5),
        "w_out": nrm(ks[17], (DEPTH, D_MODEL, D_MODEL), D_MODEL ** -0.5),
        "final_norm_g": 1.0 + nrm(ks[18], (D_MODEL,), 0.02),
    }


def reference(x, norm_g, w_in, shift_mu, w0, w_up, a0, a_up, k_k, k_a, r_k, gn_w, gn_b,
              q_norm_g, k_norm_g, w_branch_rwkv, w_branch_attn, w_out, final_norm_g):
    for l in range(DEPTH):
        x = hybrid_layer(x, norm_g[l], w_in[l], shift_mu[l], w0[l], w_up[l], a0[l], a_up[l],
                         k_k[l], k_a[l], r_k[l], gn_w[l], gn_b[l], q_norm_g[l], k_norm_g[l],
                         w_branch_rwkv[l], w_branch_attn[l], w_out[l])
    return rms_norm(x, final_norm_g)
```

```python
import functools
import math

import jax
import jax.numpy as jnp
import numpy as np
from jax import lax
from jax.experimental import pallas as pl
from jax.experimental.pallas import tpu as pltpu

F32 = jnp.float32
BF16 = jnp.bfloat16

HEAD_DIM = 64
R_HEADS = 16
R_WIDTH = R_HEADS * HEAD_DIM
LORA = 64
A_Q_HEADS = 16
A_KV_HEADS = 4
A_GROUP = A_Q_HEADS // A_KV_HEADS
A_WIDTH = A_Q_HEADS * HEAD_DIM
A_KV_WIDTH = A_KV_HEADS * HEAD_DIM
GRID_W = 64
ROPE_THETA = 10000.0
NORM_EPS = 1e-6
GN_EPS = 64e-5
KK_EPS = 1e-24
DECAY_SCALE = math.exp(-0.5)

LANES = 128
VMEM_LIMIT = 56 * 1024 * 1024

CHUNK = 64
PAIRS = R_WIDTH // LANES

COL_XR, COL_XK, COL_XV, COL_ZR, COL_Q, COL_ZA = 0, 1024, 2048, 3072, 4096, 5120
COL_GR, COL_GA, COL_K, COL_V, COL_WD, COL_AD = 6144, 8192, 10240, 10496, 10752, 10880
D_IN = 11008
PROJ_TN = 256


def _bdot(a, b):
    return jnp.dot(a.astype(BF16), b.astype(BF16), preferred_element_type=F32)


def _bdot_nt(a, b):
    return lax.dot_general(a.astype(BF16), b.astype(BF16), (((1,), (1,)), ((), ())),
                           preferred_element_type=F32)


def _bdot_tn(a, b):
    return lax.dot_general(a.astype(BF16), b.astype(BF16), (((0,), (0,)), ((), ())),
                           preferred_element_type=F32)


def _split3(x):
    hi = x.astype(BF16)
    r1 = x - hi.astype(F32)
    mid = r1.astype(BF16)
    lo = (r1 - mid.astype(F32)).astype(BF16)
    return hi, mid, lo


def _dot_exact_rhs(a_bf16, x):
    hi, mid, lo = _split3(x)
    d = lambda y: jnp.dot(a_bf16, y, preferred_element_type=F32)
    return d(hi) + d(mid) + d(lo)


def _dot_exact_lhs(x, b_bf16):
    hi, mid, lo = _split3(x)
    d = lambda y: jnp.dot(y, b_bf16, preferred_element_type=F32)
    return d(hi) + d(mid) + d(lo)


def _dot_f32(x, w):
    xh = x.astype(BF16)
    xl = (x - xh.astype(F32)).astype(BF16)
    wh = w.astype(BF16)
    wl = (w - wh.astype(F32)).astype(BF16)
    d = lambda p, q: jnp.dot(p, q, preferred_element_type=F32)
    return d(xh, wh) + d(xl, wh) + d(xh, wl)


def _head_ones():
    r = lax.broadcasted_iota(jnp.int32, (LANES, LANES), 0) // HEAD_DIM
    c = lax.broadcasted_iota(jnp.int32, (LANES, LANES), 1) // HEAD_DIM
    return jnp.where(r == c, 1.0, 0.0).astype(BF16)


def _rmsnorm_kernel(x_ref, g_ref, o_ref):
    x = x_ref[...]
    y = x * lax.rsqrt(jnp.mean(x * x, axis=-1, keepdims=True) + NORM_EPS)
    o_ref[...] = (y * g_ref[...]).astype(o_ref.dtype)


def _rmsnorm(x2d, g, out_dtype, tm=512):
    n, d = x2d.shape
    return pl.pallas_call(
        _rmsnorm_kernel,
        grid=(n // tm,),
        in_specs=[pl.BlockSpec((tm, d), lambda i: (i, 0)),
                  pl.BlockSpec((1, d), lambda i: (0, 0))],
        out_specs=pl.BlockSpec((tm, d), lambda i: (i, 0)),
        out_shape=jax.ShapeDtypeStruct((n, d), out_dtype),
        compiler_params=pltpu.CompilerParams(dimension_semantics=("arbitrary",),
                                             vmem_limit_bytes=VMEM_LIMIT),
        name="rmsnorm",
    )(x2d, g.reshape(1, d))


def _inproj_kernel(h_ref, w_ref, mu_ref, o_ref, *, n_shift_blocks, last_shift_block):
    j = pl.program_id(1)
    acc = jnp.dot(h_ref[0], w_ref[...], preferred_element_type=F32)
    shifted = jnp.logical_or(j < n_shift_blocks, j == last_shift_block)

    @pl.when(shifted)
    def _():
        t = acc.shape[0]
        row = lax.broadcasted_iota(jnp.int32, acc.shape, 0)
        prev = jnp.where(row == 0, 0.0, pltpu.roll(acc, 1, axis=0))
        nxt = jnp.where(row == t - 1, 0.0, pltpu.roll(acc, t - 1, axis=0))
        mu = mu_ref[...]
        o_ref[0] = acc + mu[0:1] * (prev - acc) + mu[1:2] * (nxt - acc)

    @pl.when(jnp.logical_not(shifted))
    def _():
        o_ref[0] = acc


def _inproj(h, w_perm, mu_perm):
    b, t, d = h.shape
    nj = D_IN // PROJ_TN
    kern = functools.partial(_inproj_kernel, n_shift_blocks=(3 * R_WIDTH) // PROJ_TN,
                             last_shift_block=COL_WD // PROJ_TN)
    return pl.pallas_call(
        kern,
        grid=(b, nj),
        in_specs=[pl.BlockSpec((1, t, d), lambda i, j: (i, 0, 0)),
                  pl.BlockSpec((d, PROJ_TN), lambda i, j: (0, j)),
                  pl.BlockSpec((2, PROJ_TN), lambda i, j: (0, j))],
        out_specs=pl.BlockSpec((1, t, PROJ_TN), lambda i, j: (i, 0, j)),
        out_shape=jax.ShapeDtypeStruct((b, t, D_IN), F32),
        compiler_params=pltpu.CompilerParams(dimension_semantics=("arbitrary", "arbitrary"),
                                             vmem_limit_bytes=VMEM_LIMIT),
        name="inproj",
    )(h, w_perm, mu_perm)


def _prep_kernel(xr_ref, xk_ref, xv_ref, wd_ref, ad_ref, w0_ref, wup_ref, a0_ref, aup_ref,
                 kk_ref_in, ka_ref, rk_ref,
                 kk_ref, ke0_ref, ke1_ref, ai0_ref, ai1_ref, lw0_ref, lw1_ref, cv_ref):
    ones = _head_ones()
    r = xr_ref[0]
    k = xk_ref[0]
    v = xv_ref[0]
    wd = jnp.tanh(wd_ref[0])
    ad = ad_ref[0]
    k_a = ka_ref[...]
    kkn = k * kk_ref_in[...]
    ss = _dot_exact_lhs(kkn * kkn, ones)
    kk_ref[0] = kkn * lax.rsqrt(jnp.maximum(ss, KK_EPS))
    ke_sum = None
    for d, (ke_ref, ai_ref, lw_ref) in enumerate(((ke0_ref, ai0_ref, lw0_ref),
                                                  (ke1_ref, ai1_ref, lw1_ref))):
        w_raw = w0_ref[d:d + 1, :] + _dot_f32(wd, wup_ref[d])
        lw_ref[0] = -DECAY_SCALE * jax.nn.sigmoid(w_raw)
        a = jax.nn.sigmoid(a0_ref[d:d + 1, :] + _dot_f32(ad, aup_ref[d]))
        ai_ref[0] = a
        ke = k * (1.0 + (a - 1.0) * k_a)
        ke_ref[0] = ke
        ke_sum = ke if ke_sum is None else ke_sum + ke
    coef = _dot_exact_lhs(r * ke_sum * rk_ref[...], ones)
    cv_ref[0] = coef * v


def _prep(proj, w0, wup_pad, a0, aup_pad, k_k, k_a, r_k, tt=512):
    b, t, _ = proj.shape
    col = lambda base: (lambda i, s, p: (i, s, base // LANES + p))
    fixed = lambda base: (lambda i, s, p: (i, s, base // LANES))
    vec = lambda i, s, p: (0, p)
    lora = lambda i, s, p: (0, 0, p)
    tile = pl.BlockSpec((1, tt, LANES), lambda i, s, p: (i, s, p))
    out_sd = jax.ShapeDtypeStruct((b, t, R_WIDTH), F32)
    return pl.pallas_call(
        _prep_kernel,
        grid=(b, t // tt, PAIRS),
        in_specs=[pl.BlockSpec((1, tt, LANES), col(COL_XR)),
                  pl.BlockSpec((1, tt, LANES), col(COL_XK)),
                  pl.BlockSpec((1, tt, LANES), col(COL_XV)),
                  pl.BlockSpec((1, tt, LANES), fixed(COL_WD)),
                  pl.BlockSpec((1, tt, LANES), fixed(COL_AD)),
                  pl.BlockSpec((2, LANES), vec),
                  pl.BlockSpec((2, LANES, LANES), lora),
                  pl.BlockSpec((2, LANES), vec),
                  pl.BlockSpec((2, LANES, LANES), lora),
                  pl.BlockSpec((1, LANES), vec),
                  pl.BlockSpec((1, LANES), vec),
                  pl.BlockSpec((1, LANES), vec)],
        out_specs=[tile] * 8,
        out_shape=[out_sd] * 8,
        compiler_params=pltpu.CompilerParams(
            dimension_semantics=("arbitrary", "arbitrary", "arbitrary"),
            vmem_limit_bytes=VMEM_LIMIT),
        name="rwkv_prep",
    )(proj, proj, proj, proj, proj, w0, wup_pad, a0, aup_pad, k_k, k_a, r_k)


def _wkv_masks(reverse):
    L = CHUNK
    t = lax.broadcasted_iota(jnp.int32, (L, 2 * L), 0)
    s = lax.broadcasted_iota(jnp.int32, (L, 2 * L), 1) % L
    if reverse:
        strict, incl = t < s, t <= s
    else:
        strict, incl = t > s, t >= s
    tt = lax.broadcasted_iota(jnp.int32, (L, L), 0)
    ss = lax.broadcasted_iota(jnp.int32, (L, L), 1)
    tri = (tt <= ss) if reverse else (tt >= ss)
    tri = jnp.where(tri, 1.0, 0.0).astype(BF16)
    eye2 = jnp.where(t == s, 1.0, 0.0).astype(F32)
    return strict, incl, tri, eye2


def _wkv_chunk(lw, kk, ai, ke, v, r, h, reverse):
    L = CHUNK
    strict, incl, tri, eye2 = _wkv_masks(reverse)
    rr = lax.broadcasted_iota(jnp.int32, (LANES, LANES), 0)
    cc = lax.broadcasted_iota(jnp.int32, (LANES, LANES), 1)
    bd = (rr // HEAD_DIM) == (cc // HEAD_DIM)
    eye = rr == cc

    def bd2(x):
        return jnp.where(bd, jnp.concatenate([x, x], axis=0), 0.0)

    a = -kk
    b = kk * ai
    cin = _dot_exact_rhs(tri, lw)
    cex = cin - lw
    cend = cin[0:1, :] if reverse else cin[L - 1:L, :]
    m = 0.5 * cend
    e_in = jnp.exp(cin - m)
    e_out = jnp.exp(m - cin)
    em = jnp.exp(m)
    at = a * jnp.exp(cex - m)
    rt = r * e_in
    bt = b * e_out
    kt = ke * e_out
    ap, rp, bh, kh = at * em, rt * em, bt * em, kt * em

    ar = jnp.concatenate([at, rt], axis=0)
    g1 = _bdot_nt(ar, bd2(bt))
    g2 = _bdot_nt(ar, bd2(kt))
    a_ab = jnp.where(strict, g1[:L], 0.0)
    p_rb = jnp.where(incl, g1[L:], 0.0)
    a_ak = jnp.where(strict, g2[:L], 0.0)
    p_rk = jnp.where(incl, g2[L:], 0.0)

    n = a_ab
    p = eye2 + n
    n = _bdot(n, bd2(n))
    steps = int(math.log2(L)) - 1
    for i in range(steps):
        nb = bd2(n)
        if i < steps - 1:
            np_ = _bdot(jnp.concatenate([n, p], axis=0), nb)
            n = np_[:L]
            p = p + np_[L:]
        else:
            p = p + _bdot(p, nb)
    tinv = p

    vb = bd2(v)
    x = _bdot(a_ak, vb)
    wu = _bdot(tinv, jnp.concatenate([bd2(ap), bd2(x)], axis=1))
    w, u0 = wu[:, :LANES], wu[:, LANES:]
    pwu = _bdot(p_rb, jnp.concatenate([bd2(w), bd2(u0)], axis=1))
    q = rp + pwu[:, :LANES]
    o0 = pwu[:, LANES:] + _bdot(p_rk, vb)
    mc = _bdot_tn(bh, wu)
    m_off = jnp.where(bd, mc[:, :LANES], 0.0)
    c = jnp.where(bd, mc[:, LANES:] + _bdot_tn(kh, v), 0.0)
    decay_col = jnp.sum(jnp.where(eye, jnp.exp(cend), 0.0), axis=1, keepdims=True)

    qm = _bdot(jnp.concatenate([q, m_off], axis=0), h)
    out = qm[:L] + o0
    h_new = decay_col * h + qm[L:] + c
    return out, h_new


def _wkv_kernel(rf_ref, vf_ref, kkf_ref, kef_ref, aif_ref, lwf_ref,
                rb_ref, vb_ref, kkb_ref, keb_ref, aib_ref, lwb_ref,
                of_ref, ob_ref, h_ref):
    @pl.when(pl.program_id(2) == 0)
    def _():
        h_ref[...] = jnp.zeros_like(h_ref)

    of, hf = _wkv_chunk(lwf_ref[0], kkf_ref[0], aif_ref[0], kef_ref[0], vf_ref[0], rf_ref[0],
                        h_ref[0], reverse=False)
    of_ref[0] = of
    h_ref[0] = hf
    ob, hb = _wkv_chunk(lwb_ref[0], kkb_ref[0], aib_ref[0], keb_ref[0], vb_ref[0], rb_ref[0],
                        h_ref[1], reverse=True)
    ob_ref[0] = ob
    h_ref[1] = hb


def _wkv(proj, kk, ke0, ke1, ai0, ai1, lw0, lw1):
    b, t, _ = proj.shape
    nc = t // CHUNK
    L = CHUNK
    fwd = lambda base: (lambda i, p, c: (i, c, base // LANES + p))
    bwd = lambda base: (lambda i, p, c: (i, nc - 1 - c, base // LANES + p))
    blk = lambda im: pl.BlockSpec((1, L, LANES), im)
    out_sd = jax.ShapeDtypeStruct((b, t, R_WIDTH), F32)
    return pl.pallas_call(
        _wkv_kernel,
        grid=(b, PAIRS, nc),
        in_specs=[blk(fwd(COL_XR)), blk(fwd(COL_XV)), blk(fwd(0)), blk(fwd(0)), blk(fwd(0)), blk(fwd(0)),
                  blk(bwd(COL_XR)), blk(bwd(COL_XV)), blk(bwd(0)), blk(bwd(0)), blk(bwd(0)), blk(bwd(0))],
        out_specs=[blk(fwd(0)), blk(bwd(0))],
        out_shape=[out_sd, out_sd],
        scratch_shapes=[pltpu.VMEM((2, LANES, LANES), F32)],
        compiler_params=pltpu.CompilerParams(
            dimension_semantics=("arbitrary", "arbitrary", "arbitrary"),
            vmem_limit_bytes=VMEM_LIMIT),
        name="wkv",
    )(proj, proj, kk, ke0, ai0, lw0, proj, proj, kk, ke1, ai1, lw1)


def _rope_tables(t):
    rows = t // GRID_W
    row = np.repeat(np.arange(rows, dtype=np.float32), GRID_W)
    colv = np.tile(np.arange(GRID_W, dtype=np.float32), rows)
    axis_dim = HEAD_DIM // 2
    freqs = jnp.asarray(ROPE_THETA, F32) ** (-jnp.arange(0, axis_dim, 2, dtype=F32) / axis_dim)
    ang = jnp.concatenate([jnp.asarray(row)[:, None] * freqs, jnp.asarray(colv)[:, None] * freqs], axis=-1)
    cos = jnp.repeat(jnp.cos(ang), 2, axis=-1)
    sin = jnp.repeat(jnp.sin(ang), 2, axis=-1)
    return cos, sin


def _norm_rope(x, g, cos, sin):
    y = x * lax.rsqrt(jnp.mean(x * x, axis=-1, keepdims=True) + NORM_EPS) * g
    lane = lax.broadcasted_iota(jnp.int32, y.shape, 1)
    even = (lane % 2) == 0
    nxt = pltpu.roll(y, HEAD_DIM - 1, axis=1)
    prv = pltpu.roll(y, 1, axis=1)
    return y * cos + jnp.where(even, -nxt, prv) * sin


def _attn_kernel(q_ref, k_ref, v_ref, cq_ref, sq_ref, ck_ref, sk_ref, qg_ref, kg_ref, o_ref,
                 kn_ref, *, tq):
    @pl.when(pl.program_id(1) == 0)
    def _():
        kf = k_ref[0]
        for hk in range(A_KV_HEADS):
            kh = kf[:, hk * HEAD_DIM:(hk + 1) * HEAD_DIM]
            kn_ref[hk] = _norm_rope(kh, kg_ref[...], ck_ref[...], sk_ref[...]).astype(BF16)

    scale = HEAD_DIM ** -0.5
    qf = q_ref[0]
    vf = v_ref[0]
    cq, sq = cq_ref[...], sq_ref[...]
    outs = []
    for hk in range(A_KV_HEADS):
        qs = []
        for g in range(A_GROUP):
            hq = hk * A_GROUP + g
            qh = qf[:, hq * HEAD_DIM:(hq + 1) * HEAD_DIM]
            qs.append(_norm_rope(qh, qg_ref[...], cq, sq) * scale)
        qst = jnp.concatenate(qs, axis=0)
        s = _bdot_nt(qst, kn_ref[hk])
        mx = jnp.max(s, axis=-1, keepdims=True)
        p = jnp.exp(s - mx)
        l = jnp.sum(p, axis=-1, keepdims=True)
        o = _bdot(p, vf[:, hk * HEAD_DIM:(hk + 1) * HEAD_DIM]) / l
        for g in range(A_GROUP):
            outs.append(o[g * tq:(g + 1) * tq])
    o_ref[0] = jnp.concatenate(outs, axis=1)


def _attention(proj, q_g, k_g, tq=128):
    b, t, _ = proj.shape
    cos, sin = _rope_tables(t)
    kern = functools.partial(_attn_kernel, tq=tq)
    return pl.pallas_call(
        kern,
        grid=(b, t // tq),
        in_specs=[pl.BlockSpec((1, tq, A_WIDTH), lambda i, j: (i, j, COL_Q // A_WIDTH)),
                  pl.BlockSpec((1, t, A_KV_WIDTH), lambda i, j: (i, 0, COL_K // A_KV_WIDTH)),
                  pl.BlockSpec((1, t, A_KV_WIDTH), lambda i, j: (i, 0, COL_V // A_KV_WIDTH)),
                  pl.BlockSpec((tq, HEAD_DIM), lambda i, j: (j, 0)),
                  pl.BlockSpec((tq, HEAD_DIM), lambda i, j: (j, 0)),
                  pl.BlockSpec((t, HEAD_DIM), lambda i, j: (0, 0)),
                  pl.BlockSpec((t, HEAD_DIM), lambda i, j: (0, 0)),
                  pl.BlockSpec((1, HEAD_DIM), lambda i, j: (0, 0)),
                  pl.BlockSpec((1, HEAD_DIM), lambda i, j: (0, 0))],
        out_specs=pl.BlockSpec((1, tq, A_WIDTH), lambda i, j: (i, j, 0)),
        out_shape=jax.ShapeDtypeStruct((b, t, A_WIDTH), F32),
        scratch_shapes=[pltpu.VMEM((A_KV_HEADS, t, HEAD_DIM), BF16)],
        compiler_params=pltpu.CompilerParams(dimension_semantics=("arbitrary", "arbitrary"),
                                             vmem_limit_bytes=VMEM_LIMIT),
        name="attention",
    )(proj, proj, proj, cos, sin, cos, sin, q_g.reshape(1, HEAD_DIM), k_g.reshape(1, HEAD_DIM))


def _post_kernel(x_ref, of_ref, ob_ref, cv_ref, zr_ref, at_ref, za_ref, gr_ref, ga_ref,
                 gnw_ref, gnb_ref, wbr_ref, wba_ref, wout_ref, fg_ref, o_ref):
    ones = _head_ones()
    wkv = of_ref[...] + ob_ref[...]
    parts = []
    for p in range(PAIRS):
        sl = slice(p * LANES, (p + 1) * LANES)
        y = wkv[:, sl]
        mu = _dot_exact_lhs(y, ones) * (1.0 / HEAD_DIM)
        yc = y - mu
        var = _dot_exact_lhs(yc * yc, ones) * (1.0 / HEAD_DIM)
        parts.append(yc * lax.rsqrt(var + GN_EPS))
    gn = jnp.concatenate(parts, axis=1) * gnw_ref[...] + gnb_ref[...]
    zr = zr_ref[...]
    o_r = (gn + cv_ref[...]) * (zr * jax.nn.sigmoid(zr))
    za = za_ref[...]
    o_a = at_ref[...] * (za * jax.nn.sigmoid(za))
    p_r = _bdot(o_r, wbr_ref[...])
    p_a = _bdot(o_a, wba_ref[...])
    merged = jax.nn.sigmoid(gr_ref[...]) * p_r + jax.nn.sigmoid(ga_ref[...]) * p_a
    y = x_ref[...] + _bdot(merged, wout_ref[...])
    yn = y * lax.rsqrt(jnp.mean(y * y, axis=-1, keepdims=True) + NORM_EPS)
    o_ref[...] = yn * fg_ref[...]


def _post(x2d, of, ob, cv, proj2d, attn, gn_w, gn_b, w_br, w_ba, w_out, final_g, tm=256):
    n, d = x2d.shape
    row = lambda w, base: pl.BlockSpec((tm, w), lambda i: (i, base // w))
    const = lambda shape: pl.BlockSpec(shape, lambda i: (0, 0))
    return pl.pallas_call(
        _post_kernel,
        grid=(n // tm,),
        in_specs=[row(d, 0), row(R_WIDTH, 0), row(R_WIDTH, 0), row(R_WIDTH, 0),
                  row(R_WIDTH, COL_ZR), row(A_WIDTH, 0), row(A_WIDTH, COL_ZA),
                  row(d, COL_GR), row(d, COL_GA),
                  const((1, R_WIDTH)), const((1, R_WIDTH)),
                  const((R_WIDTH, d)), const((A_WIDTH, d)), const((d, d)), const((1, d))],
        out_specs=row(d, 0),
        out_shape=jax.ShapeDtypeStruct((n, d), F32),
        compiler_params=pltpu.CompilerParams(dimension_semantics=("arbitrary",),
                                             vmem_limit_bytes=VMEM_LIMIT),
        name="merge_out",
    )(x2d, of, ob, cv, proj2d, attn, proj2d, proj2d, proj2d,
      gn_w.reshape(1, -1), gn_b.reshape(1, -1), w_br, w_ba, w_out, final_g.reshape(1, -1))


def _column_permutation():
    shift_w = 3 * R_WIDTH + 4 * LORA
    o_wd, o_ad = 3 * R_WIDTH, 3 * R_WIDTH + 2 * LORA
    o_zr = shift_w
    o_q = o_zr + R_WIDTH
    o_k = o_q + A_WIDTH
    o_v = o_k + A_KV_WIDTH
    o_za = o_v + A_KV_WIDTH
    o_g = o_za + A_WIDTH
    r = lambda a, w: np.arange(a, a + w)
    order = [r(0, 3 * R_WIDTH), r(o_zr, R_WIDTH), r(o_q, A_WIDTH), r(o_za, A_WIDTH),
             r(o_g, 4096), r(o_k, A_KV_WIDTH), r(o_v, A_KV_WIDTH), r(o_wd, 2 * LORA), r(o_ad, 2 * LORA)]
    perm = np.concatenate(order)
    assert perm.shape[0] == D_IN and np.array_equal(np.sort(perm), np.arange(D_IN))
    return perm, shift_w


def _pad_lora(up):
    z = jnp.zeros_like(up[0])
    return jnp.stack([jnp.concatenate([up[0], z], axis=0), jnp.concatenate([z, up[1]], axis=0)], axis=0)


def _layer(x, norm_g, w_in, shift_mu, w0, w_up, a0, a_up, k_k, k_a, r_k, gn_w, gn_b,
           q_norm_g, k_norm_g, w_branch_rwkv, w_branch_attn, w_out, out_g):
    b, t, d = x.shape
    assert t % CHUNK == 0 and t % GRID_W == 0 and d == 2048 and w_in.shape[1] == D_IN
    perm, shift_w = _column_permutation()
    w_perm = jnp.take(w_in, perm, axis=1).astype(BF16)
    mu_full = jnp.concatenate([shift_mu, jnp.zeros((2, D_IN - shift_w), F32)], axis=1)
    mu_perm = jnp.take(mu_full, perm, axis=1)

    x2d = x.reshape(b * t, d)
    h = _rmsnorm(x2d, norm_g, BF16).reshape(b, t, d)
    proj = _inproj(h, w_perm, mu_perm)
    kk, ke0, ke1, ai0, ai1, lw0, lw1, cv = _prep(
        proj, w0, _pad_lora(w_up), a0, _pad_lora(a_up),
        k_k.reshape(1, -1), k_a.reshape(1, -1), r_k.reshape(1, -1))
    of, ob = _wkv(proj, kk, ke0, ke1, ai0, ai1, lw0, lw1)
    attn = _attention(proj, q_norm_g, k_norm_g)
    n = b * t
    out = _post(x2d, of.reshape(n, -1), ob.reshape(n, -1), cv.reshape(n, -1), proj.reshape(n, D_IN),
                attn.reshape(n, -1), gn_w, gn_b, w_branch_rwkv.astype(BF16), w_branch_attn.astype(BF16),
                w_out.astype(BF16), out_g)
    return out.reshape(b, t, d)


def kernel(x, norm_g, w_in, shift_mu, w0, w_up, a0, a_up, k_k, k_a, r_k, gn_w, gn_b, q_norm_g, k_norm_g,
           w_branch_rwkv, w_branch_attn, w_out, final_norm_g):
    assert norm_g.shape[0] == 1, "single-layer block"
    return _layer(x, norm_g[0], w_in[0], shift_mu[0], w0[0], w_up[0], a0[0], a_up[0], k_k[0], k_a[0],
                  r_k[0], gn_w[0], gn_b[0], q_norm_g[0], k_norm_g[0], w_branch_rwkv[0],
                  w_branch_attn[0], w_out[0], final_norm_g)
```

```python
import functools
import math

import jax
import jax.numpy as jnp
import numpy as np
from jax import lax
from jax.experimental import pallas as pl
from jax.experimental.pallas import tpu as pltpu

F32 = jnp.float32
BF16 = jnp.bfloat16

HEAD_DIM = 64
R_HEADS = 16
R_WIDTH = R_HEADS * HEAD_DIM
LORA = 64
A_Q_HEADS = 16
A_KV_HEADS = 4
A_GROUP = A_Q_HEADS // A_KV_HEADS
A_WIDTH = A_Q_HEADS * HEAD_DIM
A_KV_WIDTH = A_KV_HEADS * HEAD_DIM
GRID_W = 64
ROPE_THETA = 10000.0
NORM_EPS = 1e-6
GN_EPS = 64e-5
KK_EPS = 1e-24
DECAY_SCALE = math.exp(-0.5)

LANES = 128
VMEM_LIMIT = 56 * 1024 * 1024

CHUNK = 64
PAIRS = R_WIDTH // LANES
WKV_PAIRS_PER_STEP = 4

COL_XR, COL_XK, COL_XV, COL_ZR, COL_Q, COL_ZA = 0, 1024, 2048, 3072, 4096, 5120
COL_GR, COL_GA, COL_K, COL_V, COL_WD, COL_AD = 6144, 8192, 10240, 10496, 10752, 10880
D_IN = 11008
PROJ_TN = 256


def _bdot(a, b):
    return jnp.dot(a.astype(BF16), b.astype(BF16), preferred_element_type=F32)


def _bdot_nt(a, b):
    return lax.dot_general(a.astype(BF16), b.astype(BF16), (((1,), (1,)), ((), ())),
                           preferred_element_type=F32)


def _bdot_tn(a, b):
    return lax.dot_general(a.astype(BF16), b.astype(BF16), (((0,), (0,)), ((), ())),
                           preferred_element_type=F32)


def _split3(x):
    hi = x.astype(BF16)
    r1 = x - hi.astype(F32)
    mid = r1.astype(BF16)
    lo = (r1 - mid.astype(F32)).astype(BF16)
    return hi, mid, lo


def _dot_exact_rhs(a_bf16, x):
    hi, mid, lo = _split3(x)
    d = lambda y: jnp.dot(a_bf16, y, preferred_element_type=F32)
    return d(hi) + d(mid) + d(lo)


def _dot_exact_lhs(x, b_bf16):
    hi, mid, lo = _split3(x)
    d = lambda y: jnp.dot(y, b_bf16, preferred_element_type=F32)
    return d(hi) + d(mid) + d(lo)


def _dot_f32(x, w):
    xh = x.astype(BF16)
    xl = (x - xh.astype(F32)).astype(BF16)
    wh = w.astype(BF16)
    wl = (w - wh.astype(F32)).astype(BF16)
    d = lambda p, q: jnp.dot(p, q, preferred_element_type=F32)
    return d(xh, wh) + d(xl, wh) + d(xh, wl)


def _head_ones():
    r = lax.broadcasted_iota(jnp.int32, (LANES, LANES), 0) // HEAD_DIM
    c = lax.broadcasted_iota(jnp.int32, (LANES, LANES), 1) // HEAD_DIM
    return jnp.where(r == c, 1.0, 0.0).astype(BF16)


def _rmsnorm_kernel(x_ref, g_ref, o_ref):
    x = x_ref[...]
    y = x * lax.rsqrt(jnp.mean(x * x, axis=-1, keepdims=True) + NORM_EPS)
    o_ref[...] = (y * g_ref[...]).astype(o_ref.dtype)


def _rmsnorm(x2d, g, out_dtype, tm=512):
    n, d = x2d.shape
    return pl.pallas_call(
        _rmsnorm_kernel,
        grid=(n // tm,),
        in_specs=[pl.BlockSpec((tm, d), lambda i: (i, 0)),
                  pl.BlockSpec((1, d), lambda i: (0, 0))],
        out_specs=pl.BlockSpec((tm, d), lambda i: (i, 0)),
        out_shape=jax.ShapeDtypeStruct((n, d), out_dtype),
        compiler_params=pltpu.CompilerParams(dimension_semantics=("arbitrary",),
                                             vmem_limit_bytes=VMEM_LIMIT),
        name="rmsnorm",
    )(x2d, g.reshape(1, d))


def _inproj_kernel(h_ref, w_ref, mu_ref, o_ref, *, n_shift_blocks, last_shift_block):
    j = pl.program_id(1)
    acc = jnp.dot(h_ref[0], w_ref[...], preferred_element_type=F32)
    shifted = jnp.logical_or(j < n_shift_blocks, j == last_shift_block)

    @pl.when(shifted)
    def _():
        t = acc.shape[0]
        row = lax.broadcasted_iota(jnp.int32, acc.shape, 0)
        prev = jnp.where(row == 0, 0.0, pltpu.roll(acc, 1, axis=0))
        nxt = jnp.where(row == t - 1, 0.0, pltpu.roll(acc, t - 1, axis=0))
        mu = mu_ref[...]
        o_ref[0] = acc + mu[0:1] * (prev - acc) + mu[1:2] * (nxt - acc)

    @pl.when(jnp.logical_not(shifted))
    def _():
        o_ref[0] = acc


def _inproj(h, w_perm, mu_perm):
    b, t, d = h.shape
    nj = D_IN // PROJ_TN
    kern = functools.partial(_inproj_kernel, n_shift_blocks=(3 * R_WIDTH) // PROJ_TN,
                             last_shift_block=COL_WD // PROJ_TN)
    return pl.pallas_call(
        kern,
        grid=(b, nj),
        in_specs=[pl.BlockSpec((1, t, d), lambda i, j: (i, 0, 0)),
                  pl.BlockSpec((d, PROJ_TN), lambda i, j: (0, j)),
                  pl.BlockSpec((2, PROJ_TN), lambda i, j: (0, j))],
        out_specs=pl.BlockSpec((1, t, PROJ_TN), lambda i, j: (i, 0, j)),
        out_shape=jax.ShapeDtypeStruct((b, t, D_IN), F32),
        compiler_params=pltpu.CompilerParams(dimension_semantics=("arbitrary", "arbitrary"),
                                             vmem_limit_bytes=VMEM_LIMIT),
        name="inproj",
    )(h, w_perm, mu_perm)


def _prep_kernel(xr_ref, xk_ref, xv_ref, wd_ref, ad_ref, w0_ref, wup_ref, a0_ref, aup_ref,
                 kk_ref_in, ka_ref, rk_ref,
                 kk_ref, ke0_ref, ke1_ref, ai0_ref, ai1_ref, lw0_ref, lw1_ref, cv_ref):
    ones = _head_ones()
    r = xr_ref[0]
    k = xk_ref[0]
    v = xv_ref[0]
    wd = jnp.tanh(wd_ref[0])
    ad = ad_ref[0]
    k_a = ka_ref[...]
    kkn = k * kk_ref_in[...]
    ss = _dot_exact_lhs(kkn * kkn, ones)
    kk_ref[0] = kkn * lax.rsqrt(jnp.maximum(ss, KK_EPS))
    ke_sum = None
    for d, (ke_ref, ai_ref, lw_ref) in enumerate(((ke0_ref, ai0_ref, lw0_ref),
                                                  (ke1_ref, ai1_ref, lw1_ref))):
        w_raw = w0_ref[d:d + 1, :] + _dot_f32(wd, wup_ref[d])
        lw_ref[0] = -DECAY_SCALE * jax.nn.sigmoid(w_raw)
        a = jax.nn.sigmoid(a0_ref[d:d + 1, :] + _dot_f32(ad, aup_ref[d]))
        ai_ref[0] = a
        ke = k * (1.0 + (a - 1.0) * k_a)
        ke_ref[0] = ke
        ke_sum = ke if ke_sum is None else ke_sum + ke
    coef = _dot_exact_lhs(r * ke_sum * rk_ref[...], ones)
    cv_ref[0] = coef * v


def _prep(proj, w0, wup_pad, a0, aup_pad, k_k, k_a, r_k, tt=512):
    b, t, _ = proj.shape
    col = lambda base: (lambda i, s, p: (i, s, base // LANES + p))
    fixed = lambda base: (lambda i, s, p: (i, s, base // LANES))
    vec = lambda i, s, p: (0, p)
    lora = lambda i, s, p: (0, 0, p)
    tile = pl.BlockSpec((1, tt, LANES), lambda i, s, p: (i, s, p))
    out_sd = jax.ShapeDtypeStruct((b, t, R_WIDTH), F32)
    return pl.pallas_call(
        _prep_kernel,
        grid=(b, t // tt, PAIRS),
        in_specs=[pl.BlockSpec((1, tt, LANES), col(COL_XR)),
                  pl.BlockSpec((1, tt, LANES), col(COL_XK)),
                  pl.BlockSpec((1, tt, LANES), col(COL_XV)),
                  pl.BlockSpec((1, tt, LANES), fixed(COL_WD)),
                  pl.BlockSpec((1, tt, LANES), fixed(COL_AD)),
                  pl.BlockSpec((2, LANES), vec),
                  pl.BlockSpec((2, LANES, LANES), lora),
                  pl.BlockSpec((2, LANES), vec),
                  pl.BlockSpec((2, LANES, LANES), lora),
                  pl.BlockSpec((1, LANES), vec),
                  pl.BlockSpec((1, LANES), vec),
                  pl.BlockSpec((1, LANES), vec)],
        out_specs=[tile] * 8,
        out_shape=[out_sd] * 8,
        compiler_params=pltpu.CompilerParams(
            dimension_semantics=("arbitrary", "arbitrary", "arbitrary"),
            vmem_limit_bytes=VMEM_LIMIT),
        name="rwkv_prep",
    )(proj, proj, proj, proj, proj, w0, wup_pad, a0, aup_pad, k_k, k_a, r_k)


def _wkv_consts(reverse):
    L = CHUNK
    t = lax.broadcasted_iota(jnp.int32, (L, 2 * L), 0)
    s = lax.broadcasted_iota(jnp.int32, (L, 2 * L), 1) % L
    if reverse:
        strict, incl = t < s, t <= s
    else:
        strict, incl = t > s, t >= s
    tt = lax.broadcasted_iota(jnp.int32, (L, L), 0)
    ss = lax.broadcasted_iota(jnp.int32, (L, L), 1)
    tri = (tt <= ss) if reverse else (tt >= ss)
    tri = jnp.where(tri, 1.0, 0.0).astype(BF16)
    eye2 = jnp.where(t == s, 1.0, 0.0).astype(F32)
    rr = lax.broadcasted_iota(jnp.int32, (LANES, LANES), 0)
    cc = lax.broadcasted_iota(jnp.int32, (LANES, LANES), 1)
    bd = (rr // HEAD_DIM) == (cc // HEAD_DIM)
    eye = rr == cc
    return strict, incl, tri, eye2, bd, eye


def _wkv_chunk(load, consts, reverse):
    L = CHUNK
    strict, incl, tri, eye2, bd, eye = consts

    def bd2(x):
        return jnp.where(bd, jnp.concatenate([x, x], axis=0), 0.0)

    lw, kk, ai, ke, v, r, h = load()
    a = -kk
    b = kk * ai
    cin = _dot_exact_rhs(tri, lw)
    yield
    cex = cin - lw
    cend = cin[0:1, :] if reverse else cin[L - 1:L, :]
    m = 0.5 * cend
    e_in = jnp.exp(cin - m)
    e_out = jnp.exp(m - cin)
    em = jnp.exp(m)
    at = a * jnp.exp(cex - m)
    rt = r * e_in
    bt = b * e_out
    kt = ke * e_out
    ap, rp, bh, kh = at * em, rt * em, bt * em, kt * em

    ar = jnp.concatenate([at, rt], axis=0)
    g1 = _bdot_nt(ar, bd2(bt))
    g2 = _bdot_nt(ar, bd2(kt))
    yield
    a_ab = jnp.where(strict, g1[:L], 0.0)
    p_rb = jnp.where(incl, g1[L:], 0.0)
    a_ak = jnp.where(strict, g2[:L], 0.0)
    p_rk = jnp.where(incl, g2[L:], 0.0)

    n = a_ab
    p = eye2 + n
    n = _bdot(n, bd2(n))
    vb = bd2(v)
    x = _bdot(a_ak, vb)
    yield
    steps = int(math.log2(L)) - 1
    for i in range(steps):
        nb = bd2(n)
        if i < steps - 1:
            np_ = _bdot(jnp.concatenate([n, p], axis=0), nb)
            n = np_[:L]
            p = p + np_[L:]
        else:
            p = p + _bdot(p, nb)
        yield
    tinv = p

    wu = _bdot(tinv, jnp.concatenate([bd2(ap), bd2(x)], axis=1))
    yield
    w, u0 = wu[:, :LANES], wu[:, LANES:]
    pwu = _bdot(p_rb, jnp.concatenate([bd2(w), bd2(u0)], axis=1))
    q = rp + pwu[:, :LANES]
    o0 = pwu[:, LANES:] + _bdot(p_rk, vb)
    mc = _bdot_tn(bh, wu)
    m_off = jnp.where(bd, mc[:, :LANES], 0.0)
    c = jnp.where(bd, mc[:, LANES:] + _bdot_tn(kh, v), 0.0)
    decay_col = jnp.sum(jnp.where(eye, jnp.exp(cend), 0.0), axis=1, keepdims=True)
    yield
    qm = _bdot(jnp.concatenate([q, m_off], axis=0), h)
    out = qm[:L] + o0
    h_new = decay_col * h + qm[L:] + c
    return out, h_new


def _interleave(gens):
    results = [None] * len(gens)
    live = list(range(len(gens)))
    while live:
        for i in list(live):
            try:
                next(gens[i])
            except StopIteration as stop:
                results[i] = stop.value
                live.remove(i)
    return results


def _wkv_kernel(rf_ref, vf_ref, kkf_ref, kef_ref, aif_ref, lwf_ref,
                rb_ref, vb_ref, kkb_ref, keb_ref, aib_ref, lwb_ref,
                of_ref, ob_ref, h_ref):
    @pl.when(pl.program_id(2) == 0)
    def _():
        h_ref[...] = jnp.zeros_like(h_ref)

    dirs = ((lwf_ref, kkf_ref, aif_ref, kef_ref, vf_ref, rf_ref, of_ref),
            (lwb_ref, kkb_ref, aib_ref, keb_ref, vb_ref, rb_ref, ob_ref))
    consts = (_wkv_consts(False), _wkv_consts(True))
    gens, sinks = [], []
    for i in range(WKV_PAIRS_PER_STEP):
        sl = slice(i * LANES, (i + 1) * LANES)
        for d in range(2):
            refs = dirs[d]
            load = functools.partial(
                lambda refs, d, i, sl: tuple(ref[0, :, sl] for ref in refs[:6]) + (h_ref[d, i],),
                refs, d, i, sl)
            gens.append(_wkv_chunk(load, consts[d], reverse=(d == 1)))
            sinks.append((refs[6], d, i, sl))
    for (out, h_new), (o_ref, d, i, sl) in zip(_interleave(gens), sinks):
        o_ref[0, :, sl] = out
        h_ref[d, i] = h_new


def _wkv(proj, kk, ke0, ke1, ai0, ai1, lw0, lw1):
    b, t, _ = proj.shape
    nc = t // CHUNK
    L = CHUNK
    w = WKV_PAIRS_PER_STEP * LANES
    fwd = lambda base: (lambda i, p, c: (i, c, base // w + p))
    bwd = lambda base: (lambda i, p, c: (i, nc - 1 - c, base // w + p))
    blk = lambda im: pl.BlockSpec((1, L, w), im)
    out_sd = jax.ShapeDtypeStruct((b, t, R_WIDTH), F32)
    return pl.pallas_call(
        _wkv_kernel,
        grid=(b, PAIRS // WKV_PAIRS_PER_STEP, nc),
        in_specs=[blk(fwd(COL_XR)), blk(fwd(COL_XV)), blk(fwd(0)), blk(fwd(0)), blk(fwd(0)), blk(fwd(0)),
                  blk(bwd(COL_XR)), blk(bwd(COL_XV)), blk(bwd(0)), blk(bwd(0)), blk(bwd(0)), blk(bwd(0))],
        out_specs=[blk(fwd(0)), blk(bwd(0))],
        out_shape=[out_sd, out_sd],
        scratch_shapes=[pltpu.VMEM((2, WKV_PAIRS_PER_STEP, LANES, LANES), F32)],
        compiler_params=pltpu.CompilerParams(
            dimension_semantics=("arbitrary", "arbitrary", "arbitrary"),
            vmem_limit_bytes=VMEM_LIMIT),
        name="wkv",
    )(proj, proj, kk, ke0, ai0, lw0, proj, proj, kk, ke1, ai1, lw1)


def _rope_tables(t):
    rows = t // GRID_W
    row = np.repeat(np.arange(rows, dtype=np.float32), GRID_W)
    colv = np.tile(np.arange(GRID_W, dtype=np.float32), rows)
    axis_dim = HEAD_DIM // 2
    freqs = jnp.asarray(ROPE_THETA, F32) ** (-jnp.arange(0, axis_dim, 2, dtype=F32) / axis_dim)
    ang = jnp.concatenate([jnp.asarray(row)[:, None] * freqs, jnp.asarray(colv)[:, None] * freqs], axis=-1)
    cos = jnp.repeat(jnp.cos(ang), 2, axis=-1)
    sin = jnp.repeat(jnp.sin(ang), 2, axis=-1)
    return cos, sin


def _norm_rope(x, g, cos, sin):
    y = x * lax.rsqrt(jnp.mean(x * x, axis=-1, keepdims=True) + NORM_EPS) * g
    lane = lax.broadcasted_iota(jnp.int32, y.shape, 1)
    even = (lane % 2) == 0
    nxt = pltpu.roll(y, HEAD_DIM - 1, axis=1)
    prv = pltpu.roll(y, 1, axis=1)
    return y * cos + jnp.where(even, -nxt, prv) * sin


def _attn_kernel(q_ref, k_ref, v_ref, cq_ref, sq_ref, ck_ref, sk_ref, qg_ref, kg_ref, o_ref,
                 kn_ref, *, tq):
    @pl.when(pl.program_id(1) == 0)
    def _():
        kf = k_ref[0]
        for hk in range(A_KV_HEADS):
            kh = kf[:, hk * HEAD_DIM:(hk + 1) * HEAD_DIM]
            kn_ref[hk] = _norm_rope(kh, kg_ref[...], ck_ref[...], sk_ref[...]).astype(BF16)

    scale = HEAD_DIM ** -0.5
    qf = q_ref[0]
    vf = v_ref[0]
    cq, sq = cq_ref[...], sq_ref[...]
    outs = []
    for hk in range(A_KV_HEADS):
        qs = []
        for g in range(A_GROUP):
            hq = hk * A_GROUP + g
            qh = qf[:, hq * HEAD_DIM:(hq + 1) * HEAD_DIM]
            qs.append(_norm_rope(qh, qg_ref[...], cq, sq) * scale)
        qst = jnp.concatenate(qs, axis=0)
        s = _bdot_nt(qst, kn_ref[hk])
        mx = jnp.max(s, axis=-1, keepdims=True)
        p = jnp.exp(s - mx)
        l = jnp.sum(p, axis=-1, keepdims=True)
        o = _bdot(p, vf[:, hk * HEAD_DIM:(hk + 1) * HEAD_DIM]) / l
        for g in range(A_GROUP):
            outs.append(o[g * tq:(g + 1) * tq])
    o_ref[0] = jnp.concatenate(outs, axis=1)


def _attention(proj, q_g, k_g, tq=128):
    b, t, _ = proj.shape
    cos, sin = _rope_tables(t)
    kern = functools.partial(_attn_kernel, tq=tq)
    return pl.pallas_call(
        kern,
        grid=(b, t // tq),
        in_specs=[pl.BlockSpec((1, tq, A_WIDTH), lambda i, j: (i, j, COL_Q // A_WIDTH)),
                  pl.BlockSpec((1, t, A_KV_WIDTH), lambda i, j: (i, 0, COL_K // A_KV_WIDTH)),
                  pl.BlockSpec((1, t, A_KV_WIDTH), lambda i, j: (i, 0, COL_V // A_KV_WIDTH)),
                  pl.BlockSpec((tq, HEAD_DIM), lambda i, j: (j, 0)),
                  pl.BlockSpec((tq, HEAD_DIM), lambda i, j: (j, 0)),
                  pl.BlockSpec((t, HEAD_DIM), lambda i, j: (0, 0)),
                  pl.BlockSpec((t, HEAD_DIM), lambda i, j: (0, 0)),
                  pl.BlockSpec((1, HEAD_DIM), lambda i, j: (0, 0)),
                  pl.BlockSpec((1, HEAD_DIM), lambda i, j: (0, 0))],
        out_specs=pl.BlockSpec((1, tq, A_WIDTH), lambda i, j: (i, j, 0)),
        out_shape=jax.ShapeDtypeStruct((b, t, A_WIDTH), F32),
        scratch_shapes=[pltpu.VMEM((A_KV_HEADS, t, HEAD_DIM), BF16)],
        compiler_params=pltpu.CompilerParams(dimension_semantics=("arbitrary", "arbitrary"),
                                             vmem_limit_bytes=VMEM_LIMIT),
        name="attention",
    )(proj, proj, proj, cos, sin, cos, sin, q_g.reshape(1, HEAD_DIM), k_g.reshape(1, HEAD_DIM))


def _post_kernel(x_ref, of_ref, ob_ref, cv_ref, zr_ref, at_ref, za_ref, gr_ref, ga_ref,
                 gnw_ref, gnb_ref, wbr_ref, wba_ref, wout_ref, fg_ref, o_ref):
    ones = _head_ones()
    wkv = of_ref[...] + ob_ref[...]
    parts = []
    for p in range(PAIRS):
        sl = slice(p * LANES, (p + 1) * LANES)
        y = wkv[:, sl]
        mu = _dot_exact_lhs(y, ones) * (1.0 / HEAD_DIM)
        yc = y - mu
        var = _dot_exact_lhs(yc * yc, ones) * (1.0 / HEAD_DIM)
        parts.append(yc * lax.rsqrt(var + GN_EPS))
    gn = jnp.concatenate(parts, axis=1) * gnw_ref[...] + gnb_ref[...]
    zr = zr_ref[...]
    o_r = (gn + cv_ref[...]) * (zr * jax.nn.sigmoid(zr))
    za = za_ref[...]
    o_a = at_ref[...] * (za * jax.nn.sigmoid(za))
    p_r = _bdot(o_r, wbr_ref[...])
    p_a = _bdot(o_a, wba_ref[...])
    merged = jax.nn.sigmoid(gr_ref[...]) * p_r + jax.nn.sigmoid(ga_ref[...]) * p_a
    y = x_ref[...] + _bdot(merged, wout_ref[...])
    yn = y * lax.rsqrt(jnp.mean(y * y, axis=-1, keepdims=True) + NORM_EPS)
    o_ref[...] = yn * fg_ref[...]


def _post(x2d, of, ob, cv, proj2d, attn, gn_w, gn_b, w_br, w_ba, w_out, final_g, tm=256):
    n, d = x2d.shape
    row = lambda w, base: pl.BlockSpec((tm, w), lambda i: (i, base // w))
    const = lambda shape: pl.BlockSpec(shape, lambda i: (0, 0))
    return pl.pallas_call(
        _post_kernel,
        grid=(n // tm,),
        in_specs=[row(d, 0), row(R_WIDTH, 0), row(R_WIDTH, 0), row(R_WIDTH, 0),
                  row(R_WIDTH, COL_ZR), row(A_WIDTH, 0), row(A_WIDTH, COL_ZA),
                  row(d, COL_GR), row(d, COL_GA),
                  const((1, R_WIDTH)), const((1, R_WIDTH)),
                  const((R_WIDTH, d)), const((A_WIDTH, d)), const((d, d)), const((1, d))],
        out_specs=row(d, 0),
        out_shape=jax.ShapeDtypeStruct((n, d), F32),
        compiler_params=pltpu.CompilerParams(dimension_semantics=("arbitrary",),
                                             vmem_limit_bytes=VMEM_LIMIT),
        name="merge_out",
    )(x2d, of, ob, cv, proj2d, attn, proj2d, proj2d, proj2d,
      gn_w.reshape(1, -1), gn_b.reshape(1, -1), w_br, w_ba, w_out, final_g.reshape(1, -1))


def _column_permutation():
    shift_w = 3 * R_WIDTH + 4 * LORA
    o_wd, o_ad = 3 * R_WIDTH, 3 * R_WIDTH + 2 * LORA
    o_zr = shift_w
    o_q = o_zr + R_WIDTH
    o_k = o_q + A_WIDTH
    o_v = o_k + A_KV_WIDTH
    o_za = o_v + A_KV_WIDTH
    o_g = o_za + A_WIDTH
    r = lambda a, w: np.arange(a, a + w)
    order = [r(0, 3 * R_WIDTH), r(o_zr, R_WIDTH), r(o_q, A_WIDTH), r(o_za, A_WIDTH),
             r(o_g, 4096), r(o_k, A_KV_WIDTH), r(o_v, A_KV_WIDTH), r(o_wd, 2 * LORA), r(o_ad, 2 * LORA)]
    perm = np.concatenate(order)
    assert perm.shape[0] == D_IN and np.array_equal(np.sort(perm), np.arange(D_IN))
    return perm, shift_w


def _pad_lora(up):
    z = jnp.zeros_like(up[0])
    return jnp.stack([jnp.concatenate([up[0], z], axis=0), jnp.concatenate([z, up[1]], axis=0)], axis=0)


def _layer(x, norm_g, w_in, shift_mu, w0, w_up, a0, a_up, k_k, k_a, r_k, gn_w, gn_b,
           q_norm_g, k_norm_g, w_branch_rwkv, w_branch_attn, w_out, out_g):
    b, t, d = x.shape
    assert t % CHUNK == 0 and t % GRID_W == 0 and d == 2048 and w_in.shape[1] == D_IN
    perm, shift_w = _column_permutation()
    w_perm = jnp.take(w_in, perm, axis=1).astype(BF16)
    mu_full = jnp.concatenate([shift_mu, jnp.zeros((2, D_IN - shift_w), F32)], axis=1)
    mu_perm = jnp.take(mu_full, perm, axis=1)

    x2d = x.reshape(b * t, d)
    h = _rmsnorm(x2d, norm_g, BF16).reshape(b, t, d)
    proj = _inproj(h, w_perm, mu_perm)
    kk, ke0, ke1, ai0, ai1, lw0, lw1, cv = _prep(
        proj, w0, _pad_lora(w_up), a0, _pad_lora(a_up),
        k_k.reshape(1, -1), k_a.reshape(1, -1), r_k.reshape(1, -1))
    of, ob = _wkv(proj, kk, ke0, ke1, ai0, ai1, lw0, lw1)
    attn = _attention(proj, q_norm_g, k_norm_g)
    n = b * t
    out = _post(x2d, of.reshape(n, -1), ob.reshape(n, -1), cv.reshape(n, -1), proj.reshape(n, D_IN),
                attn.reshape(n, -1), gn_w, gn_b, w_branch_rwkv.astype(BF16), w_branch_attn.astype(BF16),
                w_out.astype(BF16), out_g)
    return out.reshape(b, t, d)


def kernel(x, norm_g, w_in, shift_mu, w0, w_up, a0, a_up, k_k, k_a, r_k, gn_w, gn_b, q_norm_g, k_norm_g,
           w_branch_rwkv, w_branch_attn, w_out, final_norm_g):
    assert norm_g.shape[0] == 1, "single-layer block"
    return _layer(x, norm_g[0], w_in[0], shift_mu[0], w0[0], w_up[0], a0[0], a_up[0], k_k[0], k_a[0],
                  r_k[0], gn_w[0], gn_b[0], q_norm_g[0], k_norm_g[0], w_branch_rwkv[0],
                  w_branch_attn[0], w_out[0], final_norm_g)
```

```python
import functools
import math

import jax
import jax.numpy as jnp
import numpy as np
from jax import lax
from jax.experimental import pallas as pl
from jax.experimental.pallas import tpu as pltpu

F32 = jnp.float32
BF16 = jnp.bfloat16

HEAD_DIM = 64
R_HEADS = 16
R_WIDTH = R_HEADS * HEAD_DIM
LORA = 64
A_Q_HEADS = 16
A_KV_HEADS = 4
A_GROUP = A_Q_HEADS // A_KV_HEADS
A_WIDTH = A_Q_HEADS * HEAD_DIM
A_KV_WIDTH = A_KV_HEADS * HEAD_DIM
GRID_W = 64
ROPE_THETA = 10000.0
NORM_EPS = 1e-6
GN_EPS = 64e-5
KK_EPS = 1e-24
DECAY_SCALE = math.exp(-0.5)

LANES = 128
VMEM_LIMIT = 56 * 1024 * 1024

CHUNK = 64
PAIRS = R_WIDTH // LANES
WKV_PAIRS_PER_STEP = 4

COL_XR, COL_XK, COL_XV, COL_ZR, COL_Q, COL_ZA = 0, 1024, 2048, 3072, 4096, 5120
COL_GR, COL_GA, COL_K, COL_V, COL_WD, COL_AD = 6144, 8192, 10240, 10496, 10752, 10880
D_IN = 11008
PROJ_TN = 256


def _bdot(a, b):
    return jnp.dot(a.astype(BF16), b.astype(BF16), preferred_element_type=F32)


def _bdot_nt(a, b):
    return lax.dot_general(a.astype(BF16), b.astype(BF16), (((1,), (1,)), ((), ())),
                           preferred_element_type=F32)


def _bdot_tn(a, b):
    return lax.dot_general(a.astype(BF16), b.astype(BF16), (((0,), (0,)), ((), ())),
                           preferred_element_type=F32)


def _split3(x):
    hi = x.astype(BF16)
    r1 = x - hi.astype(F32)
    mid = r1.astype(BF16)
    lo = (r1 - mid.astype(F32)).astype(BF16)
    return hi, mid, lo


def _dot_exact_rhs(a_bf16, x):
    hi, mid, lo = _split3(x)
    d = lambda y: jnp.dot(a_bf16, y, preferred_element_type=F32)
    return d(hi) + d(mid) + d(lo)


def _dot_exact_lhs(x, b_bf16):
    hi, mid, lo = _split3(x)
    d = lambda y: jnp.dot(y, b_bf16, preferred_element_type=F32)
    return d(hi) + d(mid) + d(lo)


def _dot_f32(x, w):
    xh = x.astype(BF16)
    xl = (x - xh.astype(F32)).astype(BF16)
    wh = w.astype(BF16)
    wl = (w - wh.astype(F32)).astype(BF16)
    d = lambda p, q: jnp.dot(p, q, preferred_element_type=F32)
    return d(xh, wh) + d(xl, wh) + d(xh, wl)


def _head_ones(width=LANES):
    r = lax.broadcasted_iota(jnp.int32, (width, width), 0) // HEAD_DIM
    c = lax.broadcasted_iota(jnp.int32, (width, width), 1) // HEAD_DIM
    return jnp.where(r == c, 1.0, 0.0).astype(BF16)


def _rmsnorm_kernel(x_ref, g_ref, o_ref):
    x = x_ref[...]
    y = x * lax.rsqrt(jnp.mean(x * x, axis=-1, keepdims=True) + NORM_EPS)
    o_ref[...] = (y * g_ref[...]).astype(o_ref.dtype)


def _rmsnorm(x2d, g, out_dtype, tm=512):
    n, d = x2d.shape
    return pl.pallas_call(
        _rmsnorm_kernel,
        grid=(n // tm,),
        in_specs=[pl.BlockSpec((tm, d), lambda i: (i, 0)),
                  pl.BlockSpec((1, d), lambda i: (0, 0))],
        out_specs=pl.BlockSpec((tm, d), lambda i: (i, 0)),
        out_shape=jax.ShapeDtypeStruct((n, d), out_dtype),
        compiler_params=pltpu.CompilerParams(dimension_semantics=("arbitrary",),
                                             vmem_limit_bytes=VMEM_LIMIT),
        name="rmsnorm",
    )(x2d, g.reshape(1, d))


def _rope_tables(t):
    rows = t // GRID_W
    row = np.repeat(np.arange(rows, dtype=np.float32), GRID_W)
    colv = np.tile(np.arange(GRID_W, dtype=np.float32), rows)
    axis_dim = HEAD_DIM // 2
    freqs = jnp.asarray(ROPE_THETA, F32) ** (-jnp.arange(0, axis_dim, 2, dtype=F32) / axis_dim)
    ang = jnp.concatenate([jnp.asarray(row)[:, None] * freqs, jnp.asarray(colv)[:, None] * freqs], axis=-1)
    reps = PROJ_TN // HEAD_DIM
    cos = jnp.tile(jnp.repeat(jnp.cos(ang), 2, axis=-1), (1, reps))
    sin = jnp.tile(jnp.repeat(jnp.sin(ang), 2, axis=-1), (1, reps))
    return cos, sin


def _inproj_kernel(h_ref, w_ref, mu_ref, g_ref, cos_ref, sin_ref, o_ref, lora_ref, *,
                   shift_blocks, lora_block, q_blocks, k_block):
    j = pl.program_id(1)
    acc = jnp.dot(h_ref[0], w_ref[...], preferred_element_type=F32)
    is_lora = j == lora_block
    is_shift = jnp.logical_or(j < shift_blocks, is_lora)
    is_rope = jnp.logical_or(jnp.logical_and(j >= q_blocks[0], j < q_blocks[1]), j == k_block)

    def shifted():
        t = acc.shape[0]
        row = lax.broadcasted_iota(jnp.int32, acc.shape, 0)
        prev = jnp.where(row == 0, 0.0, pltpu.roll(acc, 1, axis=0))
        nxt = jnp.where(row == t - 1, 0.0, pltpu.roll(acc, t - 1, axis=0))
        mu = mu_ref[...]
        return acc + mu[0:1] * (prev - acc) + mu[1:2] * (nxt - acc)

    @pl.when(jnp.logical_and(is_shift, jnp.logical_not(is_lora)))
    def _():
        o_ref[0] = shifted().astype(o_ref.dtype)

    @pl.when(is_lora)
    def _():
        y = shifted()
        o_ref[0] = y.astype(o_ref.dtype)
        lora_ref[0] = y

    @pl.when(is_rope)
    def _():
        ss = _dot_exact_lhs(acc * acc, _head_ones(PROJ_TN)) * (1.0 / HEAD_DIM)
        y = acc * lax.rsqrt(ss + NORM_EPS) * g_ref[...]
        lane = lax.broadcasted_iota(jnp.int32, y.shape, 1)
        even = (lane % 2) == 0
        nxt = pltpu.roll(y, PROJ_TN - 1, axis=1)
        prv = pltpu.roll(y, 1, axis=1)
        o_ref[0] = (y * cos_ref[...] + jnp.where(even, -nxt, prv) * sin_ref[...]).astype(o_ref.dtype)

    @pl.when(jnp.logical_not(jnp.logical_or(is_shift, is_rope)))
    def _():
        o_ref[0] = acc.astype(o_ref.dtype)


def _inproj(h, w_perm, mu_perm, g_perm):
    b, t, d = h.shape
    nj = D_IN // PROJ_TN
    cos, sin = _rope_tables(t)
    kern = functools.partial(
        _inproj_kernel, shift_blocks=(3 * R_WIDTH) // PROJ_TN, lora_block=COL_WD // PROJ_TN,
        q_blocks=(COL_Q // PROJ_TN, (COL_Q + A_WIDTH) // PROJ_TN), k_block=COL_K // PROJ_TN)
    return pl.pallas_call(
        kern,
        grid=(b, nj),
        in_specs=[pl.BlockSpec((1, t, d), lambda i, j: (i, 0, 0)),
                  pl.BlockSpec((d, PROJ_TN), lambda i, j: (0, j)),
                  pl.BlockSpec((2, PROJ_TN), lambda i, j: (0, j)),
                  pl.BlockSpec((1, PROJ_TN), lambda i, j: (0, j)),
                  pl.BlockSpec((t, PROJ_TN), lambda i, j: (0, 0)),
                  pl.BlockSpec((t, PROJ_TN), lambda i, j: (0, 0))],
        out_specs=[pl.BlockSpec((1, t, PROJ_TN), lambda i, j: (i, 0, j)),
                   pl.BlockSpec((1, t, PROJ_TN), lambda i, j: (i, 0, 0))],
        out_shape=[jax.ShapeDtypeStruct((b, t, D_IN), BF16),
                   jax.ShapeDtypeStruct((b, t, PROJ_TN), F32)],
        compiler_params=pltpu.CompilerParams(dimension_semantics=("arbitrary", "arbitrary"),
                                             vmem_limit_bytes=VMEM_LIMIT),
        name="inproj",
    )(h, w_perm, mu_perm, g_perm, cos, sin)


def _prep_kernel(xr_ref, xk_ref, xv_ref, lora_ref, w0_ref, wup_ref, a0_ref, aup_ref,
                 kkg_ref, ka_ref, rk_ref,
                 kk_ref, ke0_ref, ke1_ref, bb0_ref, bb1_ref, cin0_ref, cin1_ref, cv_ref):
    L = CHUNK
    ones = _head_ones()
    r = xr_ref[0].astype(F32)
    k = xk_ref[0].astype(F32)
    v = xv_ref[0].astype(F32)
    lora = lora_ref[0]
    wd = jnp.tanh(lora[:, :LANES])
    ad = lora[:, LANES:]
    k_a = ka_ref[...]
    kkn = k * kkg_ref[...]
    ss = _dot_exact_lhs(kkn * kkn, ones)
    kk = kkn * lax.rsqrt(jnp.maximum(ss, KK_EPS))
    kk_ref[0] = kk.astype(kk_ref.dtype)
    tt_ = lax.broadcasted_iota(jnp.int32, (L, L), 0)
    ss_ = lax.broadcasted_iota(jnp.int32, (L, L), 1)
    tri = (jnp.where(tt_ >= ss_, 1.0, 0.0).astype(BF16), jnp.where(tt_ <= ss_, 1.0, 0.0).astype(BF16))
    ke_sum = None
    for d, (ke_ref, bb_ref, cin_ref) in enumerate(((ke0_ref, bb0_ref, cin0_ref),
                                                   (ke1_ref, bb1_ref, cin1_ref))):
        w_raw = w0_ref[d:d + 1, :] + _dot_f32(wd, wup_ref[d])
        lw = -DECAY_SCALE * jax.nn.sigmoid(w_raw)
        for c in range(lw.shape[0] // L):
            cin_ref[0, c * L:(c + 1) * L, :] = _dot_exact_rhs(tri[d], lw[c * L:(c + 1) * L, :])
        a = jax.nn.sigmoid(a0_ref[d:d + 1, :] + _dot_f32(ad, aup_ref[d]))
        bb_ref[0] = (kk * a).astype(bb_ref.dtype)
        ke = k * (1.0 + (a - 1.0) * k_a)
        ke_ref[0] = ke.astype(ke_ref.dtype)
        ke_sum = ke if ke_sum is None else ke_sum + ke
    coef = _dot_exact_lhs(r * ke_sum * rk_ref[...], ones)
    cv_ref[0] = (coef * v).astype(cv_ref.dtype)


def _prep(proj, lora, w0, wup_pad, a0, aup_pad, k_k, k_a, r_k, tt=512):
    b, t, _ = proj.shape
    col = lambda base: (lambda i, s, p: (i, s, base // LANES + p))
    vec = lambda i, s, p: (0, p)
    lo = lambda i, s, p: (0, 0, p)
    tile = pl.BlockSpec((1, tt, LANES), lambda i, s, p: (i, s, p))
    sd = lambda dt: jax.ShapeDtypeStruct((b, t, R_WIDTH), dt)
    return pl.pallas_call(
        _prep_kernel,
        grid=(b, t // tt, PAIRS),
        in_specs=[pl.BlockSpec((1, tt, LANES), col(COL_XR)),
                  pl.BlockSpec((1, tt, LANES), col(COL_XK)),
                  pl.BlockSpec((1, tt, LANES), col(COL_XV)),
                  pl.BlockSpec((1, tt, 2 * LANES), lambda i, s, p: (i, s, 0)),
                  pl.BlockSpec((2, LANES), vec),
                  pl.BlockSpec((2, LANES, LANES), lo),
                  pl.BlockSpec((2, LANES), vec),
                  pl.BlockSpec((2, LANES, LANES), lo),
                  pl.BlockSpec((1, LANES), vec),
                  pl.BlockSpec((1, LANES), vec),
                  pl.BlockSpec((1, LANES), vec)],
        out_specs=[tile] * 8,
        out_shape=[sd(BF16), sd(BF16), sd(BF16), sd(BF16), sd(BF16), sd(F32), sd(F32), sd(BF16)],
        compiler_params=pltpu.CompilerParams(
            dimension_semantics=("arbitrary", "arbitrary", "arbitrary"),
            vmem_limit_bytes=VMEM_LIMIT),
        name="rwkv_prep",
    )(proj, proj, proj, lora, w0, wup_pad, a0, aup_pad, k_k, k_a, r_k)


def _wkv_consts(reverse):
    L = CHUNK
    t = lax.broadcasted_iota(jnp.int32, (L, 2 * L), 0)
    s = lax.broadcasted_iota(jnp.int32, (L, 2 * L), 1) % L
    if reverse:
        strict, incl = t < s, t <= s
    else:
        strict, incl = t > s, t >= s
    eye2 = jnp.where(t == s, 1.0, 0.0).astype(F32)
    rr = lax.broadcasted_iota(jnp.int32, (LANES, LANES), 0)
    cc = lax.broadcasted_iota(jnp.int32, (LANES, LANES), 1)
    bd = (rr // HEAD_DIM) == (cc // HEAD_DIM)
    eye = rr == cc
    row = lax.broadcasted_iota(jnp.int32, (L, LANES), 0)
    first = row == (L - 1 if reverse else 0)
    return strict, incl, eye2, bd, eye, first


def _wkv_chunk(load, consts, reverse):
    L = CHUNK
    strict, incl, eye2, bd, eye, first = consts

    def bd2(x):
        return jnp.where(bd, jnp.concatenate([x, x], axis=0), 0.0)

    cin, kk, b, ke, v, r, h = load()
    a = -kk
    cex = jnp.where(first, 0.0, pltpu.roll(cin, L - 1 if reverse else 1, axis=0))
    cend = cin[0:1, :] if reverse else cin[L - 1:L, :]
    m = 0.5 * cend
    e_in = jnp.exp(cin - m)
    e_out = jnp.exp(m - cin)
    em = jnp.exp(m)
    at = a * jnp.exp(cex - m)
    rt = r * e_in
    bt = b * e_out
    kt = ke * e_out
    ap, rp, bh, kh = at * em, rt * em, bt * em, kt * em

    ar = jnp.concatenate([at, rt], axis=0)
    g = _bdot_nt(ar, jnp.concatenate([bd2(bt), bd2(kt)], axis=0))
    yield
    a_ab = jnp.where(strict, g[:L, :LANES], 0.0)
    a_ak = jnp.where(strict, g[:L, LANES:], 0.0)
    p_rbk = jnp.concatenate([jnp.where(incl, g[L:, :LANES], 0.0),
                             jnp.where(incl, g[L:, LANES:], 0.0)], axis=1)

    n = a_ab
    p = eye2 + n
    n = _bdot(n, bd2(n))
    vb = bd2(v)
    x = _bdot(a_ak, vb)
    yield
    steps = int(math.log2(L)) - 1
    for i in range(steps):
        nb = bd2(n)
        if i < steps - 1:
            np_ = _bdot(jnp.concatenate([n, p], axis=0), nb)
            n = np_[:L]
            p = p + np_[L:]
        else:
            p = p + _bdot(p, nb)
        yield
    tinv = p

    wu = _bdot(tinv, jnp.concatenate([bd2(ap), bd2(x)], axis=1))
    yield
    w, u0 = wu[:, :LANES], wu[:, LANES:]
    zero = jnp.zeros((LANES, LANES), F32)
    rhs = jnp.concatenate([jnp.concatenate([bd2(w), bd2(u0)], axis=1),
                           jnp.concatenate([zero, vb], axis=1)], axis=0)
    pwo = _bdot(p_rbk, rhs)
    q = rp + pwo[:, :LANES]
    o0 = pwo[:, LANES:]
    zl = jnp.zeros((L, LANES), F32)
    mc = _bdot_tn(jnp.concatenate([bh, kh], axis=0),
                  jnp.concatenate([wu, jnp.concatenate([zl, v], axis=1)], axis=0))
    m_off = jnp.where(bd, mc[:, :LANES], 0.0)
    c = jnp.where(bd, mc[:, LANES:], 0.0)
    decay_col = jnp.sum(jnp.where(eye, jnp.exp(cend), 0.0), axis=1, keepdims=True)
    yield
    qm = _bdot(jnp.concatenate([q, m_off], axis=0), h)
    out = qm[:L] + o0
    h_new = decay_col * h + qm[L:] + c
    return out, h_new


def _interleave(gens):
    results = [None] * len(gens)
    live = list(range(len(gens)))
    while live:
        for i in list(live):
            try:
                next(gens[i])
            except StopIteration as stop:
                results[i] = stop.value
                live.remove(i)
    return results


def _wkv_kernel(rf_ref, vf_ref, kkf_ref, kef_ref, bbf_ref, cinf_ref,
                rb_ref, vb_ref, kkb_ref, keb_ref, bbb_ref, cinb_ref,
                of_ref, ob_ref, h_ref):
    @pl.when(pl.program_id(2) == 0)
    def _():
        h_ref[...] = jnp.zeros_like(h_ref)

    dirs = ((cinf_ref, kkf_ref, bbf_ref, kef_ref, vf_ref, rf_ref, of_ref),
            (cinb_ref, kkb_ref, bbb_ref, keb_ref, vb_ref, rb_ref, ob_ref))
    consts = (_wkv_consts(False), _wkv_consts(True))
    gens, sinks = [], []
    for i in range(WKV_PAIRS_PER_STEP):
        sl = slice(i * LANES, (i + 1) * LANES)
        for d in range(2):
            refs = dirs[d]
            load = functools.partial(
                lambda refs, d, i, sl: tuple(ref[0, :, sl].astype(F32) for ref in refs[:6]) + (h_ref[d, i],),
                refs, d, i, sl)
            gens.append(_wkv_chunk(load, consts[d], reverse=(d == 1)))
            sinks.append((refs[6], d, i, sl))
    for (out, h_new), (o_ref, d, i, sl) in zip(_interleave(gens), sinks):
        o_ref[0, :, sl] = out.astype(o_ref.dtype)
        h_ref[d, i] = h_new


def _wkv(proj, kk, ke0, ke1, bb0, bb1, cin0, cin1):
    b, t, _ = proj.shape
    nc = t // CHUNK
    L = CHUNK
    w = WKV_PAIRS_PER_STEP * LANES
    fwd = lambda base: (lambda i, p, c: (i, c, base // w + p))
    bwd = lambda base: (lambda i, p, c: (i, nc - 1 - c, base // w + p))
    blk = lambda im: pl.BlockSpec((1, L, w), im)
    out_sd = jax.ShapeDtypeStruct((b, t, R_WIDTH), BF16)
    return pl.pallas_call(
        _wkv_kernel,
        grid=(b, PAIRS // WKV_PAIRS_PER_STEP, nc),
        in_specs=[blk(fwd(COL_XR)), blk(fwd(COL_XV)), blk(fwd(0)), blk(fwd(0)), blk(fwd(0)), blk(fwd(0)),
                  blk(bwd(COL_XR)), blk(bwd(COL_XV)), blk(bwd(0)), blk(bwd(0)), blk(bwd(0)), blk(bwd(0))],
        out_specs=[blk(fwd(0)), blk(bwd(0))],
        out_shape=[out_sd, out_sd],
        scratch_shapes=[pltpu.VMEM((2, WKV_PAIRS_PER_STEP, LANES, LANES), F32)],
        compiler_params=pltpu.CompilerParams(
            dimension_semantics=("arbitrary", "arbitrary", "arbitrary"),
            vmem_limit_bytes=VMEM_LIMIT),
        name="wkv",
    )(proj, proj, kk, ke0, bb0, cin0, proj, proj, kk, ke1, bb1, cin1)


def _attn_kernel(q_ref, k_ref, v_ref, o_ref, krep_ref, vrep_ref, *, tq):
    w = A_KV_WIDTH

    @pl.when(pl.program_id(1) == 0)
    def _():
        r = lax.broadcasted_iota(jnp.int32, (w, w), 0)
        c = lax.broadcasted_iota(jnp.int32, (w, w), 1)
        kf, vf = k_ref[0], v_ref[0]
        for hk in range(A_KV_HEADS):
            sel = jnp.where(r == hk * HEAD_DIM + c % HEAD_DIM, 1.0, 0.0).astype(BF16)
            krep_ref[hk] = jnp.dot(kf, sel, preferred_element_type=F32).astype(BF16)
            vrep_ref[hk] = jnp.dot(vf, sel, preferred_element_type=F32).astype(BF16)

    lane = lax.broadcasted_iota(jnp.int32, (tq, w), 1) // HEAD_DIM
    outs = []
    for hk in range(A_KV_HEADS):
        qg = q_ref[0, :, hk * w:(hk + 1) * w]
        zero = jnp.zeros_like(qg)
        qst = jnp.concatenate([jnp.where(lane == g, qg, zero) for g in range(A_GROUP)], axis=0)
        s = lax.dot_general(qst, krep_ref[hk], (((1,), (1,)), ((), ())),
                            preferred_element_type=F32)
        mx = jnp.max(s, axis=-1, keepdims=True)
        p = jnp.exp(s - mx)
        l = jnp.sum(p, axis=-1, keepdims=True)
        o = jnp.dot(p.astype(BF16), vrep_ref[hk], preferred_element_type=F32) / l
        og = jnp.zeros((tq, w), F32)
        for g in range(A_GROUP):
            og = jnp.where(lane == g, o[g * tq:(g + 1) * tq], og)
        outs.append(og)
    o_ref[0] = jnp.concatenate(outs, axis=1).astype(o_ref.dtype)


def _attention(proj, tq=128):
    b, t, _ = proj.shape
    kern = functools.partial(_attn_kernel, tq=tq)
    return pl.pallas_call(
        kern,
        grid=(b, t // tq),
        in_specs=[pl.BlockSpec((1, tq, A_WIDTH), lambda i, j: (i, j, COL_Q // A_WIDTH)),
                  pl.BlockSpec((1, t, A_KV_WIDTH), lambda i, j: (i, 0, COL_K // A_KV_WIDTH)),
                  pl.BlockSpec((1, t, A_KV_WIDTH), lambda i, j: (i, 0, COL_V // A_KV_WIDTH))],
        out_specs=pl.BlockSpec((1, tq, A_WIDTH), lambda i, j: (i, j, 0)),
        out_shape=jax.ShapeDtypeStruct((b, t, A_WIDTH), BF16),
        scratch_shapes=[pltpu.VMEM((A_KV_HEADS, t, A_KV_WIDTH), BF16),
                        pltpu.VMEM((A_KV_HEADS, t, A_KV_WIDTH), BF16)],
        compiler_params=pltpu.CompilerParams(dimension_semantics=("arbitrary", "arbitrary"),
                                             vmem_limit_bytes=VMEM_LIMIT),
        name="attention",
    )(proj, proj, proj)


def _post_kernel(x_ref, of_ref, ob_ref, cv_ref, zr_ref, at_ref, za_ref, gr_ref, ga_ref,
                 gnw_ref, gnb_ref, wbr_ref, wba_ref, wout_ref, fg_ref, o_ref):
    ones = _head_ones()
    wkv = of_ref[...].astype(F32) + ob_ref[...].astype(F32)
    parts = []
    for p in range(PAIRS):
        sl = slice(p * LANES, (p + 1) * LANES)
        y = wkv[:, sl]
        mu = _dot_exact_lhs(y, ones) * (1.0 / HEAD_DIM)
        yc = y - mu
        var = _dot_exact_lhs(yc * yc, ones) * (1.0 / HEAD_DIM)
        parts.append(yc * lax.rsqrt(var + GN_EPS))
    gn = jnp.concatenate(parts, axis=1) * gnw_ref[...] + gnb_ref[...]
    zr = zr_ref[...].astype(F32)
    o_r = (gn + cv_ref[...].astype(F32)) * (zr * jax.nn.sigmoid(zr))
    za = za_ref[...].astype(F32)
    o_a = at_ref[...].astype(F32) * (za * jax.nn.sigmoid(za))
    p_r = _bdot(o_r, wbr_ref[...])
    p_a = _bdot(o_a, wba_ref[...])
    merged = (jax.nn.sigmoid(gr_ref[...].astype(F32)) * p_r
              + jax.nn.sigmoid(ga_ref[...].astype(F32)) * p_a)
    y = x_ref[...] + _bdot(merged, wout_ref[...])
    yn = y * lax.rsqrt(jnp.mean(y * y, axis=-1, keepdims=True) + NORM_EPS)
    o_ref[...] = yn * fg_ref[...]


def _post(x2d, of, ob, cv, proj2d, attn, gn_w, gn_b, w_br, w_ba, w_out, final_g, tm=256):
    n, d = x2d.shape
    row = lambda w, base: pl.BlockSpec((tm, w), lambda i: (i, base // w))
    const = lambda shape: pl.BlockSpec(shape, lambda i: (0, 0))
    return pl.pallas_call(
        _post_kernel,
        grid=(n // tm,),
        in_specs=[row(d, 0), row(R_WIDTH, 0), row(R_WIDTH, 0), row(R_WIDTH, 0),
                  row(R_WIDTH, COL_ZR), row(A_WIDTH, 0), row(A_WIDTH, COL_ZA),
                  row(d, COL_GR), row(d, COL_GA),
                  const((1, R_WIDTH)), const((1, R_WIDTH)),
                  const((R_WIDTH, d)), const((A_WIDTH, d)), const((d, d)), const((1, d))],
        out_specs=row(d, 0),
        out_shape=jax.ShapeDtypeStruct((n, d), F32),
        compiler_params=pltpu.CompilerParams(dimension_semantics=("arbitrary",),
                                             vmem_limit_bytes=VMEM_LIMIT),
        name="merge_out",
    )(x2d, of, ob, cv, proj2d, attn, proj2d, proj2d, proj2d,
      gn_w.reshape(1, -1), gn_b.reshape(1, -1), w_br, w_ba, w_out, final_g.reshape(1, -1))


def _column_permutation():
    shift_w = 3 * R_WIDTH + 4 * LORA
    o_wd, o_ad = 3 * R_WIDTH, 3 * R_WIDTH + 2 * LORA
    o_zr = shift_w
    o_q = o_zr + R_WIDTH
    o_k = o_q + A_WIDTH
    o_v = o_k + A_KV_WIDTH
    o_za = o_v + A_KV_WIDTH
    o_g = o_za + A_WIDTH
    r = lambda a, w: np.arange(a, a + w)
    order = [r(0, 3 * R_WIDTH), r(o_zr, R_WIDTH), r(o_q, A_WIDTH), r(o_za, A_WIDTH),
             r(o_g, 4096), r(o_k, A_KV_WIDTH), r(o_v, A_KV_WIDTH), r(o_wd, 2 * LORA), r(o_ad, 2 * LORA)]
    perm = np.concatenate(order)
    assert perm.shape[0] == D_IN and np.array_equal(np.sort(perm), np.arange(D_IN))
    return perm, shift_w


def _pad_lora(up):
    z = jnp.zeros_like(up[0])
    return jnp.stack([jnp.concatenate([up[0], z], axis=0), jnp.concatenate([z, up[1]], axis=0)], axis=0)


def _layer(x, norm_g, w_in, shift_mu, w0, w_up, a0, a_up, k_k, k_a, r_k, gn_w, gn_b,
           q_norm_g, k_norm_g, w_branch_rwkv, w_branch_attn, w_out, out_g):
    b, t, d = x.shape
    assert t % CHUNK == 0 and t % GRID_W == 0 and d == 2048 and w_in.shape[1] == D_IN
    perm, shift_w = _column_permutation()
    w_perm = jnp.take(w_in, perm, axis=1).astype(BF16)
    mu_full = jnp.concatenate([shift_mu, jnp.zeros((2, D_IN - shift_w), F32)], axis=1)
    mu_perm = jnp.take(mu_full, perm, axis=1)
    g_perm = jnp.zeros((1, D_IN), F32)
    g_perm = g_perm.at[0, COL_Q:COL_Q + A_WIDTH].set(jnp.tile(q_norm_g * HEAD_DIM ** -0.5, A_Q_HEADS))
    g_perm = g_perm.at[0, COL_K:COL_K + A_KV_WIDTH].set(jnp.tile(k_norm_g, A_KV_HEADS))

    x2d = x.reshape(b * t, d)
    h = _rmsnorm(x2d, norm_g, BF16).reshape(b, t, d)
    proj, lora = _inproj(h, w_perm, mu_perm, g_perm)
    kk, ke0, ke1, bb0, bb1, cin0, cin1, cv = _prep(
        proj, lora, w0, _pad_lora(w_up), a0, _pad_lora(a_up),
        k_k.reshape(1, -1), k_a.reshape(1, -1), r_k.reshape(1, -1))
    of, ob = _wkv(proj, kk, ke0, ke1, bb0, bb1, cin0, cin1)
    attn = _attention(proj)
    n = b * t
    out = _post(x2d, of.reshape(n, -1), ob.reshape(n, -1), cv.reshape(n, -1), proj.reshape(n, D_IN),
                attn.reshape(n, -1), gn_w, gn_b, w_branch_rwkv.astype(BF16), w_branch_attn.astype(BF16),
                w_out.astype(BF16), out_g)
    return out.reshape(b, t, d)


def kernel(x, norm_g, w_in, shift_mu, w0, w_up, a0, a_up, k_k, k_a, r_k, gn_w, gn_b, q_norm_g, k_norm_g,
           w_branch_rwkv, w_branch_attn, w_out, final_norm_g):
    assert norm_g.shape[0] == 1, "single-layer block"
    return _layer(x, norm_g[0], w_in[0], shift_mu[0], w0[0], w_up[0], a0[0], a_up[0], k_k[0], k_a[0],
                  r_k[0], gn_w[0], gn_b[0], q_norm_g[0], k_norm_g[0], w_branch_rwkv[0],
                  w_branch_attn[0], w_out[0], final_norm_g)
```

```python
import functools
import math

import jax
import jax.numpy as jnp
import numpy as np
from jax import lax
from jax.experimental import pallas as pl
from jax.experimental.pallas import tpu as pltpu

F32 = jnp.float32
BF16 = jnp.bfloat16

HEAD_DIM = 64
R_HEADS = 16
R_WIDTH = R_HEADS * HEAD_DIM
LORA = 64
A_Q_HEADS = 16
A_KV_HEADS = 4
A_GROUP = A_Q_HEADS // A_KV_HEADS
A_WIDTH = A_Q_HEADS * HEAD_DIM
A_KV_WIDTH = A_KV_HEADS * HEAD_DIM
GRID_W = 64
ROPE_THETA = 10000.0
NORM_EPS = 1e-6
GN_EPS = 64e-5
KK_EPS = 1e-24
DECAY_SCALE = math.exp(-0.5)

LANES = 128
VMEM_LIMIT = 56 * 1024 * 1024

CHUNK = 64
PAIRS = R_WIDTH // LANES
WKV_PAIRS_PER_STEP = 8

COL_XR, COL_XK, COL_XV, COL_ZR, COL_Q, COL_ZA = 0, 1024, 2048, 3072, 4096, 5120
COL_GR, COL_GA, COL_K, COL_V, COL_WD, COL_AD = 6144, 8192, 10240, 10496, 10752, 10880
D_IN = 11008
PROJ_TN = 256
PROJ_ROW_CHUNK = 512
PROJ_SHIFT_PAD = 8


def _bdot(a, b):
    return jnp.dot(a.astype(BF16), b.astype(BF16), preferred_element_type=F32)


def _bdot_nt(a, b):
    return lax.dot_general(a.astype(BF16), b.astype(BF16), (((1,), (1,)), ((), ())),
                           preferred_element_type=F32)


def _bdot_tn(a, b):
    return lax.dot_general(a.astype(BF16), b.astype(BF16), (((0,), (0,)), ((), ())),
                           preferred_element_type=F32)


def _split2(x):
    hi = x.astype(BF16)
    lo = (x - hi.astype(F32)).astype(BF16)
    return hi, lo


def _dot_exact_rhs(a_bf16, x):
    hi, lo = _split2(x)
    d = lambda y: jnp.dot(a_bf16, y, preferred_element_type=F32)
    return d(hi) + d(lo)


def _dot_exact_lhs(x, b_bf16):
    hi, lo = _split2(x)
    d = lambda y: jnp.dot(y, b_bf16, preferred_element_type=F32)
    return d(hi) + d(lo)


def _dot_f32(x, w):
    xh = x.astype(BF16)
    xl = (x - xh.astype(F32)).astype(BF16)
    wh = w.astype(BF16)
    wl = (w - wh.astype(F32)).astype(BF16)
    d = lambda p, q: jnp.dot(p, q, preferred_element_type=F32)
    return d(xh, wh) + d(xl, wh) + d(xh, wl)


def _head_ones(width=LANES):
    r = lax.broadcasted_iota(jnp.int32, (width, width), 0) // HEAD_DIM
    c = lax.broadcasted_iota(jnp.int32, (width, width), 1) // HEAD_DIM
    return jnp.where(r == c, 1.0, 0.0).astype(BF16)


def _rmsnorm_kernel(x_ref, g_ref, o_ref):
    x = x_ref[...]
    y = x * lax.rsqrt(jnp.mean(x * x, axis=-1, keepdims=True) + NORM_EPS)
    o_ref[...] = (y * g_ref[...]).astype(o_ref.dtype)


def _rmsnorm(x2d, g, out_dtype, tm=512):
    n, d = x2d.shape
    return pl.pallas_call(
        _rmsnorm_kernel,
        grid=(n // tm,),
        in_specs=[pl.BlockSpec((tm, d), lambda i: (i, 0)),
                  pl.BlockSpec((1, d), lambda i: (0, 0))],
        out_specs=pl.BlockSpec((tm, d), lambda i: (i, 0)),
        out_shape=jax.ShapeDtypeStruct((n, d), out_dtype),
        compiler_params=pltpu.CompilerParams(dimension_semantics=("arbitrary",),
                                             vmem_limit_bytes=VMEM_LIMIT),
        name="rmsnorm",
    )(x2d, g.reshape(1, d))


def _rope_tables(t):
    rows = t // GRID_W
    row = np.repeat(np.arange(rows, dtype=np.float32), GRID_W)
    colv = np.tile(np.arange(GRID_W, dtype=np.float32), rows)
    axis_dim = HEAD_DIM // 2
    freqs = jnp.asarray(ROPE_THETA, F32) ** (-jnp.arange(0, axis_dim, 2, dtype=F32) / axis_dim)
    ang = jnp.concatenate([jnp.asarray(row)[:, None] * freqs, jnp.asarray(colv)[:, None] * freqs], axis=-1)
    reps = PROJ_TN // HEAD_DIM
    cos = jnp.tile(jnp.repeat(jnp.cos(ang), 2, axis=-1), (1, reps))
    sin = jnp.tile(jnp.repeat(jnp.sin(ang), 2, axis=-1), (1, reps))
    return cos, sin


def _inproj_kernel(h_ref, w_ref, mu_ref, g_ref, cos_ref, sin_ref, o_ref, lora_ref, acc_ref, *,
                   shift_blocks, lora_block, q_blocks, k_block):
    j = pl.program_id(1)
    t = h_ref.shape[1]
    rc = PROJ_ROW_CHUNK
    nchunk = t // rc
    is_shift = jnp.logical_or(j < shift_blocks, j == lora_block)
    is_rope = jnp.logical_or(jnp.logical_and(j >= q_blocks[0], j < q_blocks[1]), j == k_block)

    def mm(c):
        return jnp.dot(h_ref[0, c * rc:(c + 1) * rc, :], w_ref[...], preferred_element_type=F32)

    @pl.when(is_shift)
    def _():
        pad = PROJ_SHIFT_PAD
        acc_ref[0:pad, :] = jnp.zeros((pad, PROJ_TN), F32)
        acc_ref[pad + t:2 * pad + t, :] = jnp.zeros((pad, PROJ_TN), F32)
        mu = mu_ref[...]

        def epilogue(c):
            lo = pad + c * rc
            cur = acc_ref[lo:lo + rc, :]
            prev = acc_ref[lo - 1:lo - 1 + rc, :]
            nxt = acc_ref[lo + 1:lo + 1 + rc, :]
            y = cur + mu[0:1] * (prev - cur) + mu[1:2] * (nxt - cur)
            o_ref[0, c * rc:(c + 1) * rc, :] = y.astype(o_ref.dtype)
            lora_ref[0, c * rc:(c + 1) * rc, :] = y

        for c in range(nchunk):
            acc_ref[pad + c * rc:pad + (c + 1) * rc, :] = mm(c)
            if c > 0:
                epilogue(c - 1)
        epilogue(nchunk - 1)

    @pl.when(is_rope)
    def _():
        ones = _head_ones(PROJ_TN)
        lane = lax.broadcasted_iota(jnp.int32, (rc, PROJ_TN), 1)
        even = (lane % 2) == 0
        ahead = mm(0)
        for c in range(nchunk):
            rows = slice(c * rc, (c + 1) * rc)
            acc, ahead = ahead, (mm(c + 1) if c + 1 < nchunk else None)
            ss = _dot_exact_lhs(acc * acc, ones) * (1.0 / HEAD_DIM)
            y = acc * lax.rsqrt(ss + NORM_EPS) * g_ref[...]
            nxt = pltpu.roll(y, PROJ_TN - 1, axis=1)
            prv = pltpu.roll(y, 1, axis=1)
            o_ref[0, rows, :] = (y * cos_ref[rows, :]
                                 + jnp.where(even, -nxt, prv) * sin_ref[rows, :]).astype(o_ref.dtype)

    @pl.when(jnp.logical_not(jnp.logical_or(is_shift, is_rope)))
    def _():
        for c in range(nchunk):
            o_ref[0, c * rc:(c + 1) * rc, :] = mm(c).astype(o_ref.dtype)


def _inproj(h, w_perm, mu_perm, g_perm):
    b, t, d = h.shape
    nj = D_IN // PROJ_TN
    cos, sin = _rope_tables(t)
    kern = functools.partial(
        _inproj_kernel, shift_blocks=(3 * R_WIDTH) // PROJ_TN, lora_block=COL_WD // PROJ_TN,
        q_blocks=(COL_Q // PROJ_TN, (COL_Q + A_WIDTH) // PROJ_TN), k_block=COL_K // PROJ_TN)
    return pl.pallas_call(
        kern,
        grid=(b, nj),
        in_specs=[pl.BlockSpec((1, t, d), lambda i, j: (i, 0, 0)),
                  pl.BlockSpec((d, PROJ_TN), lambda i, j: (0, j)),
                  pl.BlockSpec((2, PROJ_TN), lambda i, j: (0, j)),
                  pl.BlockSpec((1, PROJ_TN), lambda i, j: (0, j)),
                  pl.BlockSpec((t, PROJ_TN), lambda i, j: (0, 0)),
                  pl.BlockSpec((t, PROJ_TN), lambda i, j: (0, 0))],
        out_specs=[pl.BlockSpec((1, t, PROJ_TN), lambda i, j: (i, 0, j)),
                   pl.BlockSpec((1, t, PROJ_TN), lambda i, j: (i, 0, 0))],
        out_shape=[jax.ShapeDtypeStruct((b, t, D_IN), BF16),
                   jax.ShapeDtypeStruct((b, t, PROJ_TN), F32)],
        scratch_shapes=[pltpu.VMEM((t + 2 * PROJ_SHIFT_PAD, PROJ_TN), F32)],
        compiler_params=pltpu.CompilerParams(dimension_semantics=("arbitrary", "arbitrary"),
                                             vmem_limit_bytes=VMEM_LIMIT),
        name="inproj",
    )(h, w_perm, mu_perm, g_perm, cos, sin)


def _prep_kernel(xr_ref, xk_ref, xv_ref, lora_ref, w0_ref, wup_ref, a0_ref, aup_ref,
                 kkg_ref, ka_ref, rk_ref,
                 kk_ref, ke0_ref, ke1_ref, bb0_ref, bb1_ref, cin0_ref, cin1_ref, cv_ref):
    L = CHUNK
    ones = _head_ones()
    r = xr_ref[0].astype(F32)
    k = xk_ref[0].astype(F32)
    v = xv_ref[0].astype(F32)
    lora = lora_ref[0]
    wd = jnp.tanh(lora[:, :LANES])
    ad = lora[:, LANES:]
    k_a = ka_ref[...]
    kkn = k * kkg_ref[...]
    ss = _dot_exact_lhs(kkn * kkn, ones)
    w_raw = w0_ref[...] + _dot_f32(wd, wup_ref[...])
    a_raw = a0_ref[...] + _bdot(ad, aup_ref[...])
    kk = kkn * lax.rsqrt(jnp.maximum(ss, KK_EPS))
    kk_ref[0] = kk.astype(kk_ref.dtype)
    lw = -DECAY_SCALE * jax.nn.sigmoid(w_raw)
    a2 = jax.nn.sigmoid(a_raw)
    tt_ = lax.broadcasted_iota(jnp.int32, (L, L), 0)
    ss_ = lax.broadcasted_iota(jnp.int32, (L, L), 1)
    tri = (jnp.where(tt_ >= ss_, 1.0, 0.0).astype(BF16), jnp.where(tt_ <= ss_, 1.0, 0.0).astype(BF16))
    nchunk = lw.shape[0] // L
    ke_sum = None
    for d, (ke_ref, bb_ref, cin_ref) in enumerate(((ke0_ref, bb0_ref, cin0_ref),
                                                   (ke1_ref, bb1_ref, cin1_ref))):
        lanes = slice(d * LANES, (d + 1) * LANES)
        wide = jnp.concatenate([lw[c * L:(c + 1) * L, lanes] for c in range(nchunk)], axis=1)
        cin = _dot_exact_rhs(tri[d], wide)
        for c in range(nchunk):
            cin_ref[0, c * L:(c + 1) * L, :] = cin[:, c * LANES:(c + 1) * LANES]
        a = a2[:, lanes]
        bb_ref[0] = (kk * a).astype(bb_ref.dtype)
        ke = k * (1.0 + (a - 1.0) * k_a)
        ke_ref[0] = ke.astype(ke_ref.dtype)
        ke_sum = ke if ke_sum is None else ke_sum + ke
    coef = _dot_exact_lhs(r * ke_sum * rk_ref[...], ones)
    cv_ref[0] = (coef * v).astype(cv_ref.dtype)


def _prep(proj, lora, w0, wup_pad, a0, aup_pad, k_k, k_a, r_k, tt=512):
    b, t, _ = proj.shape
    col = lambda base: (lambda i, s, p: (i, s, base // LANES + p))
    vec = lambda i, s, p: (0, p)
    tile = pl.BlockSpec((1, tt, LANES), lambda i, s, p: (i, s, p))
    sd = lambda dt: jax.ShapeDtypeStruct((b, t, R_WIDTH), dt)
    return pl.pallas_call(
        _prep_kernel,
        grid=(b, t // tt, PAIRS),
        in_specs=[pl.BlockSpec((1, tt, LANES), col(COL_XR)),
                  pl.BlockSpec((1, tt, LANES), col(COL_XK)),
                  pl.BlockSpec((1, tt, LANES), col(COL_XV)),
                  pl.BlockSpec((1, tt, 2 * LANES), lambda i, s, p: (i, s, 0)),
                  pl.BlockSpec((1, 2 * LANES), vec),
                  pl.BlockSpec((LANES, 2 * LANES), vec),
                  pl.BlockSpec((1, 2 * LANES), vec),
                  pl.BlockSpec((LANES, 2 * LANES), vec),
                  pl.BlockSpec((1, LANES), vec),
                  pl.BlockSpec((1, LANES), vec),
                  pl.BlockSpec((1, LANES), vec)],
        out_specs=[tile] * 8,
        out_shape=[sd(BF16), sd(BF16), sd(BF16), sd(BF16), sd(BF16), sd(F32), sd(F32), sd(BF16)],
        compiler_params=pltpu.CompilerParams(
            dimension_semantics=("arbitrary", "arbitrary", "arbitrary"),
            vmem_limit_bytes=VMEM_LIMIT),
        name="rwkv_prep",
    )(proj, proj, proj, lora, w0, wup_pad, a0, aup_pad, k_k, k_a, r_k)


def _wkv_consts(reverse):
    L = CHUNK
    t = lax.broadcasted_iota(jnp.int32, (L, 2 * L), 0)
    s = lax.broadcasted_iota(jnp.int32, (L, 2 * L), 1) % L
    if reverse:
        strict, incl = t < s, t <= s
    else:
        strict, incl = t > s, t >= s
    eye2 = jnp.where(t == s, 1.0, 0.0).astype(F32)
    rr = lax.broadcasted_iota(jnp.int32, (LANES, LANES), 0)
    cc = lax.broadcasted_iota(jnp.int32, (LANES, LANES), 1)
    bd = (rr // HEAD_DIM) == (cc // HEAD_DIM)
    eye = rr == cc
    row = lax.broadcasted_iota(jnp.int32, (L, LANES), 0)
    first = row == (L - 1 if reverse else 0)
    return strict, incl, eye2, bd, eye, first


def _wkv_chunk(load, consts, reverse):
    L = CHUNK
    strict, incl, eye2, bd, eye, first = consts

    def bd2(x):
        return jnp.where(bd, jnp.concatenate([x, x], axis=0), 0.0)

    cin, kk, b, ke, v, r, h = load()
    a = -kk
    cex = jnp.where(first, 0.0, pltpu.roll(cin, L - 1 if reverse else 1, axis=0))
    cend = cin[0:1, :] if reverse else cin[L - 1:L, :]
    m = 0.5 * cend
    e_in = jnp.exp(cin - m)
    e_out = jnp.exp(m - cin)
    em = jnp.exp(m)
    at = a * jnp.exp(cex - m)
    rt = r * e_in
    bt = b * e_out
    kt = ke * e_out
    ap, rp, bh, kh = at * em, rt * em, bt * em, kt * em

    ar = jnp.concatenate([at, rt], axis=0)
    g = _bdot_nt(ar, jnp.concatenate([bd2(bt), bd2(kt)], axis=0))
    yield
    a_ab = jnp.where(strict, g[:L, :LANES], 0.0)
    a_ak = jnp.where(strict, g[:L, LANES:], 0.0)
    p_rbk = jnp.concatenate([jnp.where(incl, g[L:, :LANES], 0.0),
                             jnp.where(incl, g[L:, LANES:], 0.0)], axis=1)

    n = a_ab
    p = eye2 + n
    n = _bdot(n, bd2(n))
    vb = bd2(v)
    x = _bdot(a_ak, vb)
    yield
    steps = int(math.log2(L)) - 1
    for i in range(steps):
        nb = bd2(n)
        if i < steps - 1:
            np_ = _bdot(jnp.concatenate([n, p], axis=0), nb)
            n = np_[:L]
            p = p + np_[L:]
        else:
            p = p + _bdot(p, nb)
        yield
    tinv = p

    wu = _bdot(tinv, jnp.concatenate([bd2(ap), bd2(x)], axis=1))
    yield
    w, u0 = wu[:, :LANES], wu[:, LANES:]
    zero = jnp.zeros((LANES, LANES), F32)
    rhs = jnp.concatenate([jnp.concatenate([bd2(w), bd2(u0)], axis=1),
                           jnp.concatenate([zero, vb], axis=1)], axis=0)
    pwo = _bdot(p_rbk, rhs)
    q = rp + pwo[:, :LANES]
    o0 = pwo[:, LANES:]
    zl = jnp.zeros((L, LANES), F32)
    mc = _bdot_tn(jnp.concatenate([bh, kh], axis=0),
                  jnp.concatenate([wu, jnp.concatenate([zl, v], axis=1)], axis=0))
    m_off = jnp.where(bd, mc[:, :LANES], 0.0)
    c = jnp.where(bd, mc[:, LANES:], 0.0)
    decay_col = jnp.sum(jnp.where(eye, jnp.exp(cend), 0.0), axis=1, keepdims=True)
    yield
    qm = _bdot(jnp.concatenate([q, m_off], axis=0), h)
    out = qm[:L] + o0
    h_new = decay_col * h + qm[L:] + c
    return out, h_new


def _interleave(gens):
    results = [None] * len(gens)
    live = list(range(len(gens)))
    while live:
        for i in list(live):
            try:
                next(gens[i])
            except StopIteration as stop:
                results[i] = stop.value
                live.remove(i)
    return results


def _wkv_kernel(rf_ref, vf_ref, kkf_ref, kef_ref, bbf_ref, cinf_ref,
                rb_ref, vb_ref, kkb_ref, keb_ref, bbb_ref, cinb_ref,
                of_ref, ob_ref, h_ref):
    @pl.when(pl.program_id(2) == 0)
    def _():
        h_ref[...] = jnp.zeros_like(h_ref)

    dirs = ((cinf_ref, kkf_ref, bbf_ref, kef_ref, vf_ref, rf_ref, of_ref),
            (cinb_ref, kkb_ref, bbb_ref, keb_ref, vb_ref, rb_ref, ob_ref))
    consts = (_wkv_consts(False), _wkv_consts(True))
    gens, sinks = [], []
    for i in range(WKV_PAIRS_PER_STEP):
        sl = slice(i * LANES, (i + 1) * LANES)
        for d in range(2):
            refs = dirs[d]
            load = functools.partial(
                lambda refs, d, i, sl: tuple(ref[0, :, sl].astype(F32) for ref in refs[:6]) + (h_ref[d, i],),
                refs, d, i, sl)
            gens.append(_wkv_chunk(load, consts[d], reverse=(d == 1)))
            sinks.append((refs[6], d, i, sl))
    for (out, h_new), (o_ref, d, i, sl) in zip(_interleave(gens), sinks):
        o_ref[0, :, sl] = out.astype(o_ref.dtype)
        h_ref[d, i] = h_new


def _wkv(proj, kk, ke0, ke1, bb0, bb1, cin0, cin1):
    b, t, _ = proj.shape
    nc = t // CHUNK
    L = CHUNK
    w = WKV_PAIRS_PER_STEP * LANES
    fwd = lambda base: (lambda i, p, c: (i, c, base // w + p))
    bwd = lambda base: (lambda i, p, c: (i, nc - 1 - c, base // w + p))
    blk = lambda im: pl.BlockSpec((1, L, w), im)
    out_sd = jax.ShapeDtypeStruct((b, t, R_WIDTH), BF16)
    return pl.pallas_call(
        _wkv_kernel,
        grid=(b, PAIRS // WKV_PAIRS_PER_STEP, nc),
        in_specs=[blk(fwd(COL_XR)), blk(fwd(COL_XV)), blk(fwd(0)), blk(fwd(0)), blk(fwd(0)), blk(fwd(0)),
                  blk(bwd(COL_XR)), blk(bwd(COL_XV)), blk(bwd(0)), blk(bwd(0)), blk(bwd(0)), blk(bwd(0))],
        out_specs=[blk(fwd(0)), blk(bwd(0))],
        out_shape=[out_sd, out_sd],
        scratch_shapes=[pltpu.VMEM((2, WKV_PAIRS_PER_STEP, LANES, LANES), F32)],
        compiler_params=pltpu.CompilerParams(
            dimension_semantics=("arbitrary", "arbitrary", "arbitrary"),
            vmem_limit_bytes=VMEM_LIMIT),
        name="wkv",
    )(proj, proj, kk, ke0, bb0, cin0, proj, proj, kk, ke1, bb1, cin1)


def _attn_kernel(q_ref, k_ref, v_ref, o_ref, krep_ref, vrep_ref, *, tq):
    w = A_KV_WIDTH

    @pl.when(pl.program_id(1) == 0)
    def _():
        r = lax.broadcasted_iota(jnp.int32, (w, w), 0)
        c = lax.broadcasted_iota(jnp.int32, (w, w), 1)
        kf, vf = k_ref[0], v_ref[0]
        for hk in range(A_KV_HEADS):
            sel = jnp.where(r == hk * HEAD_DIM + c % HEAD_DIM, 1.0, 0.0).astype(BF16)
            krep_ref[hk] = jnp.dot(kf, sel, preferred_element_type=F32).astype(BF16)
            vrep_ref[hk] = jnp.dot(vf, sel, preferred_element_type=F32).astype(BF16)

    lane = lax.broadcasted_iota(jnp.int32, (tq, w), 1) // HEAD_DIM
    outs = []
    for hk in range(A_KV_HEADS):
        qg = q_ref[0, :, hk * w:(hk + 1) * w]
        zero = jnp.zeros_like(qg)
        qst = jnp.concatenate([jnp.where(lane == g, qg, zero) for g in range(A_GROUP)], axis=0)
        s = lax.dot_general(qst, krep_ref[hk], (((1,), (1,)), ((), ())),
                            preferred_element_type=F32)
        mx = jnp.max(s, axis=-1, keepdims=True)
        p = jnp.exp(s - mx)
        l = jnp.sum(p, axis=-1, keepdims=True)
        o = jnp.dot(p.astype(BF16), vrep_ref[hk], preferred_element_type=F32) / l
        og = jnp.zeros((tq, w), F32)
        for g in range(A_GROUP):
            og = jnp.where(lane == g, o[g * tq:(g + 1) * tq], og)
        outs.append(og)
    o_ref[0] = jnp.concatenate(outs, axis=1).astype(o_ref.dtype)


def _attention(proj, tq=128):
    b, t, _ = proj.shape
    kern = functools.partial(_attn_kernel, tq=tq)
    return pl.pallas_call(
        kern,
        grid=(b, t // tq),
        in_specs=[pl.BlockSpec((1, tq, A_WIDTH), lambda i, j: (i, j, COL_Q // A_WIDTH)),
                  pl.BlockSpec((1, t, A_KV_WIDTH), lambda i, j: (i, 0, COL_K // A_KV_WIDTH)),
                  pl.BlockSpec((1, t, A_KV_WIDTH), lambda i, j: (i, 0, COL_V // A_KV_WIDTH))],
        out_specs=pl.BlockSpec((1, tq, A_WIDTH), lambda i, j: (i, j, 0)),
        out_shape=jax.ShapeDtypeStruct((b, t, A_WIDTH), BF16),
        scratch_shapes=[pltpu.VMEM((A_KV_HEADS, t, A_KV_WIDTH), BF16),
                        pltpu.VMEM((A_KV_HEADS, t, A_KV_WIDTH), BF16)],
        compiler_params=pltpu.CompilerParams(dimension_semantics=("arbitrary", "arbitrary"),
                                             vmem_limit_bytes=VMEM_LIMIT),
        name="attention",
    )(proj, proj, proj)


def _post_kernel(x_ref, of_ref, ob_ref, cv_ref, zr_ref, at_ref, za_ref, gr_ref, ga_ref,
                 gnw_ref, gnb_ref, wbr_ref, wba_ref, wout_ref, fg_ref, o_ref):
    ones = _head_ones()
    wkv = of_ref[...].astype(F32) + ob_ref[...].astype(F32)
    parts = []
    for p in range(PAIRS):
        sl = slice(p * LANES, (p + 1) * LANES)
        y = wkv[:, sl]
        mu = _dot_exact_lhs(y, ones) * (1.0 / HEAD_DIM)
        yc = y - mu
        var = _dot_exact_lhs(yc * yc, ones) * (1.0 / HEAD_DIM)
        parts.append(yc * lax.rsqrt(var + GN_EPS))
    gn = jnp.concatenate(parts, axis=1) * gnw_ref[...] + gnb_ref[...]
    zr = zr_ref[...].astype(F32)
    o_r = (gn + cv_ref[...].astype(F32)) * (zr * jax.nn.sigmoid(zr))
    za = za_ref[...].astype(F32)
    o_a = at_ref[...].astype(F32) * (za * jax.nn.sigmoid(za))
    p_r = _bdot(o_r, wbr_ref[...])
    p_a = _bdot(o_a, wba_ref[...])
    merged = (jax.nn.sigmoid(gr_ref[...].astype(F32)) * p_r
              + jax.nn.sigmoid(ga_ref[...].astype(F32)) * p_a)
    y = x_ref[...] + _bdot(merged, wout_ref[...])
    yn = y * lax.rsqrt(jnp.mean(y * y, axis=-1, keepdims=True) + NORM_EPS)
    o_ref[...] = yn * fg_ref[...]


def _post(x2d, of, ob, cv, proj2d, attn, gn_w, gn_b, w_br, w_ba, w_out, final_g, tm=256):
    n, d = x2d.shape
    row = lambda w, base: pl.BlockSpec((tm, w), lambda i: (i, base // w))
    const = lambda shape: pl.BlockSpec(shape, lambda i: (0, 0))
    return pl.pallas_call(
        _post_kernel,
        grid=(n // tm,),
        in_specs=[row(d, 0), row(R_WIDTH, 0), row(R_WIDTH, 0), row(R_WIDTH, 0),
                  row(R_WIDTH, COL_ZR), row(A_WIDTH, 0), row(A_WIDTH, COL_ZA),
                  row(d, COL_GR), row(d, COL_GA),
                  const((1, R_WIDTH)), const((1, R_WIDTH)),
                  const((R_WIDTH, d)), const((A_WIDTH, d)), const((d, d)), const((1, d))],
        out_specs=row(d, 0),
        out_shape=jax.ShapeDtypeStruct((n, d), F32),
        compiler_params=pltpu.CompilerParams(dimension_semantics=("arbitrary",),
                                             vmem_limit_bytes=VMEM_LIMIT),
        name="merge_out",
    )(x2d, of, ob, cv, proj2d, attn, proj2d, proj2d, proj2d,
      gn_w.reshape(1, -1), gn_b.reshape(1, -1), w_br, w_ba, w_out, final_g.reshape(1, -1))


def _column_permutation():
    shift_w = 3 * R_WIDTH + 4 * LORA
    o_wd, o_ad = 3 * R_WIDTH, 3 * R_WIDTH + 2 * LORA
    o_zr = shift_w
    o_q = o_zr + R_WIDTH
    o_k = o_q + A_WIDTH
    o_v = o_k + A_KV_WIDTH
    o_za = o_v + A_KV_WIDTH
    o_g = o_za + A_WIDTH
    r = lambda a, w: np.arange(a, a + w)
    order = [r(0, 3 * R_WIDTH), r(o_zr, R_WIDTH), r(o_q, A_WIDTH), r(o_za, A_WIDTH),
             r(o_g, 4096), r(o_k, A_KV_WIDTH), r(o_v, A_KV_WIDTH), r(o_wd, 2 * LORA), r(o_ad, 2 * LORA)]
    perm = np.concatenate(order)
    assert perm.shape[0] == D_IN and np.array_equal(np.sort(perm), np.arange(D_IN))
    return perm, shift_w


def _pack_lora(up):
    z = jnp.zeros((LORA, PAIRS, LANES), up.dtype)
    top = jnp.concatenate([up[0].reshape(LORA, PAIRS, LANES), z], axis=2)
    bot = jnp.concatenate([z, up[1].reshape(LORA, PAIRS, LANES)], axis=2)
    return jnp.concatenate([top, bot], axis=0).reshape(2 * LORA, 2 * R_WIDTH)


def _pack_dirs(v):
    return jnp.stack([v[0].reshape(PAIRS, LANES), v[1].reshape(PAIRS, LANES)], axis=1).reshape(1, -1)


def _layer(x, norm_g, w_in, shift_mu, w0, w_up, a0, a_up, k_k, k_a, r_k, gn_w, gn_b,
           q_norm_g, k_norm_g, w_branch_rwkv, w_branch_attn, w_out, out_g):
    b, t, d = x.shape
    assert t % CHUNK == 0 and t % GRID_W == 0 and d == 2048 and w_in.shape[1] == D_IN
    perm, shift_w = _column_permutation()
    w_perm = jnp.take(w_in, perm, axis=1).astype(BF16)
    mu_full = jnp.concatenate([shift_mu, jnp.zeros((2, D_IN - shift_w), F32)], axis=1)
    mu_perm = jnp.take(mu_full, perm, axis=1)
    g_perm = jnp.zeros((1, D_IN), F32)
    g_perm = g_perm.at[0, COL_Q:COL_Q + A_WIDTH].set(jnp.tile(q_norm_g * HEAD_DIM ** -0.5, A_Q_HEADS))
    g_perm = g_perm.at[0, COL_K:COL_K + A_KV_WIDTH].set(jnp.tile(k_norm_g, A_KV_HEADS))

    x2d = x.reshape(b * t, d)
    h = _rmsnorm(x2d, norm_g, BF16).reshape(b, t, d)
    proj, lora = _inproj(h, w_perm, mu_perm, g_perm)
    kk, ke0, ke1, bb0, bb1, cin0, cin1, cv = _prep(
        proj, lora, _pack_dirs(w0), _pack_lora(w_up), _pack_dirs(a0), _pack_lora(a_up),
        k_k.reshape(1, -1), k_a.reshape(1, -1), r_k.reshape(1, -1))
    of, ob = _wkv(proj, kk, ke0, ke1, bb0, bb1, cin0, cin1)
    attn = _attention(proj)
    n = b * t
    out = _post(x2d, of.reshape(n, -1), ob.reshape(n, -1), cv.reshape(n, -1), proj.reshape(n, D_IN),
                attn.reshape(n, -1), gn_w, gn_b, w_branch_rwkv.astype(BF16), w_branch_attn.astype(BF16),
                w_out.astype(BF16), out_g)
    return out.reshape(b, t, d)


def kernel(x, norm_g, w_in, shift_mu, w0, w_up, a0, a_up, k_k, k_a, r_k, gn_w, gn_b, q_norm_g, k_norm_g,
           w_branch_rwkv, w_branch_attn, w_out, final_norm_g):
    assert norm_g.shape[0] == 1, "single-layer block"
    return _layer(x, norm_g[0], w_in[0], shift_mu[0], w0[0], w_up[0], a0[0], a_up[0], k_k[0], k_a[0],
                  r_k[0], gn_w[0], gn_b[0], q_norm_g[0], k_norm_g[0], w_branch_rwkv[0],
                  w_branch_attn[0], w_out[0], final_norm_g)
```

```python
import functools
import math

import jax
import jax.numpy as jnp
import numpy as np
from jax import lax
from jax.experimental import pallas as pl
from jax.experimental.pallas import tpu as pltpu

F32 = jnp.float32
BF16 = jnp.bfloat16

HEAD_DIM = 64
R_HEADS = 16
R_WIDTH = R_HEADS * HEAD_DIM
LORA = 64
A_Q_HEADS = 16
A_KV_HEADS = 4
A_GROUP = A_Q_HEADS // A_KV_HEADS
A_WIDTH = A_Q_HEADS * HEAD_DIM
A_KV_WIDTH = A_KV_HEADS * HEAD_DIM
GRID_W = 64
ROPE_THETA = 10000.0
NORM_EPS = 1e-6
GN_EPS = 64e-5
KK_EPS = 1e-24
DECAY_SCALE = math.exp(-0.5)

LANES = 128
VMEM_LIMIT = 56 * 1024 * 1024

CHUNK = 64
PAIRS = R_WIDTH // LANES
WKV_PAIRS_PER_STEP = 8

COL_XR, COL_XK, COL_XV, COL_ZR, COL_Q, COL_ZA = 0, 1024, 2048, 3072, 4096, 5120
COL_GR, COL_GA, COL_K, COL_V, COL_WD, COL_AD = 6144, 8192, 10240, 10496, 10752, 10880
D_IN = 11008
PROJ_TN = 256
PROJ_ROW_CHUNK = 512
PROJ_SHIFT_PAD = 8
ATTN_VT_ROWS = HEAD_DIM + 16
ATTN_KEY_BLOCK = 256
ATTN_SCORE_LOOKAHEAD = 6
LOG2_E = math.log2(math.e)


def _bdot(a, b):
    return jnp.dot(a.astype(BF16), b.astype(BF16), preferred_element_type=F32)


def _bdot_nt(a, b):
    return lax.dot_general(a.astype(BF16), b.astype(BF16), (((1,), (1,)), ((), ())),
                           preferred_element_type=F32)


def _bdot_tn(a, b):
    return lax.dot_general(a.astype(BF16), b.astype(BF16), (((0,), (0,)), ((), ())),
                           preferred_element_type=F32)


def _split2(x):
    hi = x.astype(BF16)
    lo = (x - hi.astype(F32)).astype(BF16)
    return hi, lo


def _dot_exact_rhs(a_bf16, x):
    hi, lo = _split2(x)
    d = lambda y: jnp.dot(a_bf16, y, preferred_element_type=F32)
    return d(hi) + d(lo)


def _dot_exact_lhs(x, b_bf16):
    hi, lo = _split2(x)
    d = lambda y: jnp.dot(y, b_bf16, preferred_element_type=F32)
    return d(hi) + d(lo)


def _dot_f32(x, w):
    xh = x.astype(BF16)
    xl = (x - xh.astype(F32)).astype(BF16)
    wh = w.astype(BF16)
    wl = (w - wh.astype(F32)).astype(BF16)
    d = lambda p, q: jnp.dot(p, q, preferred_element_type=F32)
    return d(xh, wh) + d(xl, wh) + d(xh, wl)


def _head_ones(width=LANES):
    r = lax.broadcasted_iota(jnp.int32, (width, width), 0) // HEAD_DIM
    c = lax.broadcasted_iota(jnp.int32, (width, width), 1) // HEAD_DIM
    return jnp.where(r == c, 1.0, 0.0).astype(BF16)


def _rmsnorm_kernel(x_ref, g_ref, o_ref):
    x = x_ref[...]
    y = x * lax.rsqrt(jnp.mean(x * x, axis=-1, keepdims=True) + NORM_EPS)
    o_ref[...] = (y * g_ref[...]).astype(o_ref.dtype)


def _rmsnorm(x2d, g, out_dtype, tm=512):
    n, d = x2d.shape
    return pl.pallas_call(
        _rmsnorm_kernel,
        grid=(n // tm,),
        in_specs=[pl.BlockSpec((tm, d), lambda i: (i, 0)),
                  pl.BlockSpec((1, d), lambda i: (0, 0))],
        out_specs=pl.BlockSpec((tm, d), lambda i: (i, 0)),
        out_shape=jax.ShapeDtypeStruct((n, d), out_dtype),
        compiler_params=pltpu.CompilerParams(dimension_semantics=("arbitrary",),
                                             vmem_limit_bytes=VMEM_LIMIT),
        name="rmsnorm",
    )(x2d, g.reshape(1, d))


def _rope_tables(t):
    rows = t // GRID_W
    row = np.repeat(np.arange(rows, dtype=np.float32), GRID_W)
    colv = np.tile(np.arange(GRID_W, dtype=np.float32), rows)
    axis_dim = HEAD_DIM // 2
    freqs = jnp.asarray(ROPE_THETA, F32) ** (-jnp.arange(0, axis_dim, 2, dtype=F32) / axis_dim)
    ang = jnp.concatenate([jnp.asarray(row)[:, None] * freqs, jnp.asarray(colv)[:, None] * freqs], axis=-1)
    reps = PROJ_TN // HEAD_DIM
    cos = jnp.tile(jnp.repeat(jnp.cos(ang), 2, axis=-1), (1, reps))
    sin = jnp.tile(jnp.repeat(jnp.sin(ang), 2, axis=-1), (1, reps))
    return cos, sin


def _inproj_kernel(h_ref, w_ref, mu_ref, g_ref, cos_ref, sin_ref, o_ref, lora_ref, acc_ref, *,
                   shift_blocks, lora_block, q_blocks, k_block):
    j = pl.program_id(1)
    t = h_ref.shape[1]
    rc = PROJ_ROW_CHUNK
    nchunk = t // rc
    is_shift = jnp.logical_or(j < shift_blocks, j == lora_block)
    is_rope = jnp.logical_or(jnp.logical_and(j >= q_blocks[0], j < q_blocks[1]), j == k_block)

    def mm(c):
        return jnp.dot(h_ref[0, c * rc:(c + 1) * rc, :], w_ref[...], preferred_element_type=F32)

    @pl.when(is_shift)
    def _():
        pad = PROJ_SHIFT_PAD
        acc_ref[0:pad, :] = jnp.zeros((pad, PROJ_TN), F32)
        acc_ref[pad + t:2 * pad + t, :] = jnp.zeros((pad, PROJ_TN), F32)
        mu = mu_ref[...]

        def epilogue(c):
            lo = pad + c * rc
            cur = acc_ref[lo:lo + rc, :]
            prev = acc_ref[lo - 1:lo - 1 + rc, :]
            nxt = acc_ref[lo + 1:lo + 1 + rc, :]
            y = cur + mu[0:1] * (prev - cur) + mu[1:2] * (nxt - cur)
            o_ref[0, c * rc:(c + 1) * rc, :] = y.astype(o_ref.dtype)
            lora_ref[0, c * rc:(c + 1) * rc, :] = y

        for c in range(nchunk):
            acc_ref[pad + c * rc:pad + (c + 1) * rc, :] = mm(c)
            if c > 0:
                epilogue(c - 1)
        epilogue(nchunk - 1)

    @pl.when(is_rope)
    def _():
        ones = _head_ones(PROJ_TN)
        lane = lax.broadcasted_iota(jnp.int32, (rc, PROJ_TN), 1)
        even = (lane % 2) == 0
        ahead = mm(0)
        for c in range(nchunk):
            rows = slice(c * rc, (c + 1) * rc)
            acc, ahead = ahead, (mm(c + 1) if c + 1 < nchunk else None)
            ss = _dot_exact_lhs(acc * acc, ones) * (1.0 / HEAD_DIM)
            y = acc * lax.rsqrt(ss + NORM_EPS) * g_ref[...]
            nxt = pltpu.roll(y, PROJ_TN - 1, axis=1)
            prv = pltpu.roll(y, 1, axis=1)
            o_ref[0, rows, :] = (y * cos_ref[rows, :]
                                 + jnp.where(even, -nxt, prv) * sin_ref[rows, :]).astype(o_ref.dtype)

    @pl.when(jnp.logical_not(jnp.logical_or(is_shift, is_rope)))
    def _():
        for c in range(nchunk):
            o_ref[0, c * rc:(c + 1) * rc, :] = mm(c).astype(o_ref.dtype)


def _inproj(h, w_perm, mu_perm, g_perm):
    b, t, d = h.shape
    nj = D_IN // PROJ_TN
    cos, sin = _rope_tables(t)
    kern = functools.partial(
        _inproj_kernel, shift_blocks=(3 * R_WIDTH) // PROJ_TN, lora_block=COL_WD // PROJ_TN,
        q_blocks=(COL_Q // PROJ_TN, (COL_Q + A_WIDTH) // PROJ_TN), k_block=COL_K // PROJ_TN)
    return pl.pallas_call(
        kern,
        grid=(b, nj),
        in_specs=[pl.BlockSpec((1, t, d), lambda i, j: (i, 0, 0)),
                  pl.BlockSpec((d, PROJ_TN), lambda i, j: (0, j)),
                  pl.BlockSpec((2, PROJ_TN), lambda i, j: (0, j)),
                  pl.BlockSpec((1, PROJ_TN), lambda i, j: (0, j)),
                  pl.BlockSpec((t, PROJ_TN), lambda i, j: (0, 0)),
                  pl.BlockSpec((t, PROJ_TN), lambda i, j: (0, 0))],
        out_specs=[pl.BlockSpec((1, t, PROJ_TN), lambda i, j: (i, 0, j)),
                   pl.BlockSpec((1, t, PROJ_TN), lambda i, j: (i, 0, 0))],
        out_shape=[jax.ShapeDtypeStruct((b, t, D_IN), BF16),
                   jax.ShapeDtypeStruct((b, t, PROJ_TN), F32)],
        scratch_shapes=[pltpu.VMEM((t + 2 * PROJ_SHIFT_PAD, PROJ_TN), F32)],
        compiler_params=pltpu.CompilerParams(dimension_semantics=("arbitrary", "arbitrary"),
                                             vmem_limit_bytes=VMEM_LIMIT),
        name="inproj",
    )(h, w_perm, mu_perm, g_perm, cos, sin)


def _prep_kernel(xr_ref, xk_ref, xv_ref, lora_ref, w0_ref, wup_ref, a0_ref, aup_ref,
                 kkg_ref, ka_ref, rk_ref,
                 kk_ref, ke0_ref, ke1_ref, bb0_ref, bb1_ref, cin0_ref, cin1_ref, cv_ref):
    L = CHUNK
    ones = _head_ones()
    r = xr_ref[0].astype(F32)
    k = xk_ref[0].astype(F32)
    v = xv_ref[0].astype(F32)
    lora = lora_ref[0]
    wd = jnp.tanh(lora[:, :LANES])
    ad = lora[:, LANES:]
    k_a = ka_ref[...]
    kkn = k * kkg_ref[...]
    ss = _dot_exact_lhs(kkn * kkn, ones)
    w_raw = w0_ref[...] + _dot_f32(wd, wup_ref[...])
    a_raw = a0_ref[...] + _bdot(ad, aup_ref[...])
    kk = kkn * lax.rsqrt(jnp.maximum(ss, KK_EPS))
    kk_ref[0] = kk.astype(kk_ref.dtype)
    lw = -DECAY_SCALE * jax.nn.sigmoid(w_raw)
    a2 = jax.nn.sigmoid(a_raw)
    tt_ = lax.broadcasted_iota(jnp.int32, (L, L), 0)
    ss_ = lax.broadcasted_iota(jnp.int32, (L, L), 1)
    tri = (jnp.where(tt_ >= ss_, 1.0, 0.0).astype(BF16), jnp.where(tt_ <= ss_, 1.0, 0.0).astype(BF16))
    nchunk = lw.shape[0] // L
    ke_sum = None
    for d, (ke_ref, bb_ref, cin_ref) in enumerate(((ke0_ref, bb0_ref, cin0_ref),
                                                   (ke1_ref, bb1_ref, cin1_ref))):
        lanes = slice(d * LANES, (d + 1) * LANES)
        wide = jnp.concatenate([lw[c * L:(c + 1) * L, lanes] for c in range(nchunk)], axis=1)
        cin = _dot_exact_rhs(tri[d], wide)
        for c in range(nchunk):
            cin_ref[0, c * L:(c + 1) * L, :] = cin[:, c * LANES:(c + 1) * LANES]
        a = a2[:, lanes]
        bb_ref[0] = (kk * a).astype(bb_ref.dtype)
        ke = k * (1.0 + (a - 1.0) * k_a)
        ke_ref[0] = ke.astype(ke_ref.dtype)
        ke_sum = ke if ke_sum is None else ke_sum + ke
    coef = _dot_exact_lhs(r * ke_sum * rk_ref[...], ones)
    cv_ref[0] = (coef * v).astype(cv_ref.dtype)


def _prep(proj, lora, w0, wup_pad, a0, aup_pad, k_k, k_a, r_k, tt=512):
    b, t, _ = proj.shape
    col = lambda base: (lambda i, s, p: (i, s, base // LANES + p))
    vec = lambda i, s, p: (0, p)
    tile = pl.BlockSpec((1, tt, LANES), lambda i, s, p: (i, s, p))
    sd = lambda dt: jax.ShapeDtypeStruct((b, t, R_WIDTH), dt)
    return pl.pallas_call(
        _prep_kernel,
        grid=(b, t // tt, PAIRS),
        in_specs=[pl.BlockSpec((1, tt, LANES), col(COL_XR)),
                  pl.BlockSpec((1, tt, LANES), col(COL_XK)),
                  pl.BlockSpec((1, tt, LANES), col(COL_XV)),
                  pl.BlockSpec((1, tt, 2 * LANES), lambda i, s, p: (i, s, 0)),
                  pl.BlockSpec((1, 2 * LANES), vec),
                  pl.BlockSpec((LANES, 2 * LANES), vec),
                  pl.BlockSpec((1, 2 * LANES), vec),
                  pl.BlockSpec((LANES, 2 * LANES), vec),
                  pl.BlockSpec((1, LANES), vec),
                  pl.BlockSpec((1, LANES), vec),
                  pl.BlockSpec((1, LANES), vec)],
        out_specs=[tile] * 8,
        out_shape=[sd(BF16), sd(BF16), sd(BF16), sd(BF16), sd(BF16), sd(F32), sd(F32), sd(BF16)],
        compiler_params=pltpu.CompilerParams(
            dimension_semantics=("arbitrary", "arbitrary", "arbitrary"),
            vmem_limit_bytes=VMEM_LIMIT),
        name="rwkv_prep",
    )(proj, proj, proj, lora, w0, wup_pad, a0, aup_pad, k_k, k_a, r_k)


def _wkv_consts(reverse):
    L = CHUNK
    t = lax.broadcasted_iota(jnp.int32, (L, 2 * L), 0)
    s = lax.broadcasted_iota(jnp.int32, (L, 2 * L), 1) % L
    if reverse:
        strict, incl = t < s, t <= s
    else:
        strict, incl = t > s, t >= s
    eye2 = jnp.where(t == s, 1.0, 0.0).astype(F32)
    rr = lax.broadcasted_iota(jnp.int32, (LANES, LANES), 0)
    cc = lax.broadcasted_iota(jnp.int32, (LANES, LANES), 1)
    bd = (rr // HEAD_DIM) == (cc // HEAD_DIM)
    eye = rr == cc
    row = lax.broadcasted_iota(jnp.int32, (L, LANES), 0)
    first = row == (L - 1 if reverse else 0)
    return strict, incl, eye2, bd, eye, first


def _wkv_chunk(load, consts, reverse):
    L = CHUNK
    strict, incl, eye2, bd, eye, first = consts

    def bd2(x):
        return jnp.where(bd, jnp.concatenate([x, x], axis=0), 0.0)

    cin, kk, b, ke, v, r, h = load()
    a = -kk
    cex = jnp.where(first, 0.0, pltpu.roll(cin, L - 1 if reverse else 1, axis=0))
    cend = cin[0:1, :] if reverse else cin[L - 1:L, :]
    m = 0.5 * cend
    e_in = jnp.exp(cin - m)
    e_out = jnp.exp(m - cin)
    em = jnp.exp(m)
    at = a * jnp.exp(cex - m)
    rt = r * e_in
    bt = b * e_out
    kt = ke * e_out
    ap, rp, bh, kh = at * em, rt * em, bt * em, kt * em

    ar = jnp.concatenate([at, rt], axis=0)
    g = _bdot_nt(ar, jnp.concatenate([bd2(bt), bd2(kt)], axis=0))
    yield
    a_ab = jnp.where(strict, g[:L, :LANES], 0.0)
    a_ak = jnp.where(strict, g[:L, LANES:], 0.0)
    p_rbk = jnp.concatenate([jnp.where(incl, g[L:, :LANES], 0.0),
                             jnp.where(incl, g[L:, LANES:], 0.0)], axis=1)

    n = a_ab
    p = eye2 + n
    n = _bdot(n, bd2(n))
    vb = bd2(v)
    x = _bdot(a_ak, vb)
    yield
    steps = int(math.log2(L)) - 1
    for i in range(steps):
        nb = bd2(n)
        if i < steps - 1:
            np_ = _bdot(jnp.concatenate([n, p], axis=0), nb)
            n = np_[:L]
            p = p + np_[L:]
        else:
            p = p + _bdot(p, nb)
        yield
    tinv = p

    wu = _bdot(tinv, jnp.concatenate([bd2(ap), bd2(x)], axis=1))
    yield
    w, u0 = wu[:, :LANES], wu[:, LANES:]
    zero = jnp.zeros((LANES, LANES), F32)
    rhs = jnp.concatenate([jnp.concatenate([bd2(w), bd2(u0)], axis=1),
                           jnp.concatenate([zero, vb], axis=1)], axis=0)
    pwo = _bdot(p_rbk, rhs)
    q = rp + pwo[:, :LANES]
    o0 = pwo[:, LANES:]
    zl = jnp.zeros((L, LANES), F32)
    mc = _bdot_tn(jnp.concatenate([bh, kh], axis=0),
                  jnp.concatenate([wu, jnp.concatenate([zl, v], axis=1)], axis=0))
    m_off = jnp.where(bd, mc[:, :LANES], 0.0)
    c = jnp.where(bd, mc[:, LANES:], 0.0)
    decay_col = jnp.sum(jnp.where(eye, jnp.exp(cend), 0.0), axis=1, keepdims=True)
    yield
    qm = _bdot(jnp.concatenate([q, m_off], axis=0), h)
    out = qm[:L] + o0
    h_new = decay_col * h + qm[L:] + c
    return out, h_new


def _interleave(gens):
    results = [None] * len(gens)
    live = list(range(len(gens)))
    while live:
        for i in list(live):
            try:
                next(gens[i])
            except StopIteration as stop:
                results[i] = stop.value
                live.remove(i)
    return results


def _wkv_kernel(rf_ref, vf_ref, kkf_ref, kef_ref, bbf_ref, cinf_ref,
                rb_ref, vb_ref, kkb_ref, keb_ref, bbb_ref, cinb_ref,
                of_ref, ob_ref, h_ref):
    @pl.when(pl.program_id(2) == 0)
    def _():
        h_ref[...] = jnp.zeros_like(h_ref)

    dirs = ((cinf_ref, kkf_ref, bbf_ref, kef_ref, vf_ref, rf_ref, of_ref),
            (cinb_ref, kkb_ref, bbb_ref, keb_ref, vb_ref, rb_ref, ob_ref))
    consts = (_wkv_consts(False), _wkv_consts(True))
    gens, sinks = [], []
    for i in range(WKV_PAIRS_PER_STEP):
        sl = slice(i * LANES, (i + 1) * LANES)
        for d in range(2):
            refs = dirs[d]
            load = functools.partial(
                lambda refs, d, i, sl: tuple(ref[0, :, sl].astype(F32) for ref in refs[:6]) + (h_ref[d, i],),
                refs, d, i, sl)
            gens.append(_wkv_chunk(load, consts[d], reverse=(d == 1)))
            sinks.append((refs[6], d, i, sl))
    for (out, h_new), (o_ref, d, i, sl) in zip(_interleave(gens), sinks):
        o_ref[0, :, sl] = out.astype(o_ref.dtype)
        h_ref[d, i] = h_new


def _wkv(proj, kk, ke0, ke1, bb0, bb1, cin0, cin1):
    b, t, _ = proj.shape
    nc = t // CHUNK
    L = CHUNK
    w = WKV_PAIRS_PER_STEP * LANES
    fwd = lambda base: (lambda i, p, c: (i, c, base // w + p))
    bwd = lambda base: (lambda i, p, c: (i, nc - 1 - c, base // w + p))
    blk = lambda im: pl.BlockSpec((1, L, w), im)
    out_sd = jax.ShapeDtypeStruct((b, t, R_WIDTH), BF16)
    return pl.pallas_call(
        _wkv_kernel,
        grid=(b, PAIRS // WKV_PAIRS_PER_STEP, nc),
        in_specs=[blk(fwd(COL_XR)), blk(fwd(COL_XV)), blk(fwd(0)), blk(fwd(0)), blk(fwd(0)), blk(fwd(0)),
                  blk(bwd(COL_XR)), blk(bwd(COL_XV)), blk(bwd(0)), blk(bwd(0)), blk(bwd(0)), blk(bwd(0))],
        out_specs=[blk(fwd(0)), blk(bwd(0))],
        out_shape=[out_sd, out_sd],
        scratch_shapes=[pltpu.VMEM((2, WKV_PAIRS_PER_STEP, LANES, LANES), F32)],
        compiler_params=pltpu.CompilerParams(
            dimension_semantics=("arbitrary", "arbitrary", "arbitrary"),
            vmem_limit_bytes=VMEM_LIMIT),
        name="wkv",
    )(proj, proj, kk, ke0, bb0, cin0, proj, proj, kk, ke1, bb1, cin1)


def _attn_kernel(q_ref, k_ref, v_ref, o_ref, krep_ref, vt_ref, *, tq):
    w = A_KV_WIDTH

    @pl.when(pl.program_id(1) == 0)
    def _():
        r = lax.broadcasted_iota(jnp.int32, (w, w), 0)
        c = lax.broadcasted_iota(jnp.int32, (w, w), 1)
        kf = k_ref[0]
        for hk in range(A_KV_HEADS):
            sel = jnp.where(r == hk * HEAD_DIM + c % HEAD_DIM, 1.0, 0.0).astype(BF16)
            krep_ref[hk] = jnp.dot(kf, sel, preferred_element_type=F32).astype(BF16)
        vt = v_ref[0].astype(F32).T.astype(BF16)
        ones = jnp.ones((ATTN_VT_ROWS - HEAD_DIM, vt.shape[1]), BF16)
        for hk in range(A_KV_HEADS):
            vt_ref[hk] = jnp.concatenate([vt[hk * HEAD_DIM:(hk + 1) * HEAD_DIM, :], ones], axis=0)

    lane = lax.broadcasted_iota(jnp.int32, (tq, w), 1) // HEAD_DIM

    t = krep_ref.shape[1]
    kb = ATTN_KEY_BLOCK
    qst = []
    for hk in range(A_KV_HEADS):
        qg = q_ref[0, :, hk * w:(hk + 1) * w]
        zero = jnp.zeros_like(qg)
        qst.append(jnp.concatenate([jnp.where(lane == g, qg, zero) for g in range(A_GROUP)], axis=0))

    def scores(hk, j):
        return lax.dot_general(krep_ref[hk, j * kb:(j + 1) * kb, :], qst[hk], (((1,), (1,)), ((), ())),
                               preferred_element_type=F32)

    blocks = [(hk, j) for hk in range(A_KV_HEADS) for j in range(t // kb)]
    rows = []
    ahead = [scores(*blk) for blk in blocks[:ATTN_SCORE_LOOKAHEAD]]
    for i, (hk, j) in enumerate(blocks):
        s = ahead.pop(0)
        if i + ATTN_SCORE_LOOKAHEAD < len(blocks):
            ahead.append(scores(*blocks[i + ATTN_SCORE_LOOKAHEAD]))
        if j == 0:
            m = jnp.full((1, A_GROUP * tq), -1e30, F32)
            acc = jnp.zeros((ATTN_VT_ROWS, A_GROUP * tq), F32)
        m_new = jnp.maximum(m, jnp.max(s, axis=0, keepdims=True))
        p = jnp.exp2(s - m_new).astype(BF16)
        acc = acc * jnp.exp2(m - m_new) + jnp.dot(vt_ref[hk, :, j * kb:(j + 1) * kb], p,
                                                  preferred_element_type=F32)
        m = m_new
        if j == t // kb - 1:
            ot = acc[:HEAD_DIM] / acc[HEAD_DIM:HEAD_DIM + 1]
            rows += [ot[:, g * tq:(g + 1) * tq] for g in range(A_GROUP)]
    o_ref[0] = jnp.concatenate(rows, axis=0).T.astype(o_ref.dtype)


def _attention(proj, tq=128):
    b, t, _ = proj.shape
    kern = functools.partial(_attn_kernel, tq=tq)
    return pl.pallas_call(
        kern,
        grid=(b, t // tq),
        in_specs=[pl.BlockSpec((1, tq, A_WIDTH), lambda i, j: (i, j, COL_Q // A_WIDTH)),
                  pl.BlockSpec((1, t, A_KV_WIDTH), lambda i, j: (i, 0, COL_K // A_KV_WIDTH)),
                  pl.BlockSpec((1, t, A_KV_WIDTH), lambda i, j: (i, 0, COL_V // A_KV_WIDTH))],
        out_specs=pl.BlockSpec((1, tq, A_WIDTH), lambda i, j: (i, j, 0)),
        out_shape=jax.ShapeDtypeStruct((b, t, A_WIDTH), BF16),
        scratch_shapes=[pltpu.VMEM((A_KV_HEADS, t, A_KV_WIDTH), BF16),
                        pltpu.VMEM((A_KV_HEADS, ATTN_VT_ROWS, t), BF16)],
        compiler_params=pltpu.CompilerParams(dimension_semantics=("arbitrary", "arbitrary"),
                                             vmem_limit_bytes=VMEM_LIMIT),
        name="attention",
    )(proj, proj, proj)


def _post_kernel(x_ref, of_ref, ob_ref, cv_ref, zr_ref, at_ref, za_ref, gr_ref, ga_ref,
                 gnw_ref, gnb_ref, wbr_ref, wba_ref, wout_ref, fg_ref, o_ref):
    ones = _head_ones()
    wkv = of_ref[...].astype(F32) + ob_ref[...].astype(F32)
    parts = []
    for p in range(PAIRS):
        sl = slice(p * LANES, (p + 1) * LANES)
        y = wkv[:, sl]
        mu = _dot_exact_lhs(y, ones) * (1.0 / HEAD_DIM)
        yc = y - mu
        var = _dot_exact_lhs(yc * yc, ones) * (1.0 / HEAD_DIM)
        parts.append(yc * lax.rsqrt(var + GN_EPS))
    gn = jnp.concatenate(parts, axis=1) * gnw_ref[...] + gnb_ref[...]
    zr = zr_ref[...].astype(F32)
    o_r = (gn + cv_ref[...].astype(F32)) * (zr * jax.nn.sigmoid(zr))
    za = za_ref[...].astype(F32)
    o_a = at_ref[...].astype(F32) * (za * jax.nn.sigmoid(za))
    p_r = _bdot(o_r, wbr_ref[...])
    p_a = _bdot(o_a, wba_ref[...])
    merged = (jax.nn.sigmoid(gr_ref[...].astype(F32)) * p_r
              + jax.nn.sigmoid(ga_ref[...].astype(F32)) * p_a)
    y = x_ref[...] + _bdot(merged, wout_ref[...])
    yn = y * lax.rsqrt(jnp.mean(y * y, axis=-1, keepdims=True) + NORM_EPS)
    o_ref[...] = yn * fg_ref[...]


def _post(x2d, of, ob, cv, proj2d, attn, gn_w, gn_b, w_br, w_ba, w_out, final_g, tm=256):
    n, d = x2d.shape
    row = lambda w, base: pl.BlockSpec((tm, w), lambda i: (i, base // w))
    const = lambda shape: pl.BlockSpec(shape, lambda i: (0, 0))
    return pl.pallas_call(
        _post_kernel,
        grid=(n // tm,),
        in_specs=[row(d, 0), row(R_WIDTH, 0), row(R_WIDTH, 0), row(R_WIDTH, 0),
                  row(R_WIDTH, COL_ZR), row(A_WIDTH, 0), row(A_WIDTH, COL_ZA),
                  row(d, COL_GR), row(d, COL_GA),
                  const((1, R_WIDTH)), const((1, R_WIDTH)),
                  const((R_WIDTH, d)), const((A_WIDTH, d)), const((d, d)), const((1, d))],
        out_specs=row(d, 0),
        out_shape=jax.ShapeDtypeStruct((n, d), F32),
        compiler_params=pltpu.CompilerParams(dimension_semantics=("arbitrary",),
                                             vmem_limit_bytes=VMEM_LIMIT),
        name="merge_out",
    )(x2d, of, ob, cv, proj2d, attn, proj2d, proj2d, proj2d,
      gn_w.reshape(1, -1), gn_b.reshape(1, -1), w_br, w_ba, w_out, final_g.reshape(1, -1))


def _column_segments():
    shift_w = 3 * R_WIDTH + 4 * LORA
    o_wd, o_ad = 3 * R_WIDTH, 3 * R_WIDTH + 2 * LORA
    o_zr = shift_w
    o_q = o_zr + R_WIDTH
    o_k = o_q + A_WIDTH
    o_v = o_k + A_KV_WIDTH
    o_za = o_v + A_KV_WIDTH
    o_g = o_za + A_WIDTH
    segs = [(0, 3 * R_WIDTH), (o_zr, R_WIDTH), (o_q, A_WIDTH), (o_za, A_WIDTH), (o_g, 4096),
            (o_k, A_KV_WIDTH), (o_v, A_KV_WIDTH), (o_wd, 2 * LORA), (o_ad, 2 * LORA)]
    assert sum(w for _, w in segs) == D_IN
    return segs, shift_w


def _permute_columns(a, segs):
    return jnp.concatenate([a[..., s:s + w] for s, w in segs], axis=-1)


def _pack_lora(up):
    z = jnp.zeros((LORA, PAIRS, LANES), up.dtype)
    top = jnp.concatenate([up[0].reshape(LORA, PAIRS, LANES), z], axis=2)
    bot = jnp.concatenate([z, up[1].reshape(LORA, PAIRS, LANES)], axis=2)
    return jnp.concatenate([top, bot], axis=0).reshape(2 * LORA, 2 * R_WIDTH)


def _pack_dirs(v):
    return jnp.stack([v[0].reshape(PAIRS, LANES), v[1].reshape(PAIRS, LANES)], axis=1).reshape(1, -1)


def _layer(x, norm_g, w_in, shift_mu, w0, w_up, a0, a_up, k_k, k_a, r_k, gn_w, gn_b,
           q_norm_g, k_norm_g, w_branch_rwkv, w_branch_attn, w_out, out_g):
    b, t, d = x.shape
    assert t % CHUNK == 0 and t % GRID_W == 0 and d == 2048 and w_in.shape[1] == D_IN
    segs, shift_w = _column_segments()
    w_perm = _permute_columns(w_in.astype(BF16), segs)
    mu_full = jnp.concatenate([shift_mu, jnp.zeros((2, D_IN - shift_w), F32)], axis=1)
    mu_perm = _permute_columns(mu_full, segs)
    g_perm = jnp.zeros((1, D_IN), F32)
    g_perm = g_perm.at[0, COL_Q:COL_Q + A_WIDTH].set(jnp.tile(q_norm_g * (HEAD_DIM ** -0.5 * LOG2_E), A_Q_HEADS))
    g_perm = g_perm.at[0, COL_K:COL_K + A_KV_WIDTH].set(jnp.tile(k_norm_g, A_KV_HEADS))

    x2d = x.reshape(b * t, d)
    h = _rmsnorm(x2d, norm_g, BF16).reshape(b, t, d)
    proj, lora = _inproj(h, w_perm, mu_perm, g_perm)
    kk, ke0, ke1, bb0, bb1, cin0, cin1, cv = _prep(
        proj, lora, _pack_dirs(w0), _pack_lora(w_up), _pack_dirs(a0), _pack_lora(a_up),
        k_k.reshape(1, -1), k_a.reshape(1, -1), r_k.reshape(1, -1))
    of, ob = _wkv(proj, kk, ke0, ke1, bb0, bb1, cin0, cin1)
    attn = _attention(proj)
    n = b * t
    out = _post(x2d, of.reshape(n, -1), ob.reshape(n, -1), cv.reshape(n, -1), proj.reshape(n, D_IN),
                attn.reshape(n, -1), gn_w, gn_b, w_branch_rwkv.astype(BF16), w_branch_attn.astype(BF16),
                w_out.astype(BF16), out_g)
    return out.reshape(b, t, d)


def kernel(x, norm_g, w_in, shift_mu, w0, w_up, a0, a_up, k_k, k_a, r_k, gn_w, gn_b, q_norm_g, k_norm_g,
           w_branch_rwkv, w_branch_attn, w_out, final_norm_g):
    assert norm_g.shape[0] == 1, "single-layer block"
    return _layer(x, norm_g[0], w_in[0], shift_mu[0], w0[0], w_up[0], a0[0], a_up[0], k_k[0], k_a[0],
                  r_k[0], gn_w[0], gn_b[0], q_norm_g[0], k_norm_g[0], w_branch_rwkv[0],
                  w_branch_attn[0], w_out[0], final_norm_g)
```

```python
import functools
import math

import jax
import jax.numpy as jnp
import numpy as np
from jax import lax
from jax.experimental import pallas as pl
from jax.experimental.pallas import tpu as pltpu

F32 = jnp.float32
BF16 = jnp.bfloat16

HEAD_DIM = 64
R_HEADS = 16
R_WIDTH = R_HEADS * HEAD_DIM
LORA = 64
A_Q_HEADS = 16
A_KV_HEADS = 4
A_GROUP = A_Q_HEADS // A_KV_HEADS
A_WIDTH = A_Q_HEADS * HEAD_DIM
A_KV_WIDTH = A_KV_HEADS * HEAD_DIM
GRID_W = 64
ROPE_THETA = 10000.0
NORM_EPS = 1e-6
GN_EPS = 64e-5
KK_EPS = 1e-24
DECAY_SCALE = math.exp(-0.5)

LANES = 128
VMEM_LIMIT = 56 * 1024 * 1024

CHUNK = 64
PAIRS = R_WIDTH // LANES
WKV_PAIRS_PER_STEP = 8

COL_XR, COL_XK, COL_XV, COL_ZR, COL_Q, COL_ZA = 0, 1024, 2048, 3072, 4096, 5120
COL_GR, COL_GA, COL_K, COL_V, COL_WD, COL_AD = 6144, 8192, 10240, 10496, 10752, 10880
D_IN = 11008
PROJ_TN = 256
PROJ_ROW_CHUNK = 512
PROJ_SHIFT_PAD = 8
ATTN_VT_ROWS = HEAD_DIM + 16
ATTN_KEY_BLOCK = 128
ATTN_QUERY_TILE = 256
ATTN_SCORE_LOOKAHEAD = 8
LOG2_E = math.log2(math.e)


def _bdot(a, b):
    return jnp.dot(a.astype(BF16), b.astype(BF16), preferred_element_type=F32)


def _bdot_nt(a, b):
    return lax.dot_general(a.astype(BF16), b.astype(BF16), (((1,), (1,)), ((), ())),
                           preferred_element_type=F32)


def _bdot_tn(a, b):
    return lax.dot_general(a.astype(BF16), b.astype(BF16), (((0,), (0,)), ((), ())),
                           preferred_element_type=F32)


def _split2(x):
    hi = x.astype(BF16)
    lo = (x - hi.astype(F32)).astype(BF16)
    return hi, lo


def _dot_exact_rhs(a_bf16, x):
    hi, lo = _split2(x)
    d = lambda y: jnp.dot(a_bf16, y, preferred_element_type=F32)
    return d(hi) + d(lo)


def _dot_exact_lhs(x, b_bf16):
    hi, lo = _split2(x)
    d = lambda y: jnp.dot(y, b_bf16, preferred_element_type=F32)
    return d(hi) + d(lo)


def _dot_f32(x, w):
    xh = x.astype(BF16)
    xl = (x - xh.astype(F32)).astype(BF16)
    wh = w.astype(BF16)
    wl = (w - wh.astype(F32)).astype(BF16)
    d = lambda p, q: jnp.dot(p, q, preferred_element_type=F32)
    return d(xh, wh) + d(xl, wh) + d(xh, wl)


def _head_ones(width=LANES):
    r = lax.broadcasted_iota(jnp.int32, (width, width), 0) // HEAD_DIM
    c = lax.broadcasted_iota(jnp.int32, (width, width), 1) // HEAD_DIM
    return jnp.where(r == c, 1.0, 0.0).astype(BF16)


def _rmsnorm_kernel(x_ref, g_ref, o_ref):
    x = x_ref[...]
    y = x * lax.rsqrt(jnp.mean(x * x, axis=-1, keepdims=True) + NORM_EPS)
    o_ref[...] = (y * g_ref[...]).astype(o_ref.dtype)


def _rmsnorm(x2d, g, out_dtype, tm=512):
    n, d = x2d.shape
    return pl.pallas_call(
        _rmsnorm_kernel,
        grid=(n // tm,),
        in_specs=[pl.BlockSpec((tm, d), lambda i: (i, 0)),
                  pl.BlockSpec((1, d), lambda i: (0, 0))],
        out_specs=pl.BlockSpec((tm, d), lambda i: (i, 0)),
        out_shape=jax.ShapeDtypeStruct((n, d), out_dtype),
        compiler_params=pltpu.CompilerParams(dimension_semantics=("arbitrary",),
                                             vmem_limit_bytes=VMEM_LIMIT),
        name="rmsnorm",
    )(x2d, g.reshape(1, d))


def _rope_tables(t):
    rows = t // GRID_W
    row = np.repeat(np.arange(rows, dtype=np.float32), GRID_W)
    colv = np.tile(np.arange(GRID_W, dtype=np.float32), rows)
    axis_dim = HEAD_DIM // 2
    freqs = jnp.asarray(ROPE_THETA, F32) ** (-jnp.arange(0, axis_dim, 2, dtype=F32) / axis_dim)
    ang = jnp.concatenate([jnp.asarray(row)[:, None] * freqs, jnp.asarray(colv)[:, None] * freqs], axis=-1)
    reps = PROJ_TN // HEAD_DIM
    cos = jnp.tile(jnp.repeat(jnp.cos(ang), 2, axis=-1), (1, reps))
    sin = jnp.tile(jnp.repeat(jnp.sin(ang), 2, axis=-1), (1, reps))
    return cos, sin


def _inproj_kernel(h_ref, w_ref, mu_ref, g_ref, cos_ref, sin_ref, o_ref, lora_ref, acc_ref, *,
                   shift_blocks, lora_block, q_blocks, k_block):
    j = pl.program_id(1)
    t = h_ref.shape[1]
    rc = PROJ_ROW_CHUNK
    nchunk = t // rc
    is_shift = jnp.logical_or(j < shift_blocks, j == lora_block)
    is_rope = jnp.logical_or(jnp.logical_and(j >= q_blocks[0], j < q_blocks[1]), j == k_block)

    def mm(c):
        return jnp.dot(h_ref[0, c * rc:(c + 1) * rc, :], w_ref[...], preferred_element_type=F32)

    @pl.when(is_shift)
    def _():
        pad = PROJ_SHIFT_PAD
        acc_ref[0:pad, :] = jnp.zeros((pad, PROJ_TN), F32)
        acc_ref[pad + t:2 * pad + t, :] = jnp.zeros((pad, PROJ_TN), F32)
        mu = mu_ref[...]

        def epilogue(c):
            lo = pad + c * rc
            cur = acc_ref[lo:lo + rc, :]
            prev = acc_ref[lo - 1:lo - 1 + rc, :]
            nxt = acc_ref[lo + 1:lo + 1 + rc, :]
            y = cur + mu[0:1] * (prev - cur) + mu[1:2] * (nxt - cur)
            o_ref[0, c * rc:(c + 1) * rc, :] = y.astype(o_ref.dtype)
            lora_ref[0, c * rc:(c + 1) * rc, :] = y

        for c in range(nchunk):
            acc_ref[pad + c * rc:pad + (c + 1) * rc, :] = mm(c)
            if c > 0:
                epilogue(c - 1)
        epilogue(nchunk - 1)

    @pl.when(is_rope)
    def _():
        ones = _head_ones(PROJ_TN)
        lane = lax.broadcasted_iota(jnp.int32, (rc, PROJ_TN), 1)
        even = (lane % 2) == 0
        ahead = mm(0)
        for c in range(nchunk):
            rows = slice(c * rc, (c + 1) * rc)
            acc, ahead = ahead, (mm(c + 1) if c + 1 < nchunk else None)
            ss = _dot_exact_lhs(acc * acc, ones) * (1.0 / HEAD_DIM)
            y = acc * lax.rsqrt(ss + NORM_EPS) * g_ref[...]
            nxt = pltpu.roll(y, PROJ_TN - 1, axis=1)
            prv = pltpu.roll(y, 1, axis=1)
            o_ref[0, rows, :] = (y * cos_ref[rows, :]
                                 + jnp.where(even, -nxt, prv) * sin_ref[rows, :]).astype(o_ref.dtype)

    @pl.when(jnp.logical_not(jnp.logical_or(is_shift, is_rope)))
    def _():
        for c in range(nchunk):
            o_ref[0, c * rc:(c + 1) * rc, :] = mm(c).astype(o_ref.dtype)


def _inproj(h, w_perm, mu_perm, g_perm):
    b, t, d = h.shape
    nj = D_IN // PROJ_TN
    cos, sin = _rope_tables(t)
    kern = functools.partial(
        _inproj_kernel, shift_blocks=(3 * R_WIDTH) // PROJ_TN, lora_block=COL_WD // PROJ_TN,
        q_blocks=(COL_Q // PROJ_TN, (COL_Q + A_WIDTH) // PROJ_TN), k_block=COL_K // PROJ_TN)
    return pl.pallas_call(
        kern,
        grid=(b, nj),
        in_specs=[pl.BlockSpec((1, t, d), lambda i, j: (i, 0, 0)),
                  pl.BlockSpec((d, PROJ_TN), lambda i, j: (0, j)),
                  pl.BlockSpec((2, PROJ_TN), lambda i, j: (0, j)),
                  pl.BlockSpec((1, PROJ_TN), lambda i, j: (0, j)),
                  pl.BlockSpec((t, PROJ_TN), lambda i, j: (0, 0)),
                  pl.BlockSpec((t, PROJ_TN), lambda i, j: (0, 0))],
        out_specs=[pl.BlockSpec((1, t, PROJ_TN), lambda i, j: (i, 0, j)),
                   pl.BlockSpec((1, t, PROJ_TN), lambda i, j: (i, 0, 0))],
        out_shape=[jax.ShapeDtypeStruct((b, t, D_IN), BF16),
                   jax.ShapeDtypeStruct((b, t, PROJ_TN), F32)],
        scratch_shapes=[pltpu.VMEM((t + 2 * PROJ_SHIFT_PAD, PROJ_TN), F32)],
        compiler_params=pltpu.CompilerParams(dimension_semantics=("arbitrary", "arbitrary"),
                                             vmem_limit_bytes=VMEM_LIMIT),
        name="inproj",
    )(h, w_perm, mu_perm, g_perm, cos, sin)


def _prep_kernel(xr_ref, xk_ref, xv_ref, lora_ref, w0_ref, wup_ref, a0_ref, aup_ref,
                 kkg_ref, ka_ref, rk_ref,
                 kk_ref, ke0_ref, ke1_ref, bb0_ref, bb1_ref, cin0_ref, cin1_ref, cv_ref):
    L = CHUNK
    ones = _head_ones()
    r = xr_ref[0].astype(F32)
    k = xk_ref[0].astype(F32)
    v = xv_ref[0].astype(F32)
    lora = lora_ref[0]
    wd = jnp.tanh(lora[:, :LANES])
    ad = lora[:, LANES:]
    k_a = ka_ref[...]
    kkn = k * kkg_ref[...]
    ss = _dot_exact_lhs(kkn * kkn, ones)
    w_raw = w0_ref[...] + _dot_f32(wd, wup_ref[...])
    a_raw = a0_ref[...] + _bdot(ad, aup_ref[...])
    kk = kkn * lax.rsqrt(jnp.maximum(ss, KK_EPS))
    kk_ref[0] = kk.astype(kk_ref.dtype)
    lw = -DECAY_SCALE * jax.nn.sigmoid(w_raw)
    a2 = jax.nn.sigmoid(a_raw)
    tt_ = lax.broadcasted_iota(jnp.int32, (L, L), 0)
    ss_ = lax.broadcasted_iota(jnp.int32, (L, L), 1)
    tri = (jnp.where(tt_ >= ss_, 1.0, 0.0).astype(BF16), jnp.where(tt_ <= ss_, 1.0, 0.0).astype(BF16))
    nchunk = lw.shape[0] // L
    ke_sum = None
    for d, (ke_ref, bb_ref, cin_ref) in enumerate(((ke0_ref, bb0_ref, cin0_ref),
                                                   (ke1_ref, bb1_ref, cin1_ref))):
        lanes = slice(d * LANES, (d + 1) * LANES)
        wide = jnp.concatenate([lw[c * L:(c + 1) * L, lanes] for c in range(nchunk)], axis=1)
        cin = _dot_exact_rhs(tri[d], wide)
        for c in range(nchunk):
            cin_ref[0, c * L:(c + 1) * L, :] = cin[:, c * LANES:(c + 1) * LANES]
        a = a2[:, lanes]
        bb_ref[0] = (kk * a).astype(bb_ref.dtype)
        ke = k * (1.0 + (a - 1.0) * k_a)
        ke_ref[0] = ke.astype(ke_ref.dtype)
        ke_sum = ke if ke_sum is None else ke_sum + ke
    coef = _dot_exact_lhs(r * ke_sum * rk_ref[...], ones)
    cv_ref[0] = (coef * v).astype(cv_ref.dtype)


def _prep(proj, lora, w0, wup_pad, a0, aup_pad, k_k, k_a, r_k, tt=512):
    b, t, _ = proj.shape
    col = lambda base: (lambda i, s, p: (i, s, base // LANES + p))
    vec = lambda i, s, p: (0, p)
    tile = pl.BlockSpec((1, tt, LANES), lambda i, s, p: (i, s, p))
    sd = lambda dt: jax.ShapeDtypeStruct((b, t, R_WIDTH), dt)
    return pl.pallas_call(
        _prep_kernel,
        grid=(b, t // tt, PAIRS),
        in_specs=[pl.BlockSpec((1, tt, LANES), col(COL_XR)),
                  pl.BlockSpec((1, tt, LANES), col(COL_XK)),
                  pl.BlockSpec((1, tt, LANES), col(COL_XV)),
                  pl.BlockSpec((1, tt, 2 * LANES), lambda i, s, p: (i, s, 0)),
                  pl.BlockSpec((1, 2 * LANES), vec),
                  pl.BlockSpec((LANES, 2 * LANES), vec),
                  pl.BlockSpec((1, 2 * LANES), vec),
                  pl.BlockSpec((LANES, 2 * LANES), vec),
                  pl.BlockSpec((1, LANES), vec),
                  pl.BlockSpec((1, LANES), vec),
                  pl.BlockSpec((1, LANES), vec)],
        out_specs=[tile] * 8,
        out_shape=[sd(BF16), sd(BF16), sd(BF16), sd(BF16), sd(BF16), sd(F32), sd(F32), sd(BF16)],
        compiler_params=pltpu.CompilerParams(
            dimension_semantics=("arbitrary", "arbitrary", "arbitrary"),
            vmem_limit_bytes=VMEM_LIMIT),
        name="rwkv_prep",
    )(proj, proj, proj, lora, w0, wup_pad, a0, aup_pad, k_k, k_a, r_k)


def _wkv_consts(reverse):
    L = CHUNK
    t = lax.broadcasted_iota(jnp.int32, (L, 2 * L), 0)
    s = lax.broadcasted_iota(jnp.int32, (L, 2 * L), 1) % L
    if reverse:
        strict, incl = t < s, t <= s
    else:
        strict, incl = t > s, t >= s
    eye2 = jnp.where(t == s, 1.0, 0.0).astype(F32)
    rr = lax.broadcasted_iota(jnp.int32, (LANES, LANES), 0)
    cc = lax.broadcasted_iota(jnp.int32, (LANES, LANES), 1)
    bd = (rr // HEAD_DIM) == (cc // HEAD_DIM)
    eye = rr == cc
    row = lax.broadcasted_iota(jnp.int32, (L, LANES), 0)
    first = row == (L - 1 if reverse else 0)
    return strict, incl, eye2, bd, eye, first


def _wkv_chunk(load, consts, reverse):
    L = CHUNK
    strict, incl, eye2, bd, eye, first = consts

    def bd2(x):
        return jnp.where(bd, jnp.concatenate([x, x], axis=0), 0.0)

    cin, kk, b, ke, v, r, h = load()
    a = -kk
    cex = jnp.where(first, 0.0, pltpu.roll(cin, L - 1 if reverse else 1, axis=0))
    cend = cin[0:1, :] if reverse else cin[L - 1:L, :]
    m = 0.5 * cend
    e_in = jnp.exp(cin - m)
    e_out = jnp.exp(m - cin)
    em = jnp.exp(m)
    at = a * jnp.exp(cex - m)
    rt = r * e_in
    bt = b * e_out
    kt = ke * e_out
    ap, rp, bh, kh = at * em, rt * em, bt * em, kt * em

    ar = jnp.concatenate([at, rt], axis=0)
    g = _bdot_nt(ar, jnp.concatenate([bd2(bt), bd2(kt)], axis=0))
    yield
    a_ab = jnp.where(strict, g[:L, :LANES], 0.0)
    a_ak = jnp.where(strict, g[:L, LANES:], 0.0)
    p_rbk = jnp.concatenate([jnp.where(incl, g[L:, :LANES], 0.0),
                             jnp.where(incl, g[L:, LANES:], 0.0)], axis=1)

    n = a_ab
    p = eye2 + n
    n = _bdot(n, bd2(n))
    vb = bd2(v)
    x = _bdot(a_ak, vb)
    yield
    steps = int(math.log2(L)) - 1
    for i in range(steps):
        nb = bd2(n)
        if i < steps - 1:
            np_ = _bdot(jnp.concatenate([n, p], axis=0), nb)
            n = np_[:L]
            p = p + np_[L:]
        else:
            p = p + _bdot(p, nb)
        yield
    tinv = p

    wu = _bdot(tinv, jnp.concatenate([bd2(ap), bd2(x)], axis=1))
    yield
    w, u0 = wu[:, :LANES], wu[:, LANES:]
    zero = jnp.zeros((LANES, LANES), F32)
    rhs = jnp.concatenate([jnp.concatenate([bd2(w), bd2(u0)], axis=1),
                           jnp.concatenate([zero, vb], axis=1)], axis=0)
    pwo = _bdot(p_rbk, rhs)
    q = rp + pwo[:, :LANES]
    o0 = pwo[:, LANES:]
    zl = jnp.zeros((L, LANES), F32)
    mc = _bdot_tn(jnp.concatenate([bh, kh], axis=0),
                  jnp.concatenate([wu, jnp.concatenate([zl, v], axis=1)], axis=0))
    m_off = jnp.where(bd, mc[:, :LANES], 0.0)
    c = jnp.where(bd, mc[:, LANES:], 0.0)
    decay_col = jnp.sum(jnp.where(eye, jnp.exp(cend), 0.0), axis=1, keepdims=True)
    yield
    qm = _bdot(jnp.concatenate([q, m_off], axis=0), h)
    out = qm[:L] + o0
    h_new = decay_col * h + qm[L:] + c
    return out, h_new


def _interleave(gens):
    results = [None] * len(gens)
    live = list(range(len(gens)))
    while live:
        for i in list(live):
            try:
                next(gens[i])
            except StopIteration as stop:
                results[i] = stop.value
                live.remove(i)
    return results


def _wkv_kernel(rf_ref, vf_ref, kkf_ref, kef_ref, bbf_ref, cinf_ref,
                rb_ref, vb_ref, kkb_ref, keb_ref, bbb_ref, cinb_ref,
                of_ref, ob_ref, h_ref):
    @pl.when(pl.program_id(2) == 0)
    def _():
        h_ref[...] = jnp.zeros_like(h_ref)

    dirs = ((cinf_ref, kkf_ref, bbf_ref, kef_ref, vf_ref, rf_ref, of_ref),
            (cinb_ref, kkb_ref, bbb_ref, keb_ref, vb_ref, rb_ref, ob_ref))
    consts = (_wkv_consts(False), _wkv_consts(True))
    gens, sinks = [], []
    for i in range(WKV_PAIRS_PER_STEP):
        sl = slice(i * LANES, (i + 1) * LANES)
        for d in range(2):
            refs = dirs[d]
            load = functools.partial(
                lambda refs, d, i, sl: tuple(ref[0, :, sl].astype(F32) for ref in refs[:6]) + (h_ref[d, i],),
                refs, d, i, sl)
            gens.append(_wkv_chunk(load, consts[d], reverse=(d == 1)))
            sinks.append((refs[6], d, i, sl))
    for (out, h_new), (o_ref, d, i, sl) in zip(_interleave(gens), sinks):
        o_ref[0, :, sl] = out.astype(o_ref.dtype)
        h_ref[d, i] = h_new


def _wkv(proj, kk, ke0, ke1, bb0, bb1, cin0, cin1):
    b, t, _ = proj.shape
    nc = t // CHUNK
    L = CHUNK
    w = WKV_PAIRS_PER_STEP * LANES
    fwd = lambda base: (lambda i, p, c: (i, c, base // w + p))
    bwd = lambda base: (lambda i, p, c: (i, nc - 1 - c, base // w + p))
    blk = lambda im: pl.BlockSpec((1, L, w), im)
    out_sd = jax.ShapeDtypeStruct((b, t, R_WIDTH), BF16)
    return pl.pallas_call(
        _wkv_kernel,
        grid=(b, PAIRS // WKV_PAIRS_PER_STEP, nc),
        in_specs=[blk(fwd(COL_XR)), blk(fwd(COL_XV)), blk(fwd(0)), blk(fwd(0)), blk(fwd(0)), blk(fwd(0)),
                  blk(bwd(COL_XR)), blk(bwd(COL_XV)), blk(bwd(0)), blk(bwd(0)), blk(bwd(0)), blk(bwd(0))],
        out_specs=[blk(fwd(0)), blk(bwd(0))],
        out_shape=[out_sd, out_sd],
        scratch_shapes=[pltpu.VMEM((2, WKV_PAIRS_PER_STEP, LANES, LANES), F32)],
        compiler_params=pltpu.CompilerParams(
            dimension_semantics=("arbitrary", "arbitrary", "arbitrary"),
            vmem_limit_bytes=VMEM_LIMIT),
        name="wkv",
    )(proj, proj, kk, ke0, bb0, cin0, proj, proj, kk, ke1, bb1, cin1)


def _attn_kernel(q_ref, k_ref, v_ref, o_ref, kh_ref, vt_ref, *, tq):
    t = k_ref.shape[1]
    kb = ATTN_KEY_BLOCK
    heads_per_tile = ATTN_QUERY_TILE // tq
    ntile = A_GROUP // heads_per_tile

    @pl.when(pl.program_id(1) == 0)
    def _():
        r = lax.broadcasted_iota(jnp.int32, (A_KV_WIDTH, HEAD_DIM), 0)
        c = lax.broadcasted_iota(jnp.int32, (A_KV_WIDTH, HEAD_DIM), 1)
        kf = k_ref[0]
        vt = v_ref[0].astype(F32).T.astype(BF16)
        ones = jnp.ones((ATTN_VT_ROWS - HEAD_DIM, t), BF16)
        for hk in range(A_KV_HEADS):
            sel = jnp.where(r == hk * HEAD_DIM + c, 1.0, 0.0).astype(BF16)
            kh_ref[hk] = jnp.dot(kf, sel, preferred_element_type=F32).astype(BF16)
            vt_ref[hk] = jnp.concatenate([vt[hk * HEAD_DIM:(hk + 1) * HEAD_DIM, :], ones], axis=0)

    qt = q_ref[0].astype(F32).T.astype(BF16)

    def q_tile(hk, c):
        h0 = hk * A_GROUP + c * heads_per_tile
        return jnp.concatenate([qt[(h0 + e) * HEAD_DIM:(h0 + e + 1) * HEAD_DIM, :]
                                for e in range(heads_per_tile)], axis=1)

    def scores(hk, c, j):
        return jnp.dot(kh_ref[hk, j * kb:(j + 1) * kb, :], q_tile(hk, c), preferred_element_type=F32)

    tasks = [(hk, c, j) for hk in range(A_KV_HEADS) for j in range(t // kb) for c in range(ntile)]
    state = {}
    done = {}
    ahead = [scores(*task) for task in tasks[:ATTN_SCORE_LOOKAHEAD]]
    for i, (hk, c, j) in enumerate(tasks):
        s = ahead.pop(0)
        if i + ATTN_SCORE_LOOKAHEAD < len(tasks):
            ahead.append(scores(*tasks[i + ATTN_SCORE_LOOKAHEAD]))
        if j == 0:
            state[hk, c] = (jnp.full((1, ATTN_QUERY_TILE), -1e30, F32),
                            jnp.zeros((ATTN_VT_ROWS, ATTN_QUERY_TILE), F32))
        m, acc = state[hk, c]
        m_new = jnp.maximum(m, jnp.max(s, axis=0, keepdims=True))
        p = jnp.exp2(s - m_new).astype(BF16)
        acc = acc * jnp.exp2(m - m_new) + jnp.dot(vt_ref[hk, :, j * kb:(j + 1) * kb], p,
                                                  preferred_element_type=F32)
        state[hk, c] = (m_new, acc)
        if j == t // kb - 1:
            ot = acc[:HEAD_DIM] / acc[HEAD_DIM:HEAD_DIM + 1]
            for e in range(heads_per_tile):
                done[hk * A_GROUP + c * heads_per_tile + e] = ot[:, e * tq:(e + 1) * tq]
    rows = [done[h] for h in range(A_Q_HEADS)]
    o_ref[0] = jnp.concatenate(rows, axis=0).T.astype(o_ref.dtype)


def _attention(proj, tq=128):
    b, t, _ = proj.shape
    kern = functools.partial(_attn_kernel, tq=tq)
    return pl.pallas_call(
        kern,
        grid=(b, t // tq),
        in_specs=[pl.BlockSpec((1, tq, A_WIDTH), lambda i, j: (i, j, COL_Q // A_WIDTH)),
                  pl.BlockSpec((1, t, A_KV_WIDTH), lambda i, j: (i, 0, COL_K // A_KV_WIDTH)),
                  pl.BlockSpec((1, t, A_KV_WIDTH), lambda i, j: (i, 0, COL_V // A_KV_WIDTH))],
        out_specs=pl.BlockSpec((1, tq, A_WIDTH), lambda i, j: (i, j, 0)),
        out_shape=jax.ShapeDtypeStruct((b, t, A_WIDTH), BF16),
        scratch_shapes=[pltpu.VMEM((A_KV_HEADS, t, HEAD_DIM), BF16),
                        pltpu.VMEM((A_KV_HEADS, ATTN_VT_ROWS, t), BF16)],
        compiler_params=pltpu.CompilerParams(dimension_semantics=("arbitrary", "arbitrary"),
                                             vmem_limit_bytes=VMEM_LIMIT),
        name="attention",
    )(proj, proj, proj)


def _post_kernel(x_ref, of_ref, ob_ref, cv_ref, zr_ref, at_ref, za_ref, gr_ref, ga_ref,
                 gnw_ref, gnb_ref, wbr_ref, wba_ref, wout_ref, fg_ref, o_ref):
    ones = _head_ones()
    wkv = of_ref[...].astype(F32) + ob_ref[...].astype(F32)
    parts = []
    for p in range(PAIRS):
        sl = slice(p * LANES, (p + 1) * LANES)
        y = wkv[:, sl]
        mu = _dot_exact_lhs(y, ones) * (1.0 / HEAD_DIM)
        yc = y - mu
        var = _dot_exact_lhs(yc * yc, ones) * (1.0 / HEAD_DIM)
        parts.append(yc * lax.rsqrt(var + GN_EPS))
    gn = jnp.concatenate(parts, axis=1) * gnw_ref[...] + gnb_ref[...]
    zr = zr_ref[...].astype(F32)
    o_r = (gn + cv_ref[...].astype(F32)) * (zr * jax.nn.sigmoid(zr))
    za = za_ref[...].astype(F32)
    o_a = at_ref[...].astype(F32) * (za * jax.nn.sigmoid(za))
    p_r = _bdot(o_r, wbr_ref[...])
    p_a = _bdot(o_a, wba_ref[...])
    merged = (jax.nn.sigmoid(gr_ref[...].astype(F32)) * p_r
              + jax.nn.sigmoid(ga_ref[...].astype(F32)) * p_a)
    y = x_ref[...] + _bdot(merged, wout_ref[...])
    yn = y * lax.rsqrt(jnp.mean(y * y, axis=-1, keepdims=True) + NORM_EPS)
    o_ref[...] = yn * fg_ref[...]


def _post(x2d, of, ob, cv, proj2d, attn, gn_w, gn_b, w_br, w_ba, w_out, final_g, tm=256):
    n, d = x2d.shape
    row = lambda w, base: pl.BlockSpec((tm, w), lambda i: (i, base // w))
    const = lambda shape: pl.BlockSpec(shape, lambda i: (0, 0))
    return pl.pallas_call(
        _post_kernel,
        grid=(n // tm,),
        in_specs=[row(d, 0), row(R_WIDTH, 0), row(R_WIDTH, 0), row(R_WIDTH, 0),
                  row(R_WIDTH, COL_ZR), row(A_WIDTH, 0), row(A_WIDTH, COL_ZA),
                  row(d, COL_GR), row(d, COL_GA),
                  const((1, R_WIDTH)), const((1, R_WIDTH)),
                  const((R_WIDTH, d)), const((A_WIDTH, d)), const((d, d)), const((1, d))],
        out_specs=row(d, 0),
        out_shape=jax.ShapeDtypeStruct((n, d), F32),
        compiler_params=pltpu.CompilerParams(dimension_semantics=("arbitrary",),
                                             vmem_limit_bytes=VMEM_LIMIT),
        name="merge_out",
    )(x2d, of, ob, cv, proj2d, attn, proj2d, proj2d, proj2d,
      gn_w.reshape(1, -1), gn_b.reshape(1, -1), w_br, w_ba, w_out, final_g.reshape(1, -1))


def _column_segments():
    shift_w = 3 * R_WIDTH + 4 * LORA
    o_wd, o_ad = 3 * R_WIDTH, 3 * R_WIDTH + 2 * LORA
    o_zr = shift_w
    o_q = o_zr + R_WIDTH
    o_k = o_q + A_WIDTH
    o_v = o_k + A_KV_WIDTH
    o_za = o_v + A_KV_WIDTH
    o_g = o_za + A_WIDTH
    segs = [(0, 3 * R_WIDTH), (o_zr, R_WIDTH), (o_q, A_WIDTH), (o_za, A_WIDTH), (o_g, 4096),
            (o_k, A_KV_WIDTH), (o_v, A_KV_WIDTH), (o_wd, 2 * LORA), (o_ad, 2 * LORA)]
    assert sum(w for _, w in segs) == D_IN
    return segs, shift_w


def _permute_columns(a, segs):
    return jnp.concatenate([a[..., s:s + w] for s, w in segs], axis=-1)


def _pack_lora(up):
    z = jnp.zeros((LORA, PAIRS, LANES), up.dtype)
    top = jnp.concatenate([up[0].reshape(LORA, PAIRS, LANES), z], axis=2)
    bot = jnp.concatenate([z, up[1].reshape(LORA, PAIRS, LANES)], axis=2)
    return jnp.concatenate([top, bot], axis=0).reshape(2 * LORA, 2 * R_WIDTH)


def _pack_dirs(v):
    return jnp.stack([v[0].reshape(PAIRS, LANES), v[1].reshape(PAIRS, LANES)], axis=1).reshape(1, -1)


def _layer(x, norm_g, w_in, shift_mu, w0, w_up, a0, a_up, k_k, k_a, r_k, gn_w, gn_b,
           q_norm_g, k_norm_g, w_branch_rwkv, w_branch_attn, w_out, out_g):
    b, t, d = x.shape
    assert t % CHUNK == 0 and t % GRID_W == 0 and d == 2048 and w_in.shape[1] == D_IN
    segs, shift_w = _column_segments()
    w_perm = _permute_columns(w_in.astype(BF16), segs)
    mu_full = jnp.concatenate([shift_mu, jnp.zeros((2, D_IN - shift_w), F32)], axis=1)
    mu_perm = _permute_columns(mu_full, segs)
    g_perm = jnp.zeros((1, D_IN), F32)
    g_perm = g_perm.at[0, COL_Q:COL_Q + A_WIDTH].set(jnp.tile(q_norm_g * (HEAD_DIM ** -0.5 * LOG2_E), A_Q_HEADS))
    g_perm = g_perm.at[0, COL_K:COL_K + A_KV_WIDTH].set(jnp.tile(k_norm_g, A_KV_HEADS))

    x2d = x.reshape(b * t, d)
    h = _rmsnorm(x2d, norm_g, BF16).reshape(b, t, d)
    proj, lora = _inproj(h, w_perm, mu_perm, g_perm)
    kk, ke0, ke1, bb0, bb1, cin0, cin1, cv = _prep(
        proj, lora, _pack_dirs(w0), _pack_lora(w_up), _pack_dirs(a0), _pack_lora(a_up),
        k_k.reshape(1, -1), k_a.reshape(1, -1), r_k.reshape(1, -1))
    of, ob = _wkv(proj, kk, ke0, ke1, bb0, bb1, cin0, cin1)
    attn = _attention(proj)
    n = b * t
    out = _post(x2d, of.reshape(n, -1), ob.reshape(n, -1), cv.reshape(n, -1), proj.reshape(n, D_IN),
                attn.reshape(n, -1), gn_w, gn_b, w_branch_rwkv.astype(BF16), w_branch_attn.astype(BF16),
                w_out.astype(BF16), out_g)
    return out.reshape(b, t, d)


def kernel(x, norm_g, w_in, shift_mu, w0, w_up, a0, a_up, k_k, k_a, r_k, gn_w, gn_b, q_norm_g, k_norm_g,
           w_branch_rwkv, w_branch_attn, w_out, final_norm_g):
    assert norm_g.shape[0] == 1, "single-layer block"
    return _layer(x, norm_g[0], w_in[0], shift_mu[0], w0[0], w_up[0], a0[0], a_up[0], k_k[0], k_a[0],
                  r_k[0], gn_w[0], gn_b[0], q_norm_g[0], k_norm_g[0], w_branch_rwkv[0],
                  w_branch_attn[0], w_out[0], final_norm_g)
```

```python
import functools
import math

import jax
import jax.numpy as jnp
import numpy as np
from jax import lax
from jax.experimental import pallas as pl
from jax.experimental.pallas import tpu as pltpu

F32 = jnp.float32
BF16 = jnp.bfloat16

HEAD_DIM = 64
R_HEADS = 16
R_WIDTH = R_HEADS * HEAD_DIM
LORA = 64
A_Q_HEADS = 16
A_KV_HEADS = 4
A_GROUP = A_Q_HEADS // A_KV_HEADS
A_WIDTH = A_Q_HEADS * HEAD_DIM
A_KV_WIDTH = A_KV_HEADS * HEAD_DIM
GRID_W = 64
ROPE_THETA = 10000.0
NORM_EPS = 1e-6
GN_EPS = 64e-5
KK_EPS = 1e-24
DECAY_SCALE = math.exp(-0.5)

LANES = 128
VMEM_LIMIT = 56 * 1024 * 1024

CHUNK = 64
PAIRS = R_WIDTH // LANES
POST_GN_WIDTH = 256
PREP_WIDTH = 256
PREP_ROW_BANDS = 2
WKV_PAIRS_PER_STEP = 8

COL_XR, COL_XK, COL_XV, COL_ZR, COL_Q, COL_ZA = 0, 1024, 2048, 3072, 4096, 5120
COL_GR, COL_GA, COL_K, COL_V, COL_WD, COL_AD = 6144, 8192, 10240, 10496, 10752, 10880
D_IN = 11008
PROJ_TN = 256
PROJ_ROW_CHUNK = 512
PROJ_SHIFT_PAD = 8
ATTN_VT_ROWS = HEAD_DIM + 16
ATTN_KEY_BLOCK = 128
ATTN_QUERY_TILE = 256
ATTN_SCORE_LOOKAHEAD = 8
LOG2_E = math.log2(math.e)


def _bdot(a, b):
    return jnp.dot(a.astype(BF16), b.astype(BF16), preferred_element_type=F32)


def _bdot_nt(a, b):
    return lax.dot_general(a.astype(BF16), b.astype(BF16), (((1,), (1,)), ((), ())),
                           preferred_element_type=F32)


def _bdot_tn(a, b):
    return lax.dot_general(a.astype(BF16), b.astype(BF16), (((0,), (0,)), ((), ())),
                           preferred_element_type=F32)


def _split2(x):
    hi = x.astype(BF16)
    lo = (x - hi.astype(F32)).astype(BF16)
    return hi, lo


def _dot_exact_rhs(a_bf16, x):
    hi, lo = _split2(x)
    d = lambda y: jnp.dot(a_bf16, y, preferred_element_type=F32)
    return d(hi) + d(lo)


def _dot_exact_lhs(x, b_bf16):
    hi, lo = _split2(x)
    d = lambda y: jnp.dot(y, b_bf16, preferred_element_type=F32)
    return d(hi) + d(lo)


def _dot_f32(x, w):
    xh = x.astype(BF16)
    xl = (x - xh.astype(F32)).astype(BF16)
    wh = w.astype(BF16)
    wl = (w - wh.astype(F32)).astype(BF16)
    d = lambda p, q: jnp.dot(p, q, preferred_element_type=F32)
    return d(xh, wh) + d(xl, wh) + d(xh, wl)


def _head_ones(width=LANES):
    r = lax.broadcasted_iota(jnp.int32, (width, width), 0) // HEAD_DIM
    c = lax.broadcasted_iota(jnp.int32, (width, width), 1) // HEAD_DIM
    return jnp.where(r == c, 1.0, 0.0).astype(BF16)


def _rmsnorm_kernel(x_ref, g_ref, o_ref):
    x = x_ref[...]
    y = x * lax.rsqrt(jnp.mean(x * x, axis=-1, keepdims=True) + NORM_EPS)
    o_ref[...] = (y * g_ref[...]).astype(o_ref.dtype)


def _rmsnorm(x2d, g, out_dtype, tm=512):
    n, d = x2d.shape
    return pl.pallas_call(
        _rmsnorm_kernel,
        grid=(n // tm,),
        in_specs=[pl.BlockSpec((tm, d), lambda i: (i, 0)),
                  pl.BlockSpec((1, d), lambda i: (0, 0))],
        out_specs=pl.BlockSpec((tm, d), lambda i: (i, 0)),
        out_shape=jax.ShapeDtypeStruct((n, d), out_dtype),
        compiler_params=pltpu.CompilerParams(dimension_semantics=("arbitrary",),
                                             vmem_limit_bytes=VMEM_LIMIT),
        name="rmsnorm",
    )(x2d, g.reshape(1, d))


def _rope_tables(t):
    rows = t // GRID_W
    row = np.repeat(np.arange(rows, dtype=np.float32), GRID_W)
    colv = np.tile(np.arange(GRID_W, dtype=np.float32), rows)
    axis_dim = HEAD_DIM // 2
    freqs = jnp.asarray(ROPE_THETA, F32) ** (-jnp.arange(0, axis_dim, 2, dtype=F32) / axis_dim)
    ang = jnp.concatenate([jnp.asarray(row)[:, None] * freqs, jnp.asarray(colv)[:, None] * freqs], axis=-1)
    reps = PROJ_TN // HEAD_DIM
    cos = jnp.tile(jnp.repeat(jnp.cos(ang), 2, axis=-1), (1, reps))
    sin = jnp.tile(jnp.repeat(jnp.sin(ang), 2, axis=-1), (1, reps))
    return cos, sin


def _inproj_kernel(h_ref, w_ref, mu_ref, g_ref, cos_ref, sin_ref, o_ref, lora_ref, acc_ref, *,
                   shift_blocks, lora_block, q_blocks, k_block):
    j = pl.program_id(1)
    t = h_ref.shape[1]
    rc = PROJ_ROW_CHUNK
    nchunk = t // rc
    is_shift = jnp.logical_or(j < shift_blocks, j == lora_block)
    is_rope = jnp.logical_or(jnp.logical_and(j >= q_blocks[0], j < q_blocks[1]), j == k_block)

    def mm(c):
        return jnp.dot(h_ref[0, c * rc:(c + 1) * rc, :], w_ref[...], preferred_element_type=F32)

    @pl.when(is_shift)
    def _():
        pad = PROJ_SHIFT_PAD
        acc_ref[0:pad, :] = jnp.zeros((pad, PROJ_TN), F32)
        acc_ref[pad + t:2 * pad + t, :] = jnp.zeros((pad, PROJ_TN), F32)
        mu = mu_ref[...]

        def epilogue(c):
            lo = pad + c * rc
            cur = acc_ref[lo:lo + rc, :]
            prev = acc_ref[lo - 1:lo - 1 + rc, :]
            nxt = acc_ref[lo + 1:lo + 1 + rc, :]
            y = cur + mu[0:1] * (prev - cur) + mu[1:2] * (nxt - cur)
            o_ref[0, c * rc:(c + 1) * rc, :] = y.astype(o_ref.dtype)
            lora_ref[0, c * rc:(c + 1) * rc, :] = y

        for c in range(nchunk):
            acc_ref[pad + c * rc:pad + (c + 1) * rc, :] = mm(c)
            if c > 0:
                epilogue(c - 1)
        epilogue(nchunk - 1)

    @pl.when(is_rope)
    def _():
        ones = _head_ones(PROJ_TN)
        lane = lax.broadcasted_iota(jnp.int32, (rc, PROJ_TN), 1)
        even = (lane % 2) == 0
        ahead = mm(0)
        for c in range(nchunk):
            rows = slice(c * rc, (c + 1) * rc)
            acc, ahead = ahead, (mm(c + 1) if c + 1 < nchunk else None)
            ss = _dot_exact_lhs(acc * acc, ones) * (1.0 / HEAD_DIM)
            y = acc * lax.rsqrt(ss + NORM_EPS) * g_ref[...]
            nxt = pltpu.roll(y, PROJ_TN - 1, axis=1)
            prv = pltpu.roll(y, 1, axis=1)
            o_ref[0, rows, :] = (y * cos_ref[rows, :]
                                 + jnp.where(even, -nxt, prv) * sin_ref[rows, :]).astype(o_ref.dtype)

    @pl.when(jnp.logical_not(jnp.logical_or(is_shift, is_rope)))
    def _():
        for c in range(nchunk):
            o_ref[0, c * rc:(c + 1) * rc, :] = mm(c).astype(o_ref.dtype)


def _inproj(h, w_perm, mu_perm, g_perm):
    b, t, d = h.shape
    nj = D_IN // PROJ_TN
    cos, sin = _rope_tables(t)
    kern = functools.partial(
        _inproj_kernel, shift_blocks=(3 * R_WIDTH) // PROJ_TN, lora_block=COL_WD // PROJ_TN,
        q_blocks=(COL_Q // PROJ_TN, (COL_Q + A_WIDTH) // PROJ_TN), k_block=COL_K // PROJ_TN)
    return pl.pallas_call(
        kern,
        grid=(b, nj),
        in_specs=[pl.BlockSpec((1, t, d), lambda i, j: (i, 0, 0)),
                  pl.BlockSpec((d, PROJ_TN), lambda i, j: (0, j)),
                  pl.BlockSpec((2, PROJ_TN), lambda i, j: (0, j)),
                  pl.BlockSpec((1, PROJ_TN), lambda i, j: (0, j)),
                  pl.BlockSpec((t, PROJ_TN), lambda i, j: (0, 0)),
                  pl.BlockSpec((t, PROJ_TN), lambda i, j: (0, 0))],
        out_specs=[pl.BlockSpec((1, t, PROJ_TN), lambda i, j: (i, 0, j)),
                   pl.BlockSpec((1, t, PROJ_TN), lambda i, j: (i, 0, 0))],
        out_shape=[jax.ShapeDtypeStruct((b, t, D_IN), BF16),
                   jax.ShapeDtypeStruct((b, t, PROJ_TN), F32)],
        scratch_shapes=[pltpu.VMEM((t + 2 * PROJ_SHIFT_PAD, PROJ_TN), F32)],
        compiler_params=pltpu.CompilerParams(dimension_semantics=("arbitrary", "arbitrary"),
                                             vmem_limit_bytes=VMEM_LIMIT),
        name="inproj",
    )(h, w_perm, mu_perm, g_perm, cos, sin)


def _prep_rows(rows, refs, consts):
    (xr_ref, xk_ref, xv_ref, lora_ref, w0_ref, wup_ref, a0_ref, aup_ref, kkg_ref, ka_ref, rk_ref,
     kk_ref, ke_refs, bb_refs, cin_refs, cv_ref) = refs
    ones, tri = consts
    L = CHUNK
    npair = PREP_WIDTH // LANES
    r = xr_ref[0, rows, :].astype(F32)
    k = xk_ref[0, rows, :].astype(F32)
    v = xv_ref[0, rows, :].astype(F32)
    lora = lora_ref[0, rows, :]
    wd = jnp.tanh(lora[:, :LANES])
    ad = lora[:, LANES:]
    k_a = ka_ref[...]
    kkn = k * kkg_ref[...]
    ss = _dot_exact_lhs(kkn * kkn, ones)
    w_raw = w0_ref[...] + _dot_f32(wd, wup_ref[...])
    a_raw = a0_ref[...] + _bdot(ad, aup_ref[...])
    yield
    kk = kkn * lax.rsqrt(jnp.maximum(ss, KK_EPS))
    kk_ref[0, rows, :] = kk.astype(kk_ref.dtype)
    lw_all = -DECAY_SCALE * jax.nn.sigmoid(w_raw)
    a_all = jax.nn.sigmoid(a_raw)
    nchunk = k.shape[0] // L
    row0 = rows.start
    ke_sum = None
    cins = []
    for d in range(2):
        pick = lambda z: jnp.concatenate(
            [z[:, (2 * p + d) * LANES:(2 * p + d + 1) * LANES] for p in range(npair)], axis=1)
        lw, a = pick(lw_all), pick(a_all)
        wide = jnp.concatenate([lw[c * L:(c + 1) * L, :] for c in range(nchunk)], axis=1)
        cins.append(_dot_exact_rhs(tri[d], wide))
        bb_refs[d][0, rows, :] = (kk * a).astype(bb_refs[d].dtype)
        ke = k * (1.0 + (a - 1.0) * k_a)
        ke_refs[d][0, rows, :] = ke.astype(ke_refs[d].dtype)
        ke_sum = ke if ke_sum is None else ke_sum + ke
    coef = _dot_exact_lhs(r * ke_sum * rk_ref[...], ones)
    yield
    for d in range(2):
        for c in range(nchunk):
            cin_refs[d][0, row0 + c * L:row0 + (c + 1) * L, :] = cins[d][:, c * PREP_WIDTH:(c + 1) * PREP_WIDTH]
    cv_ref[0, rows, :] = (coef * v).astype(cv_ref.dtype)


def _prep_kernel(xr_ref, xk_ref, xv_ref, lora_ref, w0_ref, wup_ref, a0_ref, aup_ref,
                 kkg_ref, ka_ref, rk_ref,
                 kk_ref, ke0_ref, ke1_ref, bb0_ref, bb1_ref, cin0_ref, cin1_ref, cv_ref):
    L = CHUNK
    tt_ = lax.broadcasted_iota(jnp.int32, (L, L), 0)
    ss_ = lax.broadcasted_iota(jnp.int32, (L, L), 1)
    tri = (jnp.where(tt_ >= ss_, 1.0, 0.0).astype(BF16), jnp.where(tt_ <= ss_, 1.0, 0.0).astype(BF16))
    consts = (_head_ones(PREP_WIDTH), tri)
    refs = (xr_ref, xk_ref, xv_ref, lora_ref, w0_ref, wup_ref, a0_ref, aup_ref, kkg_ref, ka_ref, rk_ref,
            kk_ref, (ke0_ref, ke1_ref), (bb0_ref, bb1_ref), (cin0_ref, cin1_ref), cv_ref)
    tt = xr_ref.shape[1]
    band = tt // PREP_ROW_BANDS
    _interleave([_prep_rows(slice(i * band, (i + 1) * band), refs, consts) for i in range(PREP_ROW_BANDS)])


def _prep(proj, lora, w0, wup_pad, a0, aup_pad, k_k, k_a, r_k, tt=512):
    b, t, _ = proj.shape
    w = PREP_WIDTH
    col = lambda base: (lambda i, s, p: (i, s, base // w + p))
    vec = lambda i, s, p: (0, p)
    tile = pl.BlockSpec((1, tt, w), lambda i, s, p: (i, s, p))
    sd = lambda dt: jax.ShapeDtypeStruct((b, t, R_WIDTH), dt)
    return pl.pallas_call(
        _prep_kernel,
        grid=(b, t // tt, R_WIDTH // w),
        in_specs=[pl.BlockSpec((1, tt, w), col(COL_XR)),
                  pl.BlockSpec((1, tt, w), col(COL_XK)),
                  pl.BlockSpec((1, tt, w), col(COL_XV)),
                  pl.BlockSpec((1, tt, 2 * LANES), lambda i, s, p: (i, s, 0)),
                  pl.BlockSpec((1, 2 * w), vec),
                  pl.BlockSpec((LANES, 2 * w), vec),
                  pl.BlockSpec((1, 2 * w), vec),
                  pl.BlockSpec((LANES, 2 * w), vec),
                  pl.BlockSpec((1, w), vec),
                  pl.BlockSpec((1, w), vec),
                  pl.BlockSpec((1, w), vec)],
        out_specs=[tile] * 8,
        out_shape=[sd(BF16), sd(BF16), sd(BF16), sd(BF16), sd(BF16), sd(F32), sd(F32), sd(BF16)],
        compiler_params=pltpu.CompilerParams(
            dimension_semantics=("arbitrary", "arbitrary", "arbitrary"),
            vmem_limit_bytes=VMEM_LIMIT),
        name="rwkv_prep",
    )(proj, proj, proj, lora, w0, wup_pad, a0, aup_pad, k_k, k_a, r_k)


def _wkv_consts(reverse):
    L = CHUNK
    t = lax.broadcasted_iota(jnp.int32, (L, 2 * L), 0)
    s = lax.broadcasted_iota(jnp.int32, (L, 2 * L), 1) % L
    if reverse:
        strict, incl = t < s, t <= s
    else:
        strict, incl = t > s, t >= s
    eye2 = jnp.where(t == s, 1.0, 0.0).astype(F32)
    rr = lax.broadcasted_iota(jnp.int32, (LANES, LANES), 0)
    cc = lax.broadcasted_iota(jnp.int32, (LANES, LANES), 1)
    bd = (rr // HEAD_DIM) == (cc // HEAD_DIM)
    eye = rr == cc
    row = lax.broadcasted_iota(jnp.int32, (L, LANES), 0)
    first = row == (L - 1 if reverse else 0)
    return strict, incl, eye2, bd, eye, first


def _wkv_chunk(load, consts, reverse):
    L = CHUNK
    strict, incl, eye2, bd, eye, first = consts

    def bd2(x):
        return jnp.where(bd, jnp.concatenate([x, x], axis=0), 0.0)

    cin, kk, b, ke, v, r, h = load()
    a = -kk
    cex = jnp.where(first, 0.0, pltpu.roll(cin, L - 1 if reverse else 1, axis=0))
    cend = cin[0:1, :] if reverse else cin[L - 1:L, :]
    m = 0.5 * cend
    e_in = jnp.exp(cin - m)
    e_out = jnp.exp(m - cin)
    em = jnp.exp(m)
    at = a * jnp.exp(cex - m)
    rt = r * e_in
    bt = b * e_out
    kt = ke * e_out
    ap, rp, bh, kh = at * em, rt * em, bt * em, kt * em

    ar = jnp.concatenate([at, rt], axis=0)
    g = _bdot_nt(ar, jnp.concatenate([bd2(bt), bd2(kt)], axis=0))
    yield
    a_ab = jnp.where(strict, g[:L, :LANES], 0.0)
    a_ak = jnp.where(strict, g[:L, LANES:], 0.0)
    p_rbk = jnp.concatenate([jnp.where(incl, g[L:, :LANES], 0.0),
                             jnp.where(incl, g[L:, LANES:], 0.0)], axis=1)

    n = a_ab
    p = eye2 + n
    n = _bdot(n, bd2(n))
    vb = bd2(v)
    x = _bdot(a_ak, vb)
    yield
    steps = int(math.log2(L)) - 1
    for i in range(steps):
        nb = bd2(n)
        if i < steps - 1:
            np_ = _bdot(jnp.concatenate([n, p], axis=0), nb)
            n = np_[:L]
            p = p + np_[L:]
        else:
            p = p + _bdot(p, nb)
        yield
    tinv = p

    wu = _bdot(tinv, jnp.concatenate([bd2(ap), bd2(x)], axis=1))
    yield
    w, u0 = wu[:, :LANES], wu[:, LANES:]
    zero = jnp.zeros((LANES, LANES), F32)
    rhs = jnp.concatenate([jnp.concatenate([bd2(w), bd2(u0)], axis=1),
                           jnp.concatenate([zero, vb], axis=1)], axis=0)
    pwo = _bdot(p_rbk, rhs)
    q = rp + pwo[:, :LANES]
    o0 = pwo[:, LANES:]
    zl = jnp.zeros((L, LANES), F32)
    mc = _bdot_tn(jnp.concatenate([bh, kh], axis=0),
                  jnp.concatenate([wu, jnp.concatenate([zl, v], axis=1)], axis=0))
    m_off = jnp.where(bd, mc[:, :LANES], 0.0)
    c = jnp.where(bd, mc[:, LANES:], 0.0)
    decay_col = jnp.sum(jnp.where(eye, jnp.exp(cend), 0.0), axis=1, keepdims=True)
    yield
    qm = _bdot(jnp.concatenate([q, m_off], axis=0), h)
    out = qm[:L] + o0
    h_new = decay_col * h + qm[L:] + c
    return out, h_new


def _interleave(gens):
    results = [None] * len(gens)
    live = list(range(len(gens)))
    while live:
        for i in list(live):
            try:
                next(gens[i])
            except StopIteration as stop:
                results[i] = stop.value
                live.remove(i)
    return results


def _wkv_kernel(rf_ref, vf_ref, kkf_ref, kef_ref, bbf_ref, cinf_ref,
                rb_ref, vb_ref, kkb_ref, keb_ref, bbb_ref, cinb_ref,
                of_ref, ob_ref, h_ref):
    @pl.when(pl.program_id(2) == 0)
    def _():
        h_ref[...] = jnp.zeros_like(h_ref)

    dirs = ((cinf_ref, kkf_ref, bbf_ref, kef_ref, vf_ref, rf_ref, of_ref),
            (cinb_ref, kkb_ref, bbb_ref, keb_ref, vb_ref, rb_ref, ob_ref))
    consts = (_wkv_consts(False), _wkv_consts(True))
    gens, sinks = [], []
    for i in range(WKV_PAIRS_PER_STEP):
        sl = slice(i * LANES, (i + 1) * LANES)
        for d in range(2):
            refs = dirs[d]
            load = functools.partial(
                lambda refs, d, i, sl: tuple(ref[0, :, sl].astype(F32) for ref in refs[:6]) + (h_ref[d, i],),
                refs, d, i, sl)
            gens.append(_wkv_chunk(load, consts[d], reverse=(d == 1)))
            sinks.append((refs[6], d, i, sl))
    for (out, h_new), (o_ref, d, i, sl) in zip(_interleave(gens), sinks):
        o_ref[0, :, sl] = out.astype(o_ref.dtype)
        h_ref[d, i] = h_new


def _wkv(proj, kk, ke0, ke1, bb0, bb1, cin0, cin1):
    b, t, _ = proj.shape
    nc = t // CHUNK
    L = CHUNK
    w = WKV_PAIRS_PER_STEP * LANES
    fwd = lambda base: (lambda i, p, c: (i, c, base // w + p))
    bwd = lambda base: (lambda i, p, c: (i, nc - 1 - c, base // w + p))
    blk = lambda im: pl.BlockSpec((1, L, w), im)
    out_sd = jax.ShapeDtypeStruct((b, t, R_WIDTH), BF16)
    return pl.pallas_call(
        _wkv_kernel,
        grid=(b, PAIRS // WKV_PAIRS_PER_STEP, nc),
        in_specs=[blk(fwd(COL_XR)), blk(fwd(COL_XV)), blk(fwd(0)), blk(fwd(0)), blk(fwd(0)), blk(fwd(0)),
                  blk(bwd(COL_XR)), blk(bwd(COL_XV)), blk(bwd(0)), blk(bwd(0)), blk(bwd(0)), blk(bwd(0))],
        out_specs=[blk(fwd(0)), blk(bwd(0))],
        out_shape=[out_sd, out_sd],
        scratch_shapes=[pltpu.VMEM((2, WKV_PAIRS_PER_STEP, LANES, LANES), F32)],
        compiler_params=pltpu.CompilerParams(
            dimension_semantics=("arbitrary", "arbitrary", "arbitrary"),
            vmem_limit_bytes=VMEM_LIMIT),
        name="wkv",
    )(proj, proj, kk, ke0, bb0, cin0, proj, proj, kk, ke1, bb1, cin1)


def _attn_kernel(q_ref, k_ref, v_ref, o_ref, kh_ref, vt_ref, *, tq):
    t = k_ref.shape[1]
    kb = ATTN_KEY_BLOCK
    heads_per_tile = ATTN_QUERY_TILE // tq
    ntile = A_GROUP // heads_per_tile

    @pl.when(pl.program_id(1) == 0)
    def _():
        r = lax.broadcasted_iota(jnp.int32, (A_KV_WIDTH, HEAD_DIM), 0)
        c = lax.broadcasted_iota(jnp.int32, (A_KV_WIDTH, HEAD_DIM), 1)
        kf = k_ref[0]
        vt = v_ref[0].astype(F32).T.astype(BF16)
        ones = jnp.ones((ATTN_VT_ROWS - HEAD_DIM, t), BF16)
        for hk in range(A_KV_HEADS):
            sel = jnp.where(r == hk * HEAD_DIM + c, 1.0, 0.0).astype(BF16)
            kh_ref[hk] = jnp.dot(kf, sel, preferred_element_type=F32).astype(BF16)
            vt_ref[hk] = jnp.concatenate([vt[hk * HEAD_DIM:(hk + 1) * HEAD_DIM, :], ones], axis=0)

    qt = q_ref[0].astype(F32).T.astype(BF16)

    def q_tile(hk, c):
        h0 = hk * A_GROUP + c * heads_per_tile
        return jnp.concatenate([qt[(h0 + e) * HEAD_DIM:(h0 + e + 1) * HEAD_DIM, :]
                                for e in range(heads_per_tile)], axis=1)

    def scores(hk, c, j):
        return jnp.dot(kh_ref[hk, j * kb:(j + 1) * kb, :], q_tile(hk, c), preferred_element_type=F32)

    tasks = [(hk, c, j) for hk in range(A_KV_HEADS) for j in range(t // kb) for c in range(ntile)]
    state = {}
    done = {}
    ahead = [scores(*task) for task in tasks[:ATTN_SCORE_LOOKAHEAD]]
    for i, (hk, c, j) in enumerate(tasks):
        s = ahead.pop(0)
        if i + ATTN_SCORE_LOOKAHEAD < len(tasks):
            ahead.append(scores(*tasks[i + ATTN_SCORE_LOOKAHEAD]))
        if j == 0:
            state[hk, c] = (jnp.full((1, ATTN_QUERY_TILE), -1e30, F32),
                            jnp.zeros((ATTN_VT_ROWS, ATTN_QUERY_TILE), F32))
        m, acc = state[hk, c]
        m_new = jnp.maximum(m, jnp.max(s, axis=0, keepdims=True))
        p = jnp.exp2(s - m_new).astype(BF16)
        acc = acc * jnp.exp2(m - m_new) + jnp.dot(vt_ref[hk, :, j * kb:(j + 1) * kb], p,
                                                  preferred_element_type=F32)
        state[hk, c] = (m_new, acc)
        if j == t // kb - 1:
            ot = acc[:HEAD_DIM] / acc[HEAD_DIM:HEAD_DIM + 1]
            for e in range(heads_per_tile):
                done[hk * A_GROUP + c * heads_per_tile + e] = ot[:, e * tq:(e + 1) * tq]
    rows = [done[h] for h in range(A_Q_HEADS)]
    o_ref[0] = jnp.concatenate(rows, axis=0).T.astype(o_ref.dtype)


def _attention(proj, tq=128):
    b, t, _ = proj.shape
    kern = functools.partial(_attn_kernel, tq=tq)
    return pl.pallas_call(
        kern,
        grid=(b, t // tq),
        in_specs=[pl.BlockSpec((1, tq, A_WIDTH), lambda i, j: (i, j, COL_Q // A_WIDTH)),
                  pl.BlockSpec((1, t, A_KV_WIDTH), lambda i, j: (i, 0, COL_K // A_KV_WIDTH)),
                  pl.BlockSpec((1, t, A_KV_WIDTH), lambda i, j: (i, 0, COL_V // A_KV_WIDTH))],
        out_specs=pl.BlockSpec((1, tq, A_WIDTH), lambda i, j: (i, j, 0)),
        out_shape=jax.ShapeDtypeStruct((b, t, A_WIDTH), BF16),
        scratch_shapes=[pltpu.VMEM((A_KV_HEADS, t, HEAD_DIM), BF16),
                        pltpu.VMEM((A_KV_HEADS, ATTN_VT_ROWS, t), BF16)],
        compiler_params=pltpu.CompilerParams(dimension_semantics=("arbitrary", "arbitrary"),
                                             vmem_limit_bytes=VMEM_LIMIT),
        name="attention",
    )(proj, proj, proj)


def _post_kernel(x_ref, of_ref, ob_ref, cv_ref, zr_ref, at_ref, za_ref, gr_ref, ga_ref,
                 gnw_ref, gnb_ref, wbr_ref, wba_ref, wout_ref, fg_ref, o_ref):
    za = za_ref[...].astype(F32)
    o_a = at_ref[...].astype(F32) * (za * jax.nn.sigmoid(za))
    p_a = _bdot(o_a, wba_ref[...])
    gw = POST_GN_WIDTH
    ones = _head_ones(gw)
    wkv = of_ref[...].astype(F32) + ob_ref[...].astype(F32)
    groups = [slice(i * gw, (i + 1) * gw) for i in range(R_WIDTH // gw)]
    mus = [_dot_exact_lhs(wkv[:, sl], ones) * (1.0 / HEAD_DIM) for sl in groups]
    ycs = [wkv[:, sl] - mu for sl, mu in zip(groups, mus)]
    vrs = [_dot_exact_lhs(yc * yc, ones) * (1.0 / HEAD_DIM) for yc in ycs]
    gn = jnp.concatenate([yc * lax.rsqrt(var + GN_EPS) for yc, var in zip(ycs, vrs)], axis=1)
    gn = gn * gnw_ref[...] + gnb_ref[...]
    zr = zr_ref[...].astype(F32)
    o_r = (gn + cv_ref[...].astype(F32)) * (zr * jax.nn.sigmoid(zr))
    p_r = _bdot(o_r, wbr_ref[...])
    merged = (jax.nn.sigmoid(gr_ref[...].astype(F32)) * p_r
              + jax.nn.sigmoid(ga_ref[...].astype(F32)) * p_a)
    y = x_ref[...] + _bdot(merged, wout_ref[...])
    yn = y * lax.rsqrt(jnp.mean(y * y, axis=-1, keepdims=True) + NORM_EPS)
    o_ref[...] = yn * fg_ref[...]


def _post(x2d, of, ob, cv, proj2d, attn, gn_w, gn_b, w_br, w_ba, w_out, final_g, tm=256):
    n, d = x2d.shape
    row = lambda w, base: pl.BlockSpec((tm, w), lambda i: (i, base // w))
    const = lambda shape: pl.BlockSpec(shape, lambda i: (0, 0))
    return pl.pallas_call(
        _post_kernel,
        grid=(n // tm,),
        in_specs=[row(d, 0), row(R_WIDTH, 0), row(R_WIDTH, 0), row(R_WIDTH, 0),
                  row(R_WIDTH, COL_ZR), row(A_WIDTH, 0), row(A_WIDTH, COL_ZA),
                  row(d, COL_GR), row(d, COL_GA),
                  const((1, R_WIDTH)), const((1, R_WIDTH)),
                  const((R_WIDTH, d)), const((A_WIDTH, d)), const((d, d)), const((1, d))],
        out_specs=row(d, 0),
        out_shape=jax.ShapeDtypeStruct((n, d), F32),
        compiler_params=pltpu.CompilerParams(dimension_semantics=("arbitrary",),
                                             vmem_limit_bytes=VMEM_LIMIT),
        name="merge_out",
    )(x2d, of, ob, cv, proj2d, attn, proj2d, proj2d, proj2d,
      gn_w.reshape(1, -1), gn_b.reshape(1, -1), w_br, w_ba, w_out, final_g.reshape(1, -1))


def _column_segments():
    shift_w = 3 * R_WIDTH + 4 * LORA
    o_wd, o_ad = 3 * R_WIDTH, 3 * R_WIDTH + 2 * LORA
    o_zr = shift_w
    o_q = o_zr + R_WIDTH
    o_k = o_q + A_WIDTH
    o_v = o_k + A_KV_WIDTH
    o_za = o_v + A_KV_WIDTH
    o_g = o_za + A_WIDTH
    segs = [(0, 3 * R_WIDTH), (o_zr, R_WIDTH), (o_q, A_WIDTH), (o_za, A_WIDTH), (o_g, 4096),
            (o_k, A_KV_WIDTH), (o_v, A_KV_WIDTH), (o_wd, 2 * LORA), (o_ad, 2 * LORA)]
    assert sum(w for _, w in segs) == D_IN
    return segs, shift_w


def _permute_columns(a, segs):
    return jnp.concatenate([a[..., s:s + w] for s, w in segs], axis=-1)


def _pack_lora(up):
    z = jnp.zeros((LORA, PAIRS, LANES), up.dtype)
    top = jnp.concatenate([up[0].reshape(LORA, PAIRS, LANES), z], axis=2)
    bot = jnp.concatenate([z, up[1].reshape(LORA, PAIRS, LANES)], axis=2)
    return jnp.concatenate([top, bot], axis=0).reshape(2 * LORA, 2 * R_WIDTH)


def _pack_dirs(v):
    return jnp.stack([v[0].reshape(PAIRS, LANES), v[1].reshape(PAIRS, LANES)], axis=1).reshape(1, -1)


def _layer(x, norm_g, w_in, shift_mu, w0, w_up, a0, a_up, k_k, k_a, r_k, gn_w, gn_b,
           q_norm_g, k_norm_g, w_branch_rwkv, w_branch_attn, w_out, out_g):
    b, t, d = x.shape
    assert t % CHUNK == 0 and t % GRID_W == 0 and d == 2048 and w_in.shape[1] == D_IN
    segs, shift_w = _column_segments()
    w_perm = _permute_columns(w_in.astype(BF16), segs)
    mu_full = jnp.concatenate([shift_mu, jnp.zeros((2, D_IN - shift_w), F32)], axis=1)
    mu_perm = _permute_columns(mu_full, segs)
    g_perm = jnp.zeros((1, D_IN), F32)
    g_perm = g_perm.at[0, COL_Q:COL_Q + A_WIDTH].set(jnp.tile(q_norm_g * (HEAD_DIM ** -0.5 * LOG2_E), A_Q_HEADS))
    g_perm = g_perm.at[0, COL_K:COL_K + A_KV_WIDTH].set(jnp.tile(k_norm_g, A_KV_HEADS))

    x2d = x.reshape(b * t, d)
    h = _rmsnorm(x2d, norm_g, BF16).reshape(b, t, d)
    proj, lora = _inproj(h, w_perm, mu_perm, g_perm)
    kk, ke0, ke1, bb0, bb1, cin0, cin1, cv = _prep(
        proj, lora, _pack_dirs(w0), _pack_lora(w_up), _pack_dirs(a0), _pack_lora(a_up),
        k_k.reshape(1, -1), k_a.reshape(1, -1), r_k.reshape(1, -1))
    of, ob = _wkv(proj, kk, ke0, ke1, bb0, bb1, cin0, cin1)
    attn = _attention(proj)
    n = b * t
    out = _post(x2d, of.reshape(n, -1), ob.reshape(n, -1), cv.reshape(n, -1), proj.reshape(n, D_IN),
                attn.reshape(n, -1), gn_w, gn_b, w_branch_rwkv.astype(BF16), w_branch_attn.astype(BF16),
                w_out.astype(BF16), out_g)
    return out.reshape(b, t, d)


def kernel(x, norm_g, w_in, shift_mu, w0, w_up, a0, a_up, k_k, k_a, r_k, gn_w, gn_b, q_norm_g, k_norm_g,
           w_branch_rwkv, w_branch_attn, w_out, final_norm_g):
    assert norm_g.shape[0] == 1, "single-layer block"
    return _layer(x, norm_g[0], w_in[0], shift_mu[0], w0[0], w_up[0], a0[0], a_up[0], k_k[0], k_a[0],
                  r_k[0], gn_w[0], gn_b[0], q_norm_g[0], k_norm_g[0], w_branch_rwkv[0],
                  w_branch_attn[0], w_out[0], final_norm_g)
```

```python
import functools
import math

import jax
import jax.numpy as jnp
import numpy as np
from jax import lax
from jax.experimental import pallas as pl
from jax.experimental.pallas import tpu as pltpu

F32 = jnp.float32
BF16 = jnp.bfloat16

HEAD_DIM = 64
R_HEADS = 16
R_WIDTH = R_HEADS * HEAD_DIM
LORA = 64
A_Q_HEADS = 16
A_KV_HEADS = 4
A_GROUP = A_Q_HEADS // A_KV_HEADS
A_WIDTH = A_Q_HEADS * HEAD_DIM
A_KV_WIDTH = A_KV_HEADS * HEAD_DIM
GRID_W = 64
ROPE_THETA = 10000.0
NORM_EPS = 1e-6
GN_EPS = 64e-5
KK_EPS = 1e-24
DECAY_SCALE = math.exp(-0.5)

LANES = 128
VMEM_LIMIT = 56 * 1024 * 1024

CHUNK = 64
PAIRS = R_WIDTH // LANES
POST_GN_WIDTH = 256
PREP_WIDTH = 256
PREP_ROW_BANDS = 2
WKV_PAIRS_PER_STEP = 8

COL_XR, COL_XK, COL_XV, COL_ZR, COL_Q, COL_ZA = 0, 1024, 2048, 3072, 4096, 5120
COL_GR, COL_GA, COL_K, COL_V, COL_WD, COL_AD = 6144, 8192, 10240, 10496, 10752, 10880
D_IN = 11008
PROJ_TN = 512
PROJ_HALF = 256
D_PROJ = 11264
PROJ_ROW_CHUNK = 512
PROJ_SHIFT_PAD = 8
ATTN_VT_ROWS = HEAD_DIM + 16
ATTN_KEY_BLOCK = 128
ATTN_QUERY_TILE = 256
ATTN_SCORE_LOOKAHEAD = 8
LOG2_E = math.log2(math.e)


def _bdot(a, b):
    return jnp.dot(a.astype(BF16), b.astype(BF16), preferred_element_type=F32)


def _bdot_nt(a, b):
    return lax.dot_general(a.astype(BF16), b.astype(BF16), (((1,), (1,)), ((), ())),
                           preferred_element_type=F32)


def _bdot_tn(a, b):
    return lax.dot_general(a.astype(BF16), b.astype(BF16), (((0,), (0,)), ((), ())),
                           preferred_element_type=F32)


def _split2(x):
    hi = x.astype(BF16)
    lo = (x - hi.astype(F32)).astype(BF16)
    return hi, lo


def _dot_exact_rhs(a_bf16, x):
    hi, lo = _split2(x)
    d = lambda y: jnp.dot(a_bf16, y, preferred_element_type=F32)
    return d(hi) + d(lo)


def _dot_exact_lhs(x, b_bf16):
    hi, lo = _split2(x)
    d = lambda y: jnp.dot(y, b_bf16, preferred_element_type=F32)
    return d(hi) + d(lo)


def _dot_f32(x, w):
    xh = x.astype(BF16)
    xl = (x - xh.astype(F32)).astype(BF16)
    wh = w.astype(BF16)
    wl = (w - wh.astype(F32)).astype(BF16)
    d = lambda p, q: jnp.dot(p, q, preferred_element_type=F32)
    return d(xh, wh) + d(xl, wh) + d(xh, wl)


def _head_ones(width=LANES):
    r = lax.broadcasted_iota(jnp.int32, (width, width), 0) // HEAD_DIM
    c = lax.broadcasted_iota(jnp.int32, (width, width), 1) // HEAD_DIM
    return jnp.where(r == c, 1.0, 0.0).astype(BF16)


def _rmsnorm_kernel(x_ref, g_ref, o_ref):
    x = x_ref[...]
    y = x * lax.rsqrt(jnp.mean(x * x, axis=-1, keepdims=True) + NORM_EPS)
    o_ref[...] = (y * g_ref[...]).astype(o_ref.dtype)


def _rmsnorm(x2d, g, out_dtype, tm=512):
    n, d = x2d.shape
    return pl.pallas_call(
        _rmsnorm_kernel,
        grid=(n // tm,),
        in_specs=[pl.BlockSpec((tm, d), lambda i: (i, 0)),
                  pl.BlockSpec((1, d), lambda i: (0, 0))],
        out_specs=pl.BlockSpec((tm, d), lambda i: (i, 0)),
        out_shape=jax.ShapeDtypeStruct((n, d), out_dtype),
        compiler_params=pltpu.CompilerParams(dimension_semantics=("arbitrary",),
                                             vmem_limit_bytes=VMEM_LIMIT),
        name="rmsnorm",
    )(x2d, g.reshape(1, d))


def _rope_tables(t):
    rows = t // GRID_W
    row = np.repeat(np.arange(rows, dtype=np.float32), GRID_W)
    colv = np.tile(np.arange(GRID_W, dtype=np.float32), rows)
    axis_dim = HEAD_DIM // 2
    freqs = jnp.asarray(ROPE_THETA, F32) ** (-jnp.arange(0, axis_dim, 2, dtype=F32) / axis_dim)
    ang = jnp.concatenate([jnp.asarray(row)[:, None] * freqs, jnp.asarray(colv)[:, None] * freqs], axis=-1)
    reps = PROJ_HALF // HEAD_DIM
    cos = jnp.tile(jnp.repeat(jnp.cos(ang), 2, axis=-1), (1, reps))
    sin = jnp.tile(jnp.repeat(jnp.sin(ang), 2, axis=-1), (1, reps))
    return cos, sin


def _inproj_tile_kinds():
    def kind(col):
        if col < 3 * R_WIDTH or COL_WD <= col < COL_WD + PROJ_HALF:
            return "shift"
        if COL_Q <= col < COL_Q + A_WIDTH or COL_K <= col < COL_K + A_KV_WIDTH:
            return "rope"
        return "plain"
    return [(kind(j * PROJ_TN), kind(j * PROJ_TN + PROJ_HALF)) for j in range(D_PROJ // PROJ_TN)]


def _inproj_kernel(h_ref, w_ref, mu_ref, g_ref, cos_ref, sin_ref, o_ref, lora_ref, acc_ref, *, kinds):
    j = pl.program_id(1)
    t = h_ref.shape[1]
    rc = PROJ_ROW_CHUNK
    nchunk = t // rc
    pad = PROJ_SHIFT_PAD
    hw = PROJ_HALF

    def mm(c):
        return jnp.dot(h_ref[0, c * rc:(c + 1) * rc, :], w_ref[...], preferred_element_type=F32)

    def emit_plain(c, cols, acc):
        o_ref[0, c * rc:(c + 1) * rc, cols] = acc.astype(o_ref.dtype)

    def emit_rope(c, cols, acc):
        rows = slice(c * rc, (c + 1) * rc)
        ss = _dot_exact_lhs(acc * acc, _head_ones(hw)) * (1.0 / HEAD_DIM)
        y = acc * lax.rsqrt(ss + NORM_EPS) * g_ref[:, cols]
        lane = lax.broadcasted_iota(jnp.int32, (rc, hw), 1)
        nxt = pltpu.roll(y, hw - 1, axis=1)
        prv = pltpu.roll(y, 1, axis=1)
        o_ref[0, rows, cols] = (y * cos_ref[rows, :] + jnp.where((lane % 2) == 0, -nxt, prv)
                                * sin_ref[rows, :]).astype(o_ref.dtype)

    def emit_shift(c, cols, is_lora):
        lo = pad + c * rc
        mu = mu_ref[:, cols]
        cur = acc_ref[lo:lo + rc, cols]
        prev = acc_ref[lo - 1:lo - 1 + rc, cols]
        nxt = acc_ref[lo + 1:lo + 1 + rc, cols]
        y = cur + mu[0:1] * (prev - cur) + mu[1:2] * (nxt - cur)
        o_ref[0, c * rc:(c + 1) * rc, cols] = y.astype(o_ref.dtype)
        if is_lora:
            lora_ref[0, c * rc:(c + 1) * rc, :] = y

    def run(pair, is_lora):
        halves = [(slice(i * hw, (i + 1) * hw), k) for i, k in enumerate(pair)]
        if "shift" in pair:
            acc_ref[0:pad, :] = jnp.zeros((pad, PROJ_TN), F32)
            acc_ref[pad + t:2 * pad + t, :] = jnp.zeros((pad, PROJ_TN), F32)
        ahead = mm(0)
        for c in range(nchunk):
            acc, ahead = ahead, (mm(c + 1) if c + 1 < nchunk else None)
            for cols, k in halves:
                if k == "shift":
                    acc_ref[pad + c * rc:pad + (c + 1) * rc, cols] = acc[:, cols]
                    if c > 0:
                        emit_shift(c - 1, cols, is_lora)
                elif k == "rope":
                    emit_rope(c, cols, acc[:, cols])
                else:
                    emit_plain(c, cols, acc[:, cols])
        for cols, k in halves:
            if k == "shift":
                emit_shift(nchunk - 1, cols, is_lora)

    lora_tile = COL_WD // PROJ_TN
    for pair in sorted(set(kinds)):
        tiles = [i for i, k in enumerate(kinds) if k == pair and i != lora_tile]
        if tiles:
            cond = functools.reduce(jnp.logical_or, [j == i for i in tiles])
            pl.when(cond)(functools.partial(run, pair, False))
    pl.when(j == lora_tile)(functools.partial(run, kinds[lora_tile], True))


def _inproj(h, w_perm, mu_perm, g_perm):
    b, t, d = h.shape
    nj = D_PROJ // PROJ_TN
    cos, sin = _rope_tables(t)
    kern = functools.partial(_inproj_kernel, kinds=_inproj_tile_kinds())
    return pl.pallas_call(
        kern,
        grid=(b, nj),
        in_specs=[pl.BlockSpec((1, t, d), lambda i, j: (i, 0, 0)),
                  pl.BlockSpec((d, PROJ_TN), lambda i, j: (0, j)),
                  pl.BlockSpec((2, PROJ_TN), lambda i, j: (0, j)),
                  pl.BlockSpec((1, PROJ_TN), lambda i, j: (0, j)),
                  pl.BlockSpec((t, PROJ_HALF), lambda i, j: (0, 0)),
                  pl.BlockSpec((t, PROJ_HALF), lambda i, j: (0, 0))],
        out_specs=[pl.BlockSpec((1, t, PROJ_TN), lambda i, j: (i, 0, j)),
                   pl.BlockSpec((1, t, PROJ_HALF), lambda i, j: (i, 0, 0))],
        out_shape=[jax.ShapeDtypeStruct((b, t, D_PROJ), BF16),
                   jax.ShapeDtypeStruct((b, t, PROJ_HALF), F32)],
        scratch_shapes=[pltpu.VMEM((t + 2 * PROJ_SHIFT_PAD, PROJ_TN), F32)],
        compiler_params=pltpu.CompilerParams(dimension_semantics=("arbitrary", "arbitrary"),
                                             vmem_limit_bytes=VMEM_LIMIT),
        name="inproj",
    )(h, w_perm, mu_perm, g_perm, cos, sin)


def _prep_rows(rows, refs, consts):
    (xr_ref, xk_ref, xv_ref, lora_ref, w0_ref, wup_ref, a0_ref, aup_ref, kkg_ref, ka_ref, rk_ref,
     kk_ref, ke_refs, bb_refs, cin_refs, cv_ref) = refs
    ones, tri = consts
    L = CHUNK
    npair = PREP_WIDTH // LANES
    r = xr_ref[0, rows, :].astype(F32)
    k = xk_ref[0, rows, :].astype(F32)
    v = xv_ref[0, rows, :].astype(F32)
    lora = lora_ref[0, rows, :]
    wd = jnp.tanh(lora[:, :LANES])
    ad = lora[:, LANES:]
    k_a = ka_ref[...]
    kkn = k * kkg_ref[...]
    ss = _dot_exact_lhs(kkn * kkn, ones)
    w_raw = w0_ref[...] + _dot_f32(wd, wup_ref[...])
    a_raw = a0_ref[...] + _bdot(ad, aup_ref[...])
    yield
    kk = kkn * lax.rsqrt(jnp.maximum(ss, KK_EPS))
    kk_ref[0, rows, :] = kk.astype(kk_ref.dtype)
    lw_all = -DECAY_SCALE * jax.nn.sigmoid(w_raw)
    a_all = jax.nn.sigmoid(a_raw)
    nchunk = k.shape[0] // L
    row0 = rows.start
    ke_sum = None
    cins = []
    for d in range(2):
        pick = lambda z: jnp.concatenate(
            [z[:, (2 * p + d) * LANES:(2 * p + d + 1) * LANES] for p in range(npair)], axis=1)
        lw, a = pick(lw_all), pick(a_all)
        wide = jnp.concatenate([lw[c * L:(c + 1) * L, :] for c in range(nchunk)], axis=1)
        cins.append(_dot_exact_rhs(tri[d], wide))
        bb_refs[d][0, rows, :] = (kk * a).astype(bb_refs[d].dtype)
        ke = k * (1.0 + (a - 1.0) * k_a)
        ke_refs[d][0, rows, :] = ke.astype(ke_refs[d].dtype)
        ke_sum = ke if ke_sum is None else ke_sum + ke
    coef = _dot_exact_lhs(r * ke_sum * rk_ref[...], ones)
    yield
    for d in range(2):
        for c in range(nchunk):
            cin_refs[d][0, row0 + c * L:row0 + (c + 1) * L, :] = cins[d][:, c * PREP_WIDTH:(c + 1) * PREP_WIDTH]
    cv_ref[0, rows, :] = (coef * v).astype(cv_ref.dtype)


def _prep_kernel(xr_ref, xk_ref, xv_ref, lora_ref, w0_ref, wup_ref, a0_ref, aup_ref,
                 kkg_ref, ka_ref, rk_ref,
                 kk_ref, ke0_ref, ke1_ref, bb0_ref, bb1_ref, cin0_ref, cin1_ref, cv_ref):
    L = CHUNK
    tt_ = lax.broadcasted_iota(jnp.int32, (L, L), 0)
    ss_ = lax.broadcasted_iota(jnp.int32, (L, L), 1)
    tri = (jnp.where(tt_ >= ss_, 1.0, 0.0).astype(BF16), jnp.where(tt_ <= ss_, 1.0, 0.0).astype(BF16))
    consts = (_head_ones(PREP_WIDTH), tri)
    refs = (xr_ref, xk_ref, xv_ref, lora_ref, w0_ref, wup_ref, a0_ref, aup_ref, kkg_ref, ka_ref, rk_ref,
            kk_ref, (ke0_ref, ke1_ref), (bb0_ref, bb1_ref), (cin0_ref, cin1_ref), cv_ref)
    tt = xr_ref.shape[1]
    band = tt // PREP_ROW_BANDS
    _interleave([_prep_rows(slice(i * band, (i + 1) * band), refs, consts) for i in range(PREP_ROW_BANDS)])


def _prep(proj, lora, w0, wup_pad, a0, aup_pad, k_k, k_a, r_k, tt=512):
    b, t, _ = proj.shape
    w = PREP_WIDTH
    col = lambda base: (lambda i, s, p: (i, s, base // w + p))
    vec = lambda i, s, p: (0, p)
    tile = pl.BlockSpec((1, tt, w), lambda i, s, p: (i, s, p))
    sd = lambda dt: jax.ShapeDtypeStruct((b, t, R_WIDTH), dt)
    return pl.pallas_call(
        _prep_kernel,
        grid=(b, t // tt, R_WIDTH // w),
        in_specs=[pl.BlockSpec((1, tt, w), col(COL_XR)),
                  pl.BlockSpec((1, tt, w), col(COL_XK)),
                  pl.BlockSpec((1, tt, w), col(COL_XV)),
                  pl.BlockSpec((1, tt, 2 * LANES), lambda i, s, p: (i, s, 0)),
                  pl.BlockSpec((1, 2 * w), vec),
                  pl.BlockSpec((LANES, 2 * w), vec),
                  pl.BlockSpec((1, 2 * w), vec),
                  pl.BlockSpec((LANES, 2 * w), vec),
                  pl.BlockSpec((1, w), vec),
                  pl.BlockSpec((1, w), vec),
                  pl.BlockSpec((1, w), vec)],
        out_specs=[tile] * 8,
        out_shape=[sd(BF16), sd(BF16), sd(BF16), sd(BF16), sd(BF16), sd(F32), sd(F32), sd(BF16)],
        compiler_params=pltpu.CompilerParams(
            dimension_semantics=("arbitrary", "arbitrary", "arbitrary"),
            vmem_limit_bytes=VMEM_LIMIT),
        name="rwkv_prep",
    )(proj, proj, proj, lora, w0, wup_pad, a0, aup_pad, k_k, k_a, r_k)


def _wkv_consts(reverse):
    L = CHUNK
    t = lax.broadcasted_iota(jnp.int32, (L, 2 * L), 0)
    s = lax.broadcasted_iota(jnp.int32, (L, 2 * L), 1) % L
    if reverse:
        strict, incl = t < s, t <= s
    else:
        strict, incl = t > s, t >= s
    eye2 = jnp.where(t == s, 1.0, 0.0).astype(F32)
    rr = lax.broadcasted_iota(jnp.int32, (LANES, LANES), 0)
    cc = lax.broadcasted_iota(jnp.int32, (LANES, LANES), 1)
    bd = (rr // HEAD_DIM) == (cc // HEAD_DIM)
    eye = rr == cc
    row = lax.broadcasted_iota(jnp.int32, (L, LANES), 0)
    first = row == (L - 1 if reverse else 0)
    return strict, incl, eye2, bd, eye, first


def _wkv_chunk(load, consts, reverse):
    L = CHUNK
    strict, incl, eye2, bd, eye, first = consts

    def bd2(x):
        return jnp.where(bd, jnp.concatenate([x, x], axis=0), 0.0)

    cin, kk, b, ke, v, r, h = load()
    a = -kk
    cex = jnp.where(first, 0.0, pltpu.roll(cin, L - 1 if reverse else 1, axis=0))
    cend = cin[0:1, :] if reverse else cin[L - 1:L, :]
    m = 0.5 * cend
    e_in = jnp.exp(cin - m)
    e_out = jnp.exp(m - cin)
    em = jnp.exp(m)
    at = a * jnp.exp(cex - m)
    rt = r * e_in
    bt = b * e_out
    kt = ke * e_out
    ap, rp, bh, kh = at * em, rt * em, bt * em, kt * em

    ar = jnp.concatenate([at, rt], axis=0)
    g = _bdot_nt(ar, jnp.concatenate([bd2(bt), bd2(kt)], axis=0))
    yield
    a_ab = jnp.where(strict, g[:L, :LANES], 0.0)
    a_ak = jnp.where(strict, g[:L, LANES:], 0.0)
    p_rbk = jnp.concatenate([jnp.where(incl, g[L:, :LANES], 0.0),
                             jnp.where(incl, g[L:, LANES:], 0.0)], axis=1)

    n = a_ab
    p = eye2 + n
    n = _bdot(n, bd2(n))
    vb = bd2(v)
    x = _bdot(a_ak, vb)
    yield
    steps = int(math.log2(L)) - 1
    for i in range(steps):
        nb = bd2(n)
        if i < steps - 1:
            np_ = _bdot(jnp.concatenate([n, p], axis=0), nb)
            n = np_[:L]
            p = p + np_[L:]
        else:
            p = p + _bdot(p, nb)
        yield
    tinv = p

    wu = _bdot(tinv, jnp.concatenate([bd2(ap), bd2(x)], axis=1))
    yield
    w, u0 = wu[:, :LANES], wu[:, LANES:]
    zero = jnp.zeros((LANES, LANES), F32)
    rhs = jnp.concatenate([jnp.concatenate([bd2(w), bd2(u0)], axis=1),
                           jnp.concatenate([zero, vb], axis=1)], axis=0)
    pwo = _bdot(p_rbk, rhs)
    q = rp + pwo[:, :LANES]
    o0 = pwo[:, LANES:]
    zl = jnp.zeros((L, LANES), F32)
    mc = _bdot_tn(jnp.concatenate([bh, kh], axis=0),
                  jnp.concatenate([wu, jnp.concatenate([zl, v], axis=1)], axis=0))
    m_off = jnp.where(bd, mc[:, :LANES], 0.0)
    c = jnp.where(bd, mc[:, LANES:], 0.0)
    decay_col = jnp.sum(jnp.where(eye, jnp.exp(cend), 0.0), axis=1, keepdims=True)
    yield
    qm = _bdot(jnp.concatenate([q, m_off], axis=0), h)
    out = qm[:L] + o0
    h_new = decay_col * h + qm[L:] + c
    return out, h_new


def _interleave(gens):
    results = [None] * len(gens)
    live = list(range(len(gens)))
    while live:
        for i in list(live):
            try:
                next(gens[i])
            except StopIteration as stop:
                results[i] = stop.value
                live.remove(i)
    return results


def _wkv_kernel(rf_ref, vf_ref, kkf_ref, kef_ref, bbf_ref, cinf_ref,
                rb_ref, vb_ref, kkb_ref, keb_ref, bbb_ref, cinb_ref,
                of_ref, ob_ref, h_ref):
    @pl.when(pl.program_id(2) == 0)
    def _():
        h_ref[...] = jnp.zeros_like(h_ref)

    dirs = ((cinf_ref, kkf_ref, bbf_ref, kef_ref, vf_ref, rf_ref, of_ref),
            (cinb_ref, kkb_ref, bbb_ref, keb_ref, vb_ref, rb_ref, ob_ref))
    consts = (_wkv_consts(False), _wkv_consts(True))
    gens, sinks = [], []
    for i in range(WKV_PAIRS_PER_STEP):
        sl = slice(i * LANES, (i + 1) * LANES)
        for d in range(2):
            refs = dirs[d]
            load = functools.partial(
                lambda refs, d, i, sl: tuple(ref[0, :, sl].astype(F32) for ref in refs[:6]) + (h_ref[d, i],),
                refs, d, i, sl)
            gens.append(_wkv_chunk(load, consts[d], reverse=(d == 1)))
            sinks.append((refs[6], d, i, sl))
    for (out, h_new), (o_ref, d, i, sl) in zip(_interleave(gens), sinks):
        o_ref[0, :, sl] = out.astype(o_ref.dtype)
        h_ref[d, i] = h_new


def _wkv(proj, kk, ke0, ke1, bb0, bb1, cin0, cin1):
    b, t, _ = proj.shape
    nc = t // CHUNK
    L = CHUNK
    w = WKV_PAIRS_PER_STEP * LANES
    fwd = lambda base: (lambda i, p, c: (i, c, base // w + p))
    bwd = lambda base: (lambda i, p, c: (i, nc - 1 - c, base // w + p))
    blk = lambda im: pl.BlockSpec((1, L, w), im)
    out_sd = jax.ShapeDtypeStruct((b, t, R_WIDTH), BF16)
    return pl.pallas_call(
        _wkv_kernel,
        grid=(b, PAIRS // WKV_PAIRS_PER_STEP, nc),
        in_specs=[blk(fwd(COL_XR)), blk(fwd(COL_XV)), blk(fwd(0)), blk(fwd(0)), blk(fwd(0)), blk(fwd(0)),
                  blk(bwd(COL_XR)), blk(bwd(COL_XV)), blk(bwd(0)), blk(bwd(0)), blk(bwd(0)), blk(bwd(0))],
        out_specs=[blk(fwd(0)), blk(bwd(0))],
        out_shape=[out_sd, out_sd],
        scratch_shapes=[pltpu.VMEM((2, WKV_PAIRS_PER_STEP, LANES, LANES), F32)],
        compiler_params=pltpu.CompilerParams(
            dimension_semantics=("arbitrary", "arbitrary", "arbitrary"),
            vmem_limit_bytes=VMEM_LIMIT),
        name="wkv",
    )(proj, proj, kk, ke0, bb0, cin0, proj, proj, kk, ke1, bb1, cin1)


def _attn_kernel(q_ref, k_ref, v_ref, o_ref, kh_ref, vt_ref, *, tq):
    t = k_ref.shape[1]
    kb = ATTN_KEY_BLOCK
    heads_per_tile = ATTN_QUERY_TILE // tq
    ntile = A_GROUP // heads_per_tile

    @pl.when(pl.program_id(1) == 0)
    def _():
        r = lax.broadcasted_iota(jnp.int32, (A_KV_WIDTH, HEAD_DIM), 0)
        c = lax.broadcasted_iota(jnp.int32, (A_KV_WIDTH, HEAD_DIM), 1)
        kf = k_ref[0]
        vt = v_ref[0].astype(F32).T.astype(BF16)
        ones = jnp.ones((ATTN_VT_ROWS - HEAD_DIM, t), BF16)
        for hk in range(A_KV_HEADS):
            sel = jnp.where(r == hk * HEAD_DIM + c, 1.0, 0.0).astype(BF16)
            kh_ref[hk] = jnp.dot(kf, sel, preferred_element_type=F32).astype(BF16)
            vt_ref[hk] = jnp.concatenate([vt[hk * HEAD_DIM:(hk + 1) * HEAD_DIM, :], ones], axis=0)

    qt = q_ref[0].astype(F32).T.astype(BF16)

    def q_tile(hk, c):
        h0 = hk * A_GROUP + c * heads_per_tile
        return jnp.concatenate([qt[(h0 + e) * HEAD_DIM:(h0 + e + 1) * HEAD_DIM, :]
                                for e in range(heads_per_tile)], axis=1)

    def scores(hk, c, j):
        return jnp.dot(kh_ref[hk, j * kb:(j + 1) * kb, :], q_tile(hk, c), preferred_element_type=F32)

    tasks = [(hk, c, j) for hk in range(A_KV_HEADS) for j in range(t // kb) for c in range(ntile)]
    state = {}
    done = {}
    ahead = [scores(*task) for task in tasks[:ATTN_SCORE_LOOKAHEAD]]
    for i, (hk, c, j) in enumerate(tasks):
        s = ahead.pop(0)
        if i + ATTN_SCORE_LOOKAHEAD < len(tasks):
            ahead.append(scores(*tasks[i + ATTN_SCORE_LOOKAHEAD]))
        if j == 0:
            state[hk, c] = (jnp.full((1, ATTN_QUERY_TILE), -1e30, F32),
                            jnp.zeros((ATTN_VT_ROWS, ATTN_QUERY_TILE), F32))
        m, acc = state[hk, c]
        m_new = jnp.maximum(m, jnp.max(s, axis=0, keepdims=True))
        p = jnp.exp2(s - m_new).astype(BF16)
        acc = acc * jnp.exp2(m - m_new) + jnp.dot(vt_ref[hk, :, j * kb:(j + 1) * kb], p,
                                                  preferred_element_type=F32)
        state[hk, c] = (m_new, acc)
        if j == t // kb - 1:
            ot = acc[:HEAD_DIM] / acc[HEAD_DIM:HEAD_DIM + 1]
            for e in range(heads_per_tile):
                done[hk * A_GROUP + c * heads_per_tile + e] = ot[:, e * tq:(e + 1) * tq]
    rows = [done[h] for h in range(A_Q_HEADS)]
    o_ref[0] = jnp.concatenate(rows, axis=0).T.astype(o_ref.dtype)


def _attention(proj, tq=128):
    b, t, _ = proj.shape
    kern = functools.partial(_attn_kernel, tq=tq)
    return pl.pallas_call(
        kern,
        grid=(b, t // tq),
        in_specs=[pl.BlockSpec((1, tq, A_WIDTH), lambda i, j: (i, j, COL_Q // A_WIDTH)),
                  pl.BlockSpec((1, t, A_KV_WIDTH), lambda i, j: (i, 0, COL_K // A_KV_WIDTH)),
                  pl.BlockSpec((1, t, A_KV_WIDTH), lambda i, j: (i, 0, COL_V // A_KV_WIDTH))],
        out_specs=pl.BlockSpec((1, tq, A_WIDTH), lambda i, j: (i, j, 0)),
        out_shape=jax.ShapeDtypeStruct((b, t, A_WIDTH), BF16),
        scratch_shapes=[pltpu.VMEM((A_KV_HEADS, t, HEAD_DIM), BF16),
                        pltpu.VMEM((A_KV_HEADS, ATTN_VT_ROWS, t), BF16)],
        compiler_params=pltpu.CompilerParams(dimension_semantics=("arbitrary", "arbitrary"),
                                             vmem_limit_bytes=VMEM_LIMIT),
        name="attention",
    )(proj, proj, proj)


def _post_kernel(x_ref, of_ref, ob_ref, cv_ref, zr_ref, at_ref, za_ref, gr_ref, ga_ref,
                 gnw_ref, gnb_ref, wbr_ref, wba_ref, wout_ref, fg_ref, o_ref):
    za = za_ref[...].astype(F32)
    o_a = at_ref[...].astype(F32) * (za * jax.nn.sigmoid(za))
    p_a = _bdot(o_a, wba_ref[...])
    gw = POST_GN_WIDTH
    ones = _head_ones(gw)
    wkv = of_ref[...].astype(F32) + ob_ref[...].astype(F32)
    groups = [slice(i * gw, (i + 1) * gw) for i in range(R_WIDTH // gw)]
    mus = [_dot_exact_lhs(wkv[:, sl], ones) * (1.0 / HEAD_DIM) for sl in groups]
    ycs = [wkv[:, sl] - mu for sl, mu in zip(groups, mus)]
    vrs = [_dot_exact_lhs(yc * yc, ones) * (1.0 / HEAD_DIM) for yc in ycs]
    gn = jnp.concatenate([yc * lax.rsqrt(var + GN_EPS) for yc, var in zip(ycs, vrs)], axis=1)
    gn = gn * gnw_ref[...] + gnb_ref[...]
    zr = zr_ref[...].astype(F32)
    o_r = (gn + cv_ref[...].astype(F32)) * (zr * jax.nn.sigmoid(zr))
    p_r = _bdot(o_r, wbr_ref[...])
    merged = (jax.nn.sigmoid(gr_ref[...].astype(F32)) * p_r
              + jax.nn.sigmoid(ga_ref[...].astype(F32)) * p_a)
    y = x_ref[...] + _bdot(merged, wout_ref[...])
    yn = y * lax.rsqrt(jnp.mean(y * y, axis=-1, keepdims=True) + NORM_EPS)
    o_ref[...] = yn * fg_ref[...]


def _post(x2d, of, ob, cv, proj2d, attn, gn_w, gn_b, w_br, w_ba, w_out, final_g, tm=256):
    n, d = x2d.shape
    row = lambda w, base: pl.BlockSpec((tm, w), lambda i: (i, base // w))
    const = lambda shape: pl.BlockSpec(shape, lambda i: (0, 0))
    return pl.pallas_call(
        _post_kernel,
        grid=(n // tm,),
        in_specs=[row(d, 0), row(R_WIDTH, 0), row(R_WIDTH, 0), row(R_WIDTH, 0),
                  row(R_WIDTH, COL_ZR), row(A_WIDTH, 0), row(A_WIDTH, COL_ZA),
                  row(d, COL_GR), row(d, COL_GA),
                  const((1, R_WIDTH)), const((1, R_WIDTH)),
                  const((R_WIDTH, d)), const((A_WIDTH, d)), const((d, d)), const((1, d))],
        out_specs=row(d, 0),
        out_shape=jax.ShapeDtypeStruct((n, d), F32),
        compiler_params=pltpu.CompilerParams(dimension_semantics=("arbitrary",),
                                             vmem_limit_bytes=VMEM_LIMIT),
        name="merge_out",
    )(x2d, of, ob, cv, proj2d, attn, proj2d, proj2d, proj2d,
      gn_w.reshape(1, -1), gn_b.reshape(1, -1), w_br, w_ba, w_out, final_g.reshape(1, -1))


def _column_segments():
    shift_w = 3 * R_WIDTH + 4 * LORA
    o_wd, o_ad = 3 * R_WIDTH, 3 * R_WIDTH + 2 * LORA
    o_zr = shift_w
    o_q = o_zr + R_WIDTH
    o_k = o_q + A_WIDTH
    o_v = o_k + A_KV_WIDTH
    o_za = o_v + A_KV_WIDTH
    o_g = o_za + A_WIDTH
    segs = [(0, 3 * R_WIDTH), (o_zr, R_WIDTH), (o_q, A_WIDTH), (o_za, A_WIDTH), (o_g, 4096),
            (o_k, A_KV_WIDTH), (o_v, A_KV_WIDTH), (o_wd, 2 * LORA), (o_ad, 2 * LORA)]
    assert sum(w for _, w in segs) == D_IN
    return segs, shift_w


def _permute_columns(a, segs):
    return jnp.concatenate([a[..., s:s + w] for s, w in segs], axis=-1)


def _pack_lora(up):
    z = jnp.zeros((LORA, PAIRS, LANES), up.dtype)
    top = jnp.concatenate([up[0].reshape(LORA, PAIRS, LANES), z], axis=2)
    bot = jnp.concatenate([z, up[1].reshape(LORA, PAIRS, LANES)], axis=2)
    return jnp.concatenate([top, bot], axis=0).reshape(2 * LORA, 2 * R_WIDTH)


def _pack_dirs(v):
    return jnp.stack([v[0].reshape(PAIRS, LANES), v[1].reshape(PAIRS, LANES)], axis=1).reshape(1, -1)


def _layer(x, norm_g, w_in, shift_mu, w0, w_up, a0, a_up, k_k, k_a, r_k, gn_w, gn_b,
           q_norm_g, k_norm_g, w_branch_rwkv, w_branch_attn, w_out, out_g):
    b, t, d = x.shape
    assert t % CHUNK == 0 and t % GRID_W == 0 and d == 2048 and w_in.shape[1] == D_IN
    segs, shift_w = _column_segments()
    pad_cols = lambda a: jnp.pad(a, ((0, 0), (0, D_PROJ - D_IN)))
    w_perm = pad_cols(_permute_columns(w_in.astype(BF16), segs))
    mu_full = jnp.concatenate([shift_mu, jnp.zeros((2, D_IN - shift_w), F32)], axis=1)
    mu_perm = pad_cols(_permute_columns(mu_full, segs))
    g_perm = jnp.zeros((1, D_PROJ), F32)
    g_perm = g_perm.at[0, COL_Q:COL_Q + A_WIDTH].set(jnp.tile(q_norm_g * (HEAD_DIM ** -0.5 * LOG2_E), A_Q_HEADS))
    g_perm = g_perm.at[0, COL_K:COL_K + A_KV_WIDTH].set(jnp.tile(k_norm_g, A_KV_HEADS))

    x2d = x.reshape(b * t, d)
    h = _rmsnorm(x2d, norm_g, BF16).reshape(b, t, d)
    proj, lora = _inproj(h, w_perm, mu_perm, g_perm)
    kk, ke0, ke1, bb0, bb1, cin0, cin1, cv = _prep(
        proj, lora, _pack_dirs(w0), _pack_lora(w_up), _pack_dirs(a0), _pack_lora(a_up),
        k_k.reshape(1, -1), k_a.reshape(1, -1), r_k.reshape(1, -1))
    of, ob = _wkv(proj, kk, ke0, ke1, bb0, bb1, cin0, cin1)
    attn = _attention(proj)
    n = b * t
    out = _post(x2d, of.reshape(n, -1), ob.reshape(n, -1), cv.reshape(n, -1), proj.reshape(n, D_PROJ),
                attn.reshape(n, -1), gn_w, gn_b, w_branch_rwkv.astype(BF16), w_branch_attn.astype(BF16),
                w_out.astype(BF16), out_g)
    return out.reshape(b, t, d)


def kernel(x, norm_g, w_in, shift_mu, w0, w_up, a0, a_up, k_k, k_a, r_k, gn_w, gn_b, q_norm_g, k_norm_g,
           w_branch_rwkv, w_branch_attn, w_out, final_norm_g):
    assert norm_g.shape[0] == 1, "single-layer block"
    return _layer(x, norm_g[0], w_in[0], shift_mu[0], w0[0], w_up[0], a0[0], a_up[0], k_k[0], k_a[0],
                  r_k[0], gn_w[0], gn_b[0], q_norm_g[0], k_norm_g[0], w_branch_rwkv[0],
                  w_branch_attn[0], w_out[0], final_norm_g)
```

```python
import functools
import math

import jax
import jax.numpy as jnp
import numpy as np
from jax import lax
from jax.experimental import pallas as pl
from jax.experimental.pallas import tpu as pltpu

F32 = jnp.float32
BF16 = jnp.bfloat16

HEAD_DIM = 64
R_HEADS = 16
R_WIDTH = R_HEADS * HEAD_DIM
LORA = 64
A_Q_HEADS = 16
A_KV_HEADS = 4
A_GROUP = A_Q_HEADS // A_KV_HEADS
A_WIDTH = A_Q_HEADS * HEAD_DIM
A_KV_WIDTH = A_KV_HEADS * HEAD_DIM
GRID_W = 64
ROPE_THETA = 10000.0
NORM_EPS = 1e-6
GN_EPS = 64e-5
KK_EPS = 1e-24
DECAY_SCALE = math.exp(-0.5)

LANES = 128
VMEM_LIMIT = 56 * 1024 * 1024

CHUNK = 64
PAIRS = R_WIDTH // LANES
POST_GN_WIDTH = 256
PREP_WIDTH = 256
WKV_CHUNKS_PER_STEP = 2
PREP_ROW_BANDS = 2
WKV_PAIRS_PER_STEP = 8

COL_XR, COL_XK, COL_XV, COL_ZR, COL_Q, COL_ZA = 0, 1024, 2048, 3072, 4096, 5120
COL_GR, COL_GA, COL_K, COL_V, COL_WD, COL_AD = 6144, 8192, 10240, 10496, 10752, 10880
D_IN = 11008
PROJ_TN = 512
PROJ_HALF = 256
D_PROJ = 11264
PROJ_ROW_CHUNK = 512
PROJ_SHIFT_PAD = 8
ATTN_VT_ROWS = HEAD_DIM + 16
ATTN_KEY_BLOCK = 128
ATTN_QUERY_TILE = 256
ATTN_SCORE_LOOKAHEAD = 8
LOG2_E = math.log2(math.e)


def _bdot(a, b):
    return jnp.dot(a.astype(BF16), b.astype(BF16), preferred_element_type=F32)


def _bdot_nt(a, b):
    return lax.dot_general(a.astype(BF16), b.astype(BF16), (((1,), (1,)), ((), ())),
                           preferred_element_type=F32)


def _bdot_tn(a, b):
    return lax.dot_general(a.astype(BF16), b.astype(BF16), (((0,), (0,)), ((), ())),
                           preferred_element_type=F32)


def _split2(x):
    hi = x.astype(BF16)
    lo = (x - hi.astype(F32)).astype(BF16)
    return hi, lo


def _dot_exact_rhs(a_bf16, x):
    hi, lo = _split2(x)
    d = lambda y: jnp.dot(a_bf16, y, preferred_element_type=F32)
    return d(hi) + d(lo)


def _dot_exact_lhs(x, b_bf16):
    hi, lo = _split2(x)
    d = lambda y: jnp.dot(y, b_bf16, preferred_element_type=F32)
    return d(hi) + d(lo)


def _dot_f32(x, w):
    xh = x.astype(BF16)
    xl = (x - xh.astype(F32)).astype(BF16)
    wh = w.astype(BF16)
    wl = (w - wh.astype(F32)).astype(BF16)
    d = lambda p, q: jnp.dot(p, q, preferred_element_type=F32)
    return d(xh, wh) + d(xl, wh) + d(xh, wl)


def _head_ones(width=LANES):
    r = lax.broadcasted_iota(jnp.int32, (width, width), 0) // HEAD_DIM
    c = lax.broadcasted_iota(jnp.int32, (width, width), 1) // HEAD_DIM
    return jnp.where(r == c, 1.0, 0.0).astype(BF16)


def _rmsnorm_kernel(x_ref, g_ref, o_ref):
    x = x_ref[...]
    y = x * lax.rsqrt(jnp.mean(x * x, axis=-1, keepdims=True) + NORM_EPS)
    o_ref[...] = (y * g_ref[...]).astype(o_ref.dtype)


def _rmsnorm(x2d, g, out_dtype, tm=512):
    n, d = x2d.shape
    return pl.pallas_call(
        _rmsnorm_kernel,
        grid=(n // tm,),
        in_specs=[pl.BlockSpec((tm, d), lambda i: (i, 0)),
                  pl.BlockSpec((1, d), lambda i: (0, 0))],
        out_specs=pl.BlockSpec((tm, d), lambda i: (i, 0)),
        out_shape=jax.ShapeDtypeStruct((n, d), out_dtype),
        compiler_params=pltpu.CompilerParams(dimension_semantics=("arbitrary",),
                                             vmem_limit_bytes=VMEM_LIMIT),
        name="rmsnorm",
    )(x2d, g.reshape(1, d))


def _rope_tables(t):
    rows = t // GRID_W
    row = np.repeat(np.arange(rows, dtype=np.float32), GRID_W)
    colv = np.tile(np.arange(GRID_W, dtype=np.float32), rows)
    axis_dim = HEAD_DIM // 2
    freqs = jnp.asarray(ROPE_THETA, F32) ** (-jnp.arange(0, axis_dim, 2, dtype=F32) / axis_dim)
    ang = jnp.concatenate([jnp.asarray(row)[:, None] * freqs, jnp.asarray(colv)[:, None] * freqs], axis=-1)
    reps = PROJ_HALF // HEAD_DIM
    cos = jnp.tile(jnp.repeat(jnp.cos(ang), 2, axis=-1), (1, reps))
    sin = jnp.tile(jnp.repeat(jnp.sin(ang), 2, axis=-1), (1, reps))
    return cos, sin


def _inproj_tile_kinds():
    def kind(col):
        if col < 3 * R_WIDTH or COL_WD <= col < COL_WD + PROJ_HALF:
            return "shift"
        if COL_Q <= col < COL_Q + A_WIDTH or COL_K <= col < COL_K + A_KV_WIDTH:
            return "rope"
        return "plain"
    return [(kind(j * PROJ_TN), kind(j * PROJ_TN + PROJ_HALF)) for j in range(D_PROJ // PROJ_TN)]


def _inproj_kernel(h_ref, w_ref, mu_ref, g_ref, cos_ref, sin_ref, o_ref, lora_ref, acc_ref, *, kinds):
    j = pl.program_id(1)
    t = h_ref.shape[1]
    rc = PROJ_ROW_CHUNK
    nchunk = t // rc
    pad = PROJ_SHIFT_PAD
    hw = PROJ_HALF

    def mm(c):
        return jnp.dot(h_ref[0, c * rc:(c + 1) * rc, :], w_ref[...], preferred_element_type=F32)

    def emit_plain(c, cols, acc):
        o_ref[0, c * rc:(c + 1) * rc, cols] = acc.astype(o_ref.dtype)

    def emit_rope(c, cols, acc):
        rows = slice(c * rc, (c + 1) * rc)
        ss = _dot_exact_lhs(acc * acc, _head_ones(hw)) * (1.0 / HEAD_DIM)
        y = acc * lax.rsqrt(ss + NORM_EPS) * g_ref[:, cols]
        lane = lax.broadcasted_iota(jnp.int32, (rc, hw), 1)
        nxt = pltpu.roll(y, hw - 1, axis=1)
        prv = pltpu.roll(y, 1, axis=1)
        o_ref[0, rows, cols] = (y * cos_ref[rows, :] + jnp.where((lane % 2) == 0, -nxt, prv)
                                * sin_ref[rows, :]).astype(o_ref.dtype)

    def emit_shift(c, cols, is_lora):
        lo = pad + c * rc
        mu = mu_ref[:, cols]
        cur = acc_ref[lo:lo + rc, cols]
        prev = acc_ref[lo - 1:lo - 1 + rc, cols]
        nxt = acc_ref[lo + 1:lo + 1 + rc, cols]
        y = cur + mu[0:1] * (prev - cur) + mu[1:2] * (nxt - cur)
        o_ref[0, c * rc:(c + 1) * rc, cols] = y.astype(o_ref.dtype)
        if is_lora:
            lora_ref[0, c * rc:(c + 1) * rc, :] = y

    def run(pair, is_lora):
        halves = [(slice(i * hw, (i + 1) * hw), k) for i, k in enumerate(pair)]
        if "shift" in pair:
            acc_ref[0:pad, :] = jnp.zeros((pad, PROJ_TN), F32)
            acc_ref[pad + t:2 * pad + t, :] = jnp.zeros((pad, PROJ_TN), F32)
        ahead = mm(0)
        for c in range(nchunk):
            acc, ahead = ahead, (mm(c + 1) if c + 1 < nchunk else None)
            for cols, k in halves:
                if k == "shift":
                    acc_ref[pad + c * rc:pad + (c + 1) * rc, cols] = acc[:, cols]
                    if c > 0:
                        emit_shift(c - 1, cols, is_lora)
                elif k == "rope":
                    emit_rope(c, cols, acc[:, cols])
                else:
                    emit_plain(c, cols, acc[:, cols])
        for cols, k in halves:
            if k == "shift":
                emit_shift(nchunk - 1, cols, is_lora)

    lora_tile = COL_WD // PROJ_TN
    for pair in sorted(set(kinds)):
        tiles = [i for i, k in enumerate(kinds) if k == pair and i != lora_tile]
        if tiles:
            cond = functools.reduce(jnp.logical_or, [j == i for i in tiles])
            pl.when(cond)(functools.partial(run, pair, False))
    pl.when(j == lora_tile)(functools.partial(run, kinds[lora_tile], True))


def _inproj(h, w_perm, mu_perm, g_perm):
    b, t, d = h.shape
    nj = D_PROJ // PROJ_TN
    cos, sin = _rope_tables(t)
    kern = functools.partial(_inproj_kernel, kinds=_inproj_tile_kinds())
    return pl.pallas_call(
        kern,
        grid=(b, nj),
        in_specs=[pl.BlockSpec((1, t, d), lambda i, j: (i, 0, 0)),
                  pl.BlockSpec((d, PROJ_TN), lambda i, j: (0, j)),
                  pl.BlockSpec((2, PROJ_TN), lambda i, j: (0, j)),
                  pl.BlockSpec((1, PROJ_TN), lambda i, j: (0, j)),
                  pl.BlockSpec((t, PROJ_HALF), lambda i, j: (0, 0)),
                  pl.BlockSpec((t, PROJ_HALF), lambda i, j: (0, 0))],
        out_specs=[pl.BlockSpec((1, t, PROJ_TN), lambda i, j: (i, 0, j)),
                   pl.BlockSpec((1, t, PROJ_HALF), lambda i, j: (i, 0, 0))],
        out_shape=[jax.ShapeDtypeStruct((b, t, D_PROJ), BF16),
                   jax.ShapeDtypeStruct((b, t, PROJ_HALF), F32)],
        scratch_shapes=[pltpu.VMEM((t + 2 * PROJ_SHIFT_PAD, PROJ_TN), F32)],
        compiler_params=pltpu.CompilerParams(dimension_semantics=("arbitrary", "arbitrary"),
                                             vmem_limit_bytes=VMEM_LIMIT),
        name="inproj",
    )(h, w_perm, mu_perm, g_perm, cos, sin)


def _prep_rows(rows, refs, consts):
    (xr_ref, xk_ref, xv_ref, lora_ref, w0_ref, wup_ref, a0_ref, aup_ref, kkg_ref, ka_ref, rk_ref,
     kk_ref, ke_refs, bb_refs, cin_refs, cv_ref) = refs
    ones, tri = consts
    L = CHUNK
    npair = PREP_WIDTH // LANES
    r = xr_ref[0, rows, :].astype(F32)
    k = xk_ref[0, rows, :].astype(F32)
    v = xv_ref[0, rows, :].astype(F32)
    lora = lora_ref[0, rows, :]
    wd = jnp.tanh(lora[:, :LANES])
    ad = lora[:, LANES:]
    k_a = ka_ref[...]
    kkn = k * kkg_ref[...]
    ss = _dot_exact_lhs(kkn * kkn, ones)
    w_raw = w0_ref[...] + _dot_f32(wd, wup_ref[...])
    a_raw = a0_ref[...] + _bdot(ad, aup_ref[...])
    yield
    kk = kkn * lax.rsqrt(jnp.maximum(ss, KK_EPS))
    kk_ref[0, rows, :] = kk.astype(kk_ref.dtype)
    lw_all = -DECAY_SCALE * jax.nn.sigmoid(w_raw)
    a_all = jax.nn.sigmoid(a_raw)
    nchunk = k.shape[0] // L
    row0 = rows.start
    ke_sum = None
    cins = []
    for d in range(2):
        pick = lambda z: jnp.concatenate(
            [z[:, (2 * p + d) * LANES:(2 * p + d + 1) * LANES] for p in range(npair)], axis=1)
        lw, a = pick(lw_all), pick(a_all)
        wide = jnp.concatenate([lw[c * L:(c + 1) * L, :] for c in range(nchunk)], axis=1)
        cins.append(_dot_exact_rhs(tri[d], wide))
        bb_refs[d][0, rows, :] = (kk * a).astype(bb_refs[d].dtype)
        ke = k * (1.0 + (a - 1.0) * k_a)
        ke_refs[d][0, rows, :] = ke.astype(ke_refs[d].dtype)
        ke_sum = ke if ke_sum is None else ke_sum + ke
    coef = _dot_exact_lhs(r * ke_sum * rk_ref[...], ones)
    yield
    for d in range(2):
        for c in range(nchunk):
            cin_refs[d][0, row0 + c * L:row0 + (c + 1) * L, :] = cins[d][:, c * PREP_WIDTH:(c + 1) * PREP_WIDTH]
    cv_ref[0, rows, :] = (coef * v).astype(cv_ref.dtype)


def _prep_kernel(xr_ref, xk_ref, xv_ref, lora_ref, w0_ref, wup_ref, a0_ref, aup_ref,
                 kkg_ref, ka_ref, rk_ref,
                 kk_ref, ke0_ref, ke1_ref, bb0_ref, bb1_ref, cin0_ref, cin1_ref, cv_ref):
    L = CHUNK
    tt_ = lax.broadcasted_iota(jnp.int32, (L, L), 0)
    ss_ = lax.broadcasted_iota(jnp.int32, (L, L), 1)
    tri = (jnp.where(tt_ >= ss_, 1.0, 0.0).astype(BF16), jnp.where(tt_ <= ss_, 1.0, 0.0).astype(BF16))
    consts = (_head_ones(PREP_WIDTH), tri)
    refs = (xr_ref, xk_ref, xv_ref, lora_ref, w0_ref, wup_ref, a0_ref, aup_ref, kkg_ref, ka_ref, rk_ref,
            kk_ref, (ke0_ref, ke1_ref), (bb0_ref, bb1_ref), (cin0_ref, cin1_ref), cv_ref)
    tt = xr_ref.shape[1]
    band = tt // PREP_ROW_BANDS
    _interleave([_prep_rows(slice(i * band, (i + 1) * band), refs, consts) for i in range(PREP_ROW_BANDS)])


def _prep(proj, lora, w0, wup_pad, a0, aup_pad, k_k, k_a, r_k, tt=1024):
    b, t, _ = proj.shape
    w = PREP_WIDTH
    col = lambda base: (lambda i, s, p: (i, s, base // w + p))
    vec = lambda i, s, p: (0, p)
    tile = pl.BlockSpec((1, tt, w), lambda i, s, p: (i, s, p))
    sd = lambda dt: jax.ShapeDtypeStruct((b, t, R_WIDTH), dt)
    return pl.pallas_call(
        _prep_kernel,
        grid=(b, t // tt, R_WIDTH // w),
        in_specs=[pl.BlockSpec((1, tt, w), col(COL_XR)),
                  pl.BlockSpec((1, tt, w), col(COL_XK)),
                  pl.BlockSpec((1, tt, w), col(COL_XV)),
                  pl.BlockSpec((1, tt, 2 * LANES), lambda i, s, p: (i, s, 0)),
                  pl.BlockSpec((1, 2 * w), vec),
                  pl.BlockSpec((LANES, 2 * w), vec),
                  pl.BlockSpec((1, 2 * w), vec),
                  pl.BlockSpec((LANES, 2 * w), vec),
                  pl.BlockSpec((1, w), vec),
                  pl.BlockSpec((1, w), vec),
                  pl.BlockSpec((1, w), vec)],
        out_specs=[tile] * 8,
        out_shape=[sd(BF16), sd(BF16), sd(BF16), sd(BF16), sd(BF16), sd(F32), sd(F32), sd(BF16)],
        compiler_params=pltpu.CompilerParams(
            dimension_semantics=("arbitrary", "arbitrary", "arbitrary"),
            vmem_limit_bytes=VMEM_LIMIT),
        name="rwkv_prep",
    )(proj, proj, proj, lora, w0, wup_pad, a0, aup_pad, k_k, k_a, r_k)


def _wkv_consts(reverse):
    L = CHUNK
    t = lax.broadcasted_iota(jnp.int32, (L, 2 * L), 0)
    s = lax.broadcasted_iota(jnp.int32, (L, 2 * L), 1) % L
    if reverse:
        strict, incl = t < s, t <= s
    else:
        strict, incl = t > s, t >= s
    eye2 = jnp.where(t == s, 1.0, 0.0).astype(F32)
    rr = lax.broadcasted_iota(jnp.int32, (LANES, LANES), 0)
    cc = lax.broadcasted_iota(jnp.int32, (LANES, LANES), 1)
    bd = (rr // HEAD_DIM) == (cc // HEAD_DIM)
    eye = rr == cc
    row = lax.broadcasted_iota(jnp.int32, (L, LANES), 0)
    first = row == (L - 1 if reverse else 0)
    return strict, incl, eye2, bd, eye, first


def _wkv_chunk(load, get_state, finish, consts, reverse):
    L = CHUNK
    strict, incl, eye2, bd, eye, first = consts

    def bd2(x):
        return jnp.where(bd, jnp.concatenate([x, x], axis=0), 0.0)

    cin, kk, b, ke, v, r = load()
    a = -kk
    cex = jnp.where(first, 0.0, pltpu.roll(cin, L - 1 if reverse else 1, axis=0))
    cend = cin[0:1, :] if reverse else cin[L - 1:L, :]
    m = 0.5 * cend
    e_in = jnp.exp(cin - m)
    e_out = jnp.exp(m - cin)
    em = jnp.exp(m)
    at = a * jnp.exp(cex - m)
    rt = r * e_in
    bt = b * e_out
    kt = ke * e_out
    ap, rp, bh, kh = at * em, rt * em, bt * em, kt * em

    ar = jnp.concatenate([at, rt], axis=0)
    g = _bdot_nt(ar, jnp.concatenate([bd2(bt), bd2(kt)], axis=0))
    yield
    a_ab = jnp.where(strict, g[:L, :LANES], 0.0)
    a_ak = jnp.where(strict, g[:L, LANES:], 0.0)
    p_rbk = jnp.concatenate([jnp.where(incl, g[L:, :LANES], 0.0),
                             jnp.where(incl, g[L:, LANES:], 0.0)], axis=1)

    n = a_ab
    p = eye2 + n
    n = _bdot(n, bd2(n))
    vb = bd2(v)
    x = _bdot(a_ak, vb)
    yield
    steps = int(math.log2(L)) - 1
    for i in range(steps):
        nb = bd2(n)
        if i < steps - 1:
            np_ = _bdot(jnp.concatenate([n, p], axis=0), nb)
            n = np_[:L]
            p = p + np_[L:]
        else:
            p = p + _bdot(p, nb)
        yield
    tinv = p

    wu = _bdot(tinv, jnp.concatenate([bd2(ap), bd2(x)], axis=1))
    yield
    w, u0 = wu[:, :LANES], wu[:, LANES:]
    zero = jnp.zeros((LANES, LANES), F32)
    rhs = jnp.concatenate([jnp.concatenate([bd2(w), bd2(u0)], axis=1),
                           jnp.concatenate([zero, vb], axis=1)], axis=0)
    pwo = _bdot(p_rbk, rhs)
    q = rp + pwo[:, :LANES]
    o0 = pwo[:, LANES:]
    zl = jnp.zeros((L, LANES), F32)
    mc = _bdot_tn(jnp.concatenate([bh, kh], axis=0),
                  jnp.concatenate([wu, jnp.concatenate([zl, v], axis=1)], axis=0))
    m_off = jnp.where(bd, mc[:, :LANES], 0.0)
    c = jnp.where(bd, mc[:, LANES:], 0.0)
    decay_col = jnp.sum(jnp.where(eye, jnp.exp(cend), 0.0), axis=1, keepdims=True)
    yield
    h = get_state()
    qm = _bdot(jnp.concatenate([q, m_off], axis=0), h)
    finish(qm[:L] + o0, decay_col * h + qm[L:] + c)


def _interleave(gens):
    results = [None] * len(gens)
    live = list(range(len(gens)))
    while live:
        for i in list(live):
            try:
                next(gens[i])
            except StopIteration as stop:
                results[i] = stop.value
                live.remove(i)
    return results


def _wkv_kernel(rf_ref, vf_ref, kkf_ref, kef_ref, bbf_ref, cinf_ref,
                rb_ref, vb_ref, kkb_ref, keb_ref, bbb_ref, cinb_ref,
                of_ref, ob_ref, h_ref):
    @pl.when(pl.program_id(2) == 0)
    def _():
        h_ref[...] = jnp.zeros_like(h_ref)

    dirs = ((cinf_ref, kkf_ref, bbf_ref, kef_ref, vf_ref, rf_ref, of_ref),
            (cinb_ref, kkb_ref, bbb_ref, keb_ref, vb_ref, rb_ref, ob_ref))
    consts = (_wkv_consts(False), _wkv_consts(True))
    L = CHUNK
    state = {}
    gens = []
    for i in range(WKV_PAIRS_PER_STEP):
        sl = slice(i * LANES, (i + 1) * LANES)
        for d in range(2):
            refs = dirs[d]
            order = range(WKV_CHUNKS_PER_STEP) if d == 0 else reversed(range(WKV_CHUNKS_PER_STEP))
            for s in order:
                rows = slice(s * L, (s + 1) * L)

                def load(refs=refs, rows=rows, sl=sl):
                    return tuple(ref[0, rows, sl].astype(F32) for ref in refs[:6])

                def get_state(d=d, i=i):
                    return state[d, i] if (d, i) in state else h_ref[d, i]

                def finish(out, h_new, o_ref=refs[6], rows=rows, sl=sl, d=d, i=i):
                    o_ref[0, rows, sl] = out.astype(o_ref.dtype)
                    state[d, i] = h_new

                gens.append(_wkv_chunk(load, get_state, finish, consts[d], reverse=(d == 1)))
    _interleave(gens)
    for (d, i), h_new in state.items():
        h_ref[d, i] = h_new


def _wkv(proj, kk, ke0, ke1, bb0, bb1, cin0, cin1):
    b, t, _ = proj.shape
    nc = t // (CHUNK * WKV_CHUNKS_PER_STEP)
    L = CHUNK * WKV_CHUNKS_PER_STEP
    w = WKV_PAIRS_PER_STEP * LANES
    fwd = lambda base: (lambda i, p, c: (i, c, base // w + p))
    bwd = lambda base: (lambda i, p, c: (i, nc - 1 - c, base // w + p))
    blk = lambda im: pl.BlockSpec((1, L, w), im)
    out_sd = jax.ShapeDtypeStruct((b, t, R_WIDTH), BF16)
    return pl.pallas_call(
        _wkv_kernel,
        grid=(b, PAIRS // WKV_PAIRS_PER_STEP, nc),
        in_specs=[blk(fwd(COL_XR)), blk(fwd(COL_XV)), blk(fwd(0)), blk(fwd(0)), blk(fwd(0)), blk(fwd(0)),
                  blk(bwd(COL_XR)), blk(bwd(COL_XV)), blk(bwd(0)), blk(bwd(0)), blk(bwd(0)), blk(bwd(0))],
        out_specs=[blk(fwd(0)), blk(bwd(0))],
        out_shape=[out_sd, out_sd],
        scratch_shapes=[pltpu.VMEM((2, WKV_PAIRS_PER_STEP, LANES, LANES), F32)],
        compiler_params=pltpu.CompilerParams(
            dimension_semantics=("arbitrary", "arbitrary", "arbitrary"),
            vmem_limit_bytes=VMEM_LIMIT),
        name="wkv",
    )(proj, proj, kk, ke0, bb0, cin0, proj, proj, kk, ke1, bb1, cin1)


def _attn_kernel(q_ref, k_ref, v_ref, o_ref, kh_ref, vt_ref, *, tq):
    t = k_ref.shape[1]
    kb = ATTN_KEY_BLOCK
    heads_per_tile = ATTN_QUERY_TILE // tq
    ntile = A_GROUP // heads_per_tile

    @pl.when(pl.program_id(1) == 0)
    def _():
        r = lax.broadcasted_iota(jnp.int32, (A_KV_WIDTH, HEAD_DIM), 0)
        c = lax.broadcasted_iota(jnp.int32, (A_KV_WIDTH, HEAD_DIM), 1)
        kf = k_ref[0]
        vt = v_ref[0].astype(F32).T.astype(BF16)
        ones = jnp.ones((ATTN_VT_ROWS - HEAD_DIM, t), BF16)
        for hk in range(A_KV_HEADS):
            sel = jnp.where(r == hk * HEAD_DIM + c, 1.0, 0.0).astype(BF16)
            kh_ref[hk] = jnp.dot(kf, sel, preferred_element_type=F32).astype(BF16)
            vt_ref[hk] = jnp.concatenate([vt[hk * HEAD_DIM:(hk + 1) * HEAD_DIM, :], ones], axis=0)

    qt = q_ref[0].astype(F32).T.astype(BF16)

    def q_tile(hk, c):
        h0 = hk * A_GROUP + c * heads_per_tile
        return jnp.concatenate([qt[(h0 + e) * HEAD_DIM:(h0 + e + 1) * HEAD_DIM, :]
                                for e in range(heads_per_tile)], axis=1)

    def scores(hk, c, j):
        return jnp.dot(kh_ref[hk, j * kb:(j + 1) * kb, :], q_tile(hk, c), preferred_element_type=F32)

    tasks = [(hk, c, j) for hk in range(A_KV_HEADS) for j in range(t // kb) for c in range(ntile)]
    state = {}
    done = {}
    ahead = [scores(*task) for task in tasks[:ATTN_SCORE_LOOKAHEAD]]
    for i, (hk, c, j) in enumerate(tasks):
        s = ahead.pop(0)
        if i + ATTN_SCORE_LOOKAHEAD < len(tasks):
            ahead.append(scores(*tasks[i + ATTN_SCORE_LOOKAHEAD]))
        if j == 0:
            state[hk, c] = (jnp.full((1, ATTN_QUERY_TILE), -1e30, F32),
                            jnp.zeros((ATTN_VT_ROWS, ATTN_QUERY_TILE), F32))
        m, acc = state[hk, c]
        m_new = jnp.maximum(m, jnp.max(s, axis=0, keepdims=True))
        p = jnp.exp2(s - m_new).astype(BF16)
        acc = acc * jnp.exp2(m - m_new) + jnp.dot(vt_ref[hk, :, j * kb:(j + 1) * kb], p,
                                                  preferred_element_type=F32)
        state[hk, c] = (m_new, acc)
        if j == t // kb - 1:
            ot = acc[:HEAD_DIM] / acc[HEAD_DIM:HEAD_DIM + 1]
            for e in range(heads_per_tile):
                done[hk * A_GROUP + c * heads_per_tile + e] = ot[:, e * tq:(e + 1) * tq]
    rows = [done[h] for h in range(A_Q_HEADS)]
    o_ref[0] = jnp.concatenate(rows, axis=0).T.astype(o_ref.dtype)


def _attention(proj, tq=256):
    b, t, _ = proj.shape
    kern = functools.partial(_attn_kernel, tq=tq)
    return pl.pallas_call(
        kern,
        grid=(b, t // tq),
        in_specs=[pl.BlockSpec((1, tq, A_WIDTH), lambda i, j: (i, j, COL_Q // A_WIDTH)),
                  pl.BlockSpec((1, t, A_KV_WIDTH), lambda i, j: (i, 0, COL_K // A_KV_WIDTH)),
                  pl.BlockSpec((1, t, A_KV_WIDTH), lambda i, j: (i, 0, COL_V // A_KV_WIDTH))],
        out_specs=pl.BlockSpec((1, tq, A_WIDTH), lambda i, j: (i, j, 0)),
        out_shape=jax.ShapeDtypeStruct((b, t, A_WIDTH), BF16),
        scratch_shapes=[pltpu.VMEM((A_KV_HEADS, t, HEAD_DIM), BF16),
                        pltpu.VMEM((A_KV_HEADS, ATTN_VT_ROWS, t), BF16)],
        compiler_params=pltpu.CompilerParams(dimension_semantics=("arbitrary", "arbitrary"),
                                             vmem_limit_bytes=VMEM_LIMIT),
        name="attention",
    )(proj, proj, proj)


def _post_kernel(x_ref, of_ref, ob_ref, cv_ref, zr_ref, at_ref, za_ref, gr_ref, ga_ref,
                 gnw_ref, gnb_ref, wbr_ref, wba_ref, wout_ref, fg_ref, o_ref):
    za = za_ref[...].astype(F32)
    o_a = at_ref[...].astype(F32) * (za * jax.nn.sigmoid(za))
    p_a = _bdot(o_a, wba_ref[...])
    gw = POST_GN_WIDTH
    ones = _head_ones(gw)
    wkv = of_ref[...].astype(F32) + ob_ref[...].astype(F32)
    groups = [slice(i * gw, (i + 1) * gw) for i in range(R_WIDTH // gw)]
    mus = [_dot_exact_lhs(wkv[:, sl], ones) * (1.0 / HEAD_DIM) for sl in groups]
    ycs = [wkv[:, sl] - mu for sl, mu in zip(groups, mus)]
    vrs = [_dot_exact_lhs(yc * yc, ones) * (1.0 / HEAD_DIM) for yc in ycs]
    gn = jnp.concatenate([yc * lax.rsqrt(var + GN_EPS) for yc, var in zip(ycs, vrs)], axis=1)
    gn = gn * gnw_ref[...] + gnb_ref[...]
    zr = zr_ref[...].astype(F32)
    o_r = (gn + cv_ref[...].astype(F32)) * (zr * jax.nn.sigmoid(zr))
    p_r = _bdot(o_r, wbr_ref[...])
    merged = (jax.nn.sigmoid(gr_ref[...].astype(F32)) * p_r
              + jax.nn.sigmoid(ga_ref[...].astype(F32)) * p_a)
    y = x_ref[...] + _bdot(merged, wout_ref[...])
    yn = y * lax.rsqrt(jnp.mean(y * y, axis=-1, keepdims=True) + NORM_EPS)
    o_ref[...] = yn * fg_ref[...]


def _post(x2d, of, ob, cv, proj2d, attn, gn_w, gn_b, w_br, w_ba, w_out, final_g, tm=256):
    n, d = x2d.shape
    row = lambda w, base: pl.BlockSpec((tm, w), lambda i: (i, base // w))
    const = lambda shape: pl.BlockSpec(shape, lambda i: (0, 0))
    return pl.pallas_call(
        _post_kernel,
        grid=(n // tm,),
        in_specs=[row(d, 0), row(R_WIDTH, 0), row(R_WIDTH, 0), row(R_WIDTH, 0),
                  row(R_WIDTH, COL_ZR), row(A_WIDTH, 0), row(A_WIDTH, COL_ZA),
                  row(d, COL_GR), row(d, COL_GA),
                  const((1, R_WIDTH)), const((1, R_WIDTH)),
                  const((R_WIDTH, d)), const((A_WIDTH, d)), const((d, d)), const((1, d))],
        out_specs=row(d, 0),
        out_shape=jax.ShapeDtypeStruct((n, d), F32),
        compiler_params=pltpu.CompilerParams(dimension_semantics=("arbitrary",),
                                             vmem_limit_bytes=VMEM_LIMIT),
        name="merge_out",
    )(x2d, of, ob, cv, proj2d, attn, proj2d, proj2d, proj2d,
      gn_w.reshape(1, -1), gn_b.reshape(1, -1), w_br, w_ba, w_out, final_g.reshape(1, -1))


def _column_segments():
    shift_w = 3 * R_WIDTH + 4 * LORA
    o_wd, o_ad = 3 * R_WIDTH, 3 * R_WIDTH + 2 * LORA
    o_zr = shift_w
    o_q = o_zr + R_WIDTH
    o_k = o_q + A_WIDTH
    o_v = o_k + A_KV_WIDTH
    o_za = o_v + A_KV_WIDTH
    o_g = o_za + A_WIDTH
    segs = [(0, 3 * R_WIDTH), (o_zr, R_WIDTH), (o_q, A_WIDTH), (o_za, A_WIDTH), (o_g, 4096),
            (o_k, A_KV_WIDTH), (o_v, A_KV_WIDTH), (o_wd, 2 * LORA), (o_ad, 2 * LORA)]
    assert sum(w for _, w in segs) == D_IN
    return segs, shift_w


def _permute_columns(a, segs):
    return jnp.concatenate([a[..., s:s + w] for s, w in segs], axis=-1)


def _pack_lora(up):
    z = jnp.zeros((LORA, PAIRS, LANES), up.dtype)
    top = jnp.concatenate([up[0].reshape(LORA, PAIRS, LANES), z], axis=2)
    bot = jnp.concatenate([z, up[1].reshape(LORA, PAIRS, LANES)], axis=2)
    return jnp.concatenate([top, bot], axis=0).reshape(2 * LORA, 2 * R_WIDTH)


def _pack_dirs(v):
    return jnp.stack([v[0].reshape(PAIRS, LANES), v[1].reshape(PAIRS, LANES)], axis=1).reshape(1, -1)


def _layer(x, norm_g, w_in, shift_mu, w0, w_up, a0, a_up, k_k, k_a, r_k, gn_w, gn_b,
           q_norm_g, k_norm_g, w_branch_rwkv, w_branch_attn, w_out, out_g):
    b, t, d = x.shape
    assert t % (CHUNK * WKV_CHUNKS_PER_STEP) == 0 and t % GRID_W == 0 and d == 2048 and w_in.shape[1] == D_IN
    segs, shift_w = _column_segments()
    pad_cols = lambda a: jnp.pad(a, ((0, 0), (0, D_PROJ - D_IN)))
    w_perm = pad_cols(_permute_columns(w_in.astype(BF16), segs))
    mu_full = jnp.concatenate([shift_mu, jnp.zeros((2, D_IN - shift_w), F32)], axis=1)
    mu_perm = pad_cols(_permute_columns(mu_full, segs))
    g_perm = jnp.zeros((1, D_PROJ), F32)
    g_perm = g_perm.at[0, COL_Q:COL_Q + A_WIDTH].set(jnp.tile(q_norm_g * (HEAD_DIM ** -0.5 * LOG2_E), A_Q_HEADS))
    g_perm = g_perm.at[0, COL_K:COL_K + A_KV_WIDTH].set(jnp.tile(k_norm_g, A_KV_HEADS))

    x2d = x.reshape(b * t, d)
    h = _rmsnorm(x2d, norm_g, BF16).reshape(b, t, d)
    proj, lora = _inproj(h, w_perm, mu_perm, g_perm)
    kk, ke0, ke1, bb0, bb1, cin0, cin1, cv = _prep(
        proj, lora, _pack_dirs(w0), _pack_lora(w_up), _pack_dirs(a0), _pack_lora(a_up),
        k_k.reshape(1, -1), k_a.reshape(1, -1), r_k.reshape(1, -1))
    of, ob = _wkv(proj, kk, ke0, ke1, bb0, bb1, cin0, cin1)
    attn = _attention(proj)
    n = b * t
    out = _post(x2d, of.reshape(n, -1), ob.reshape(n, -1), cv.reshape(n, -1), proj.reshape(n, D_PROJ),
                attn.reshape(n, -1), gn_w, gn_b, w_branch_rwkv.astype(BF16), w_branch_attn.astype(BF16),
                w_out.astype(BF16), out_g)
    return out.reshape(b, t, d)


def kernel(x, norm_g, w_in, shift_mu, w0, w_up, a0, a_up, k_k, k_a, r_k, gn_w, gn_b, q_norm_g, k_norm_g,
           w_branch_rwkv, w_branch_attn, w_out, final_norm_g):
    assert norm_g.shape[0] == 1, "single-layer block"
    return _layer(x, norm_g[0], w_in[0], shift_mu[0], w0[0], w_up[0], a0[0], a_up[0], k_k[0], k_a[0],
                  r_k[0], gn_w[0], gn_b[0], q_norm_g[0], k_norm_g[0], w_branch_rwkv[0],
                  w_branch_attn[0], w_out[0], final_norm_g)
```

```python
import functools
import math

import jax
import jax.numpy as jnp
import numpy as np
from jax import lax
from jax.experimental import pallas as pl
from jax.experimental.pallas import tpu as pltpu

F32 = jnp.float32
BF16 = jnp.bfloat16

HEAD_DIM = 64
R_HEADS = 16
R_WIDTH = R_HEADS * HEAD_DIM
LORA = 64
A_Q_HEADS = 16
A_KV_HEADS = 4
A_GROUP = A_Q_HEADS // A_KV_HEADS
A_WIDTH = A_Q_HEADS * HEAD_DIM
A_KV_WIDTH = A_KV_HEADS * HEAD_DIM
GRID_W = 64
ROPE_THETA = 10000.0
NORM_EPS = 1e-6
GN_EPS = 64e-5
KK_EPS = 1e-24
DECAY_SCALE = math.exp(-0.5)

D_MODEL = 2048
LOG2_E = math.log2(math.e)

LANES = 128
SUBLANES = 8
BF16_SUBLANES = 16
MXU_WIDTH = 256
VMEM_LIMIT = 56 * 1024 * 1024

NORM_ROWS = 1024
PROJ_TN = 2 * MXU_WIDTH
PROJ_HALF = MXU_WIDTH
PROJ_ROW_CHUNK = 256
PROJ_SHIFT_PAD = SUBLANES
PREP_ROWS = 2048
PREP_WIDTH = MXU_WIDTH
PREP_ROW_BANDS = 2
CHUNK = 64
PAIRS = R_WIDTH // LANES
WKV_PAIRS_PER_STEP = 8
WKV_CHUNKS_PER_STEP = 2
ATTN_QUERIES = 256
ATTN_KEY_BLOCK = 128
ATTN_QUERY_TILE = MXU_WIDTH
ATTN_SCORE_LOOKAHEAD = 8
ATTN_VT_ROWS = HEAD_DIM + BF16_SUBLANES
POST_ROWS = 256
POST_GN_WIDTH = MXU_WIDTH

COL_XR, COL_XK, COL_XV = 0, R_WIDTH, 2 * R_WIDTH
COL_ZR = 3 * R_WIDTH
COL_Q = COL_ZR + R_WIDTH
COL_ZA = COL_Q + A_WIDTH
COL_GR = COL_ZA + A_WIDTH
COL_GA = COL_GR + D_MODEL
COL_K = COL_GA + D_MODEL
COL_V = COL_K + A_KV_WIDTH
COL_WD = COL_V + A_KV_WIDTH
COL_AD = COL_WD + 2 * LORA
D_IN = COL_AD + 2 * LORA
D_PROJ = -(-D_IN // PROJ_TN) * PROJ_TN


def _bdot(a, b):
    return jnp.dot(a.astype(BF16), b.astype(BF16), preferred_element_type=F32)


def _bdot_nt(a, b):
    return lax.dot_general(a.astype(BF16), b.astype(BF16), (((1,), (1,)), ((), ())),
                           preferred_element_type=F32)


def _bdot_tn(a, b):
    return lax.dot_general(a.astype(BF16), b.astype(BF16), (((0,), (0,)), ((), ())),
                           preferred_element_type=F32)


def _split2(x):
    hi = x.astype(BF16)
    lo = (x - hi.astype(F32)).astype(BF16)
    return hi, lo


def _dot_exact_rhs(a_bf16, x):
    hi, lo = _split2(x)
    d = lambda y: jnp.dot(a_bf16, y, preferred_element_type=F32)
    return d(hi) + d(lo)


def _dot_exact_lhs(x, b_bf16):
    hi, lo = _split2(x)
    d = lambda y: jnp.dot(y, b_bf16, preferred_element_type=F32)
    return d(hi) + d(lo)


def _dot_f32(x, w):
    xh = x.astype(BF16)
    xl = (x - xh.astype(F32)).astype(BF16)
    wh = w.astype(BF16)
    wl = (w - wh.astype(F32)).astype(BF16)
    d = lambda p, q: jnp.dot(p, q, preferred_element_type=F32)
    return d(xh, wh) + d(xl, wh) + d(xh, wl)


def _head_ones(width=LANES):
    r = lax.broadcasted_iota(jnp.int32, (width, width), 0) // HEAD_DIM
    c = lax.broadcasted_iota(jnp.int32, (width, width), 1) // HEAD_DIM
    return jnp.where(r == c, 1.0, 0.0).astype(BF16)


def _rmsnorm_kernel(x_ref, g_ref, o_ref):
    x = x_ref[...]
    y = x * lax.rsqrt(jnp.mean(x * x, axis=-1, keepdims=True) + NORM_EPS)
    o_ref[...] = (y * g_ref[...]).astype(o_ref.dtype)


def _rmsnorm(x2d, g, out_dtype, tm=NORM_ROWS):
    n, d = x2d.shape
    return pl.pallas_call(
        _rmsnorm_kernel,
        grid=(n // tm,),
        in_specs=[pl.BlockSpec((tm, d), lambda i: (i, 0)),
                  pl.BlockSpec((1, d), lambda i: (0, 0))],
        out_specs=pl.BlockSpec((tm, d), lambda i: (i, 0)),
        out_shape=jax.ShapeDtypeStruct((n, d), out_dtype),
        compiler_params=pltpu.CompilerParams(dimension_semantics=("arbitrary",),
                                             vmem_limit_bytes=VMEM_LIMIT),
        name="rmsnorm",
    )(x2d, g.reshape(1, d))


def _rope_tables(t):
    rows = t // GRID_W
    row = np.repeat(np.arange(rows, dtype=np.float32), GRID_W)
    colv = np.tile(np.arange(GRID_W, dtype=np.float32), rows)
    axis_dim = HEAD_DIM // 2
    freqs = jnp.asarray(ROPE_THETA, F32) ** (-jnp.arange(0, axis_dim, 2, dtype=F32) / axis_dim)
    ang = jnp.concatenate([jnp.asarray(row)[:, None] * freqs, jnp.asarray(colv)[:, None] * freqs], axis=-1)
    reps = PROJ_HALF // HEAD_DIM
    cos = jnp.tile(jnp.repeat(jnp.cos(ang), 2, axis=-1), (1, reps))
    sin = jnp.tile(jnp.repeat(jnp.sin(ang), 2, axis=-1), (1, reps))
    return cos, sin


def _inproj_tile_kinds():
    def kind(col):
        if col < 3 * R_WIDTH or COL_WD <= col < COL_WD + PROJ_HALF:
            return "shift"
        if COL_Q <= col < COL_Q + A_WIDTH or COL_K <= col < COL_K + A_KV_WIDTH:
            return "rope"
        return "plain"
    return [(kind(j * PROJ_TN), kind(j * PROJ_TN + PROJ_HALF)) for j in range(D_PROJ // PROJ_TN)]


def _inproj_kernel(h_ref, w_ref, mu_ref, g_ref, cos_ref, sin_ref, o_ref, lora_ref, acc_ref, *, kinds):
    j = pl.program_id(1)
    t = h_ref.shape[1]
    rc = PROJ_ROW_CHUNK
    nchunk = t // rc
    pad = PROJ_SHIFT_PAD
    hw = PROJ_HALF

    def mm(c):
        return jnp.dot(h_ref[0, c * rc:(c + 1) * rc, :], w_ref[...], preferred_element_type=F32)

    def emit_plain(c, cols, acc):
        o_ref[0, c * rc:(c + 1) * rc, cols] = acc.astype(o_ref.dtype)

    def emit_rope(c, cols, acc):
        rows = slice(c * rc, (c + 1) * rc)
        ss = _dot_exact_lhs(acc * acc, _head_ones(hw)) * (1.0 / HEAD_DIM)
        y = acc * lax.rsqrt(ss + NORM_EPS) * g_ref[:, cols]
        lane = lax.broadcasted_iota(jnp.int32, (rc, hw), 1)
        nxt = pltpu.roll(y, hw - 1, axis=1)
        prv = pltpu.roll(y, 1, axis=1)
        o_ref[0, rows, cols] = (y * cos_ref[rows, :] + jnp.where((lane % 2) == 0, -nxt, prv)
                                * sin_ref[rows, :]).astype(o_ref.dtype)

    def emit_shift(c, cols, is_lora):
        lo = pad + c * rc
        mu = mu_ref[:, cols]
        cur = acc_ref[lo:lo + rc, cols]
        prev = acc_ref[lo - 1:lo - 1 + rc, cols]
        nxt = acc_ref[lo + 1:lo + 1 + rc, cols]
        y = cur + mu[0:1] * (prev - cur) + mu[1:2] * (nxt - cur)
        o_ref[0, c * rc:(c + 1) * rc, cols] = y.astype(o_ref.dtype)
        if is_lora:
            lora_ref[0, c * rc:(c + 1) * rc, :] = y

    def run(pair, is_lora):
        halves = [(slice(i * hw, (i + 1) * hw), k) for i, k in enumerate(pair)]
        if "shift" in pair:
            acc_ref[0:pad, :] = jnp.zeros((pad, PROJ_TN), F32)
            acc_ref[pad + t:2 * pad + t, :] = jnp.zeros((pad, PROJ_TN), F32)
        ahead = mm(0)
        for c in range(nchunk):
            acc, ahead = ahead, (mm(c + 1) if c + 1 < nchunk else None)
            for cols, k in halves:
                if k == "shift":
                    acc_ref[pad + c * rc:pad + (c + 1) * rc, cols] = acc[:, cols]
                    if c > 0:
                        emit_shift(c - 1, cols, is_lora)
                elif k == "rope":
                    emit_rope(c, cols, acc[:, cols])
                else:
                    emit_plain(c, cols, acc[:, cols])
        for cols, k in halves:
            if k == "shift":
                emit_shift(nchunk - 1, cols, is_lora)

    lora_tile = COL_WD // PROJ_TN
    for pair in sorted(set(kinds)):
        tiles = [i for i, k in enumerate(kinds) if k == pair and i != lora_tile]
        if tiles:
            cond = functools.reduce(jnp.logical_or, [j == i for i in tiles])
            pl.when(cond)(functools.partial(run, pair, False))
    pl.when(j == lora_tile)(functools.partial(run, kinds[lora_tile], True))


def _inproj(h, w_perm, mu_perm, g_perm):
    b, t, d = h.shape
    nj = D_PROJ // PROJ_TN
    cos, sin = _rope_tables(t)
    kern = functools.partial(_inproj_kernel, kinds=_inproj_tile_kinds())
    return pl.pallas_call(
        kern,
        grid=(b, nj),
        in_specs=[pl.BlockSpec((1, t, d), lambda i, j: (i, 0, 0)),
                  pl.BlockSpec((d, PROJ_TN), lambda i, j: (0, j)),
                  pl.BlockSpec((2, PROJ_TN), lambda i, j: (0, j)),
                  pl.BlockSpec((1, PROJ_TN), lambda i, j: (0, j)),
                  pl.BlockSpec((t, PROJ_HALF), lambda i, j: (0, 0)),
                  pl.BlockSpec((t, PROJ_HALF), lambda i, j: (0, 0))],
        out_specs=[pl.BlockSpec((1, t, PROJ_TN), lambda i, j: (i, 0, j)),
                   pl.BlockSpec((1, t, PROJ_HALF), lambda i, j: (i, 0, 0))],
        out_shape=[jax.ShapeDtypeStruct((b, t, D_PROJ), BF16),
                   jax.ShapeDtypeStruct((b, t, PROJ_HALF), F32)],
        scratch_shapes=[pltpu.VMEM((t + 2 * PROJ_SHIFT_PAD, PROJ_TN), F32)],
        compiler_params=pltpu.CompilerParams(dimension_semantics=("arbitrary", "arbitrary"),
                                             vmem_limit_bytes=VMEM_LIMIT),
        name="inproj",
    )(h, w_perm, mu_perm, g_perm, cos, sin)


def _prep_rows(rows, refs, consts):
    (xr_ref, xk_ref, xv_ref, lora_ref, w0_ref, wup_ref, a0_ref, aup_ref, kkg_ref, ka_ref, rk_ref,
     kk_ref, ke_refs, bb_refs, cin_refs, cv_ref) = refs
    ones, tri = consts
    L = CHUNK
    npair = PREP_WIDTH // LANES
    r = xr_ref[0, rows, :].astype(F32)
    k = xk_ref[0, rows, :].astype(F32)
    v = xv_ref[0, rows, :].astype(F32)
    lora = lora_ref[0, rows, :]
    wd = jnp.tanh(lora[:, :LANES])
    ad = lora[:, LANES:]
    k_a = ka_ref[...]
    kkn = k * kkg_ref[...]
    ss = _dot_exact_lhs(kkn * kkn, ones)
    w_raw = w0_ref[...] + _dot_f32(wd, wup_ref[...])
    a_raw = a0_ref[...] + _bdot(ad, aup_ref[...])
    yield
    kk = kkn * lax.rsqrt(jnp.maximum(ss, KK_EPS))
    kk_ref[0, rows, :] = kk.astype(kk_ref.dtype)
    lw_all = -DECAY_SCALE * jax.nn.sigmoid(w_raw)
    a_all = jax.nn.sigmoid(a_raw)
    nchunk = k.shape[0] // L
    row0 = rows.start
    ke_sum = None
    cins = []
    for d in range(2):
        pick = lambda z: jnp.concatenate(
            [z[:, (2 * p + d) * LANES:(2 * p + d + 1) * LANES] for p in range(npair)], axis=1)
        lw, a = pick(lw_all), pick(a_all)
        wide = jnp.concatenate([lw[c * L:(c + 1) * L, :] for c in range(nchunk)], axis=1)
        cins.append(_dot_exact_rhs(tri[d], wide))
        bb_refs[d][0, rows, :] = (kk * a).astype(bb_refs[d].dtype)
        ke = k * (1.0 + (a - 1.0) * k_a)
        ke_refs[d][0, rows, :] = ke.astype(ke_refs[d].dtype)
        ke_sum = ke if ke_sum is None else ke_sum + ke
    coef = _dot_exact_lhs(r * ke_sum * rk_ref[...], ones)
    yield
    for d in range(2):
        for c in range(nchunk):
            cin_refs[d][0, row0 + c * L:row0 + (c + 1) * L, :] = cins[d][:, c * PREP_WIDTH:(c + 1) * PREP_WIDTH]
    cv_ref[0, rows, :] = (coef * v).astype(cv_ref.dtype)


def _prep_kernel(xr_ref, xk_ref, xv_ref, lora_ref, w0_ref, wup_ref, a0_ref, aup_ref,
                 kkg_ref, ka_ref, rk_ref,
                 kk_ref, ke0_ref, ke1_ref, bb0_ref, bb1_ref, cin0_ref, cin1_ref, cv_ref):
    L = CHUNK
    tt_ = lax.broadcasted_iota(jnp.int32, (L, L), 0)
    ss_ = lax.broadcasted_iota(jnp.int32, (L, L), 1)
    tri = (jnp.where(tt_ >= ss_, 1.0, 0.0).astype(BF16), jnp.where(tt_ <= ss_, 1.0, 0.0).astype(BF16))
    consts = (_head_ones(PREP_WIDTH), tri)
    refs = (xr_ref, xk_ref, xv_ref, lora_ref, w0_ref, wup_ref, a0_ref, aup_ref, kkg_ref, ka_ref, rk_ref,
            kk_ref, (ke0_ref, ke1_ref), (bb0_ref, bb1_ref), (cin0_ref, cin1_ref), cv_ref)
    tt = xr_ref.shape[1]
    band = tt // PREP_ROW_BANDS
    _interleave([_prep_rows(slice(i * band, (i + 1) * band), refs, consts) for i in range(PREP_ROW_BANDS)])


def _prep(proj, lora, w0, wup_pad, a0, aup_pad, k_k, k_a, r_k, tt=PREP_ROWS):
    b, t, _ = proj.shape
    tt = min(tt, t)
    w = PREP_WIDTH
    col = lambda base: (lambda i, s, p: (i, s, base // w + p))
    vec = lambda i, s, p: (0, p)
    tile = pl.BlockSpec((1, tt, w), lambda i, s, p: (i, s, p))
    sd = lambda dt: jax.ShapeDtypeStruct((b, t, R_WIDTH), dt)
    return pl.pallas_call(
        _prep_kernel,
        grid=(b, t // tt, R_WIDTH // w),
        in_specs=[pl.BlockSpec((1, tt, w), col(COL_XR)),
                  pl.BlockSpec((1, tt, w), col(COL_XK)),
                  pl.BlockSpec((1, tt, w), col(COL_XV)),
                  pl.BlockSpec((1, tt, 2 * LANES), lambda i, s, p: (i, s, 0)),
                  pl.BlockSpec((1, 2 * w), vec),
                  pl.BlockSpec((LANES, 2 * w), vec),
                  pl.BlockSpec((1, 2 * w), vec),
                  pl.BlockSpec((LANES, 2 * w), vec),
                  pl.BlockSpec((1, w), vec),
                  pl.BlockSpec((1, w), vec),
                  pl.BlockSpec((1, w), vec)],
        out_specs=[tile] * 8,
        out_shape=[sd(BF16), sd(BF16), sd(BF16), sd(BF16), sd(BF16), sd(F32), sd(F32), sd(BF16)],
        compiler_params=pltpu.CompilerParams(
            dimension_semantics=("arbitrary", "arbitrary", "arbitrary"),
            vmem_limit_bytes=VMEM_LIMIT),
        name="rwkv_prep",
    )(proj, proj, proj, lora, w0, wup_pad, a0, aup_pad, k_k, k_a, r_k)


def _wkv_consts(reverse):
    L = CHUNK
    t = lax.broadcasted_iota(jnp.int32, (L, 2 * L), 0)
    s = lax.broadcasted_iota(jnp.int32, (L, 2 * L), 1) % L
    if reverse:
        strict, incl = t < s, t <= s
    else:
        strict, incl = t > s, t >= s
    eye2 = jnp.where(t == s, 1.0, 0.0).astype(F32)
    rr = lax.broadcasted_iota(jnp.int32, (LANES, LANES), 0)
    cc = lax.broadcasted_iota(jnp.int32, (LANES, LANES), 1)
    bd = (rr // HEAD_DIM) == (cc // HEAD_DIM)
    eye = rr == cc
    row = lax.broadcasted_iota(jnp.int32, (L, LANES), 0)
    first = row == (L - 1 if reverse else 0)
    return strict, incl, eye2, bd, eye, first


def _wkv_chunk(load, get_state, finish, consts, reverse):
    L = CHUNK
    strict, incl, eye2, bd, eye, first = consts

    def bd2(x):
        return jnp.where(bd, jnp.concatenate([x, x], axis=0), 0.0)

    cin, kk, b, ke, v, r = load()
    a = -kk
    cex = jnp.where(first, 0.0, pltpu.roll(cin, L - 1 if reverse else 1, axis=0))
    cend = cin[0:1, :] if reverse else cin[L - 1:L, :]
    m = 0.5 * cend
    e_in = jnp.exp(cin - m)
    e_out = jnp.exp(m - cin)
    em = jnp.exp(m)
    at = a * jnp.exp(cex - m)
    rt = r * e_in
    bt = b * e_out
    kt = ke * e_out
    ap, rp, bh, kh = at * em, rt * em, bt * em, kt * em

    ar = jnp.concatenate([at, rt], axis=0)
    g = _bdot_nt(ar, jnp.concatenate([bd2(bt), bd2(kt)], axis=0))
    yield
    a_ab = jnp.where(strict, g[:L, :LANES], 0.0)
    a_ak = jnp.where(strict, g[:L, LANES:], 0.0)
    p_rbk = jnp.concatenate([jnp.where(incl, g[L:, :LANES], 0.0),
                             jnp.where(incl, g[L:, LANES:], 0.0)], axis=1)

    n = a_ab
    p = eye2 + n
    n = _bdot(n, bd2(n))
    vb = bd2(v)
    x = _bdot(a_ak, vb)
    yield
    steps = int(math.log2(L)) - 1
    for i in range(steps):
        nb = bd2(n)
        if i < steps - 1:
            np_ = _bdot(jnp.concatenate([n, p], axis=0), nb)
            n = np_[:L]
            p = p + np_[L:]
        else:
            p = p + _bdot(p, nb)
        yield
    tinv = p

    wu = _bdot(tinv, jnp.concatenate([bd2(ap), bd2(x)], axis=1))
    yield
    w, u0 = wu[:, :LANES], wu[:, LANES:]
    zero = jnp.zeros((LANES, LANES), F32)
    rhs = jnp.concatenate([jnp.concatenate([bd2(w), bd2(u0)], axis=1),
                           jnp.concatenate([zero, vb], axis=1)], axis=0)
    pwo = _bdot(p_rbk, rhs)
    q = rp + pwo[:, :LANES]
    o0 = pwo[:, LANES:]
    zl = jnp.zeros((L, LANES), F32)
    mc = _bdot_tn(jnp.concatenate([bh, kh], axis=0),
                  jnp.concatenate([wu, jnp.concatenate([zl, v], axis=1)], axis=0))
    m_off = jnp.where(bd, mc[:, :LANES], 0.0)
    c = jnp.where(bd, mc[:, LANES:], 0.0)
    decay_col = jnp.sum(jnp.where(eye, jnp.exp(cend), 0.0), axis=1, keepdims=True)
    yield
    h = get_state()
    qm = _bdot(jnp.concatenate([q, m_off], axis=0), h)
    finish(qm[:L] + o0, decay_col * h + qm[L:] + c)


def _interleave(gens):
    results = [None] * len(gens)
    live = list(range(len(gens)))
    while live:
        for i in list(live):
            try:
                next(gens[i])
            except StopIteration as stop:
                results[i] = stop.value
                live.remove(i)
    return results


def _wkv_kernel(rf_ref, vf_ref, kkf_ref, kef_ref, bbf_ref, cinf_ref,
                rb_ref, vb_ref, kkb_ref, keb_ref, bbb_ref, cinb_ref,
                of_ref, ob_ref, h_ref):
    @pl.when(pl.program_id(2) == 0)
    def _():
        h_ref[...] = jnp.zeros_like(h_ref)

    dirs = ((cinf_ref, kkf_ref, bbf_ref, kef_ref, vf_ref, rf_ref, of_ref),
            (cinb_ref, kkb_ref, bbb_ref, keb_ref, vb_ref, rb_ref, ob_ref))
    consts = (_wkv_consts(False), _wkv_consts(True))
    L = CHUNK
    state = {}
    gens = []
    for i in range(WKV_PAIRS_PER_STEP):
        sl = slice(i * LANES, (i + 1) * LANES)
        for d in range(2):
            refs = dirs[d]
            order = range(WKV_CHUNKS_PER_STEP) if d == 0 else reversed(range(WKV_CHUNKS_PER_STEP))
            for s in order:
                rows = slice(s * L, (s + 1) * L)

                def load(refs=refs, rows=rows, sl=sl):
                    return tuple(ref[0, rows, sl].astype(F32) for ref in refs[:6])

                def get_state(d=d, i=i):
                    return state[d, i] if (d, i) in state else h_ref[d, i]

                def finish(out, h_new, o_ref=refs[6], rows=rows, sl=sl, d=d, i=i):
                    o_ref[0, rows, sl] = out.astype(o_ref.dtype)
                    state[d, i] = h_new

                gens.append(_wkv_chunk(load, get_state, finish, consts[d], reverse=(d == 1)))
    _interleave(gens)
    for (d, i), h_new in state.items():
        h_ref[d, i] = h_new


def _wkv(proj, kk, ke0, ke1, bb0, bb1, cin0, cin1):
    b, t, _ = proj.shape
    nc = t // (CHUNK * WKV_CHUNKS_PER_STEP)
    L = CHUNK * WKV_CHUNKS_PER_STEP
    w = WKV_PAIRS_PER_STEP * LANES
    fwd = lambda base: (lambda i, p, c: (i, c, base // w + p))
    bwd = lambda base: (lambda i, p, c: (i, nc - 1 - c, base // w + p))
    blk = lambda im: pl.BlockSpec((1, L, w), im)
    out_sd = jax.ShapeDtypeStruct((b, t, R_WIDTH), BF16)
    return pl.pallas_call(
        _wkv_kernel,
        grid=(b, PAIRS // WKV_PAIRS_PER_STEP, nc),
        in_specs=[blk(fwd(COL_XR)), blk(fwd(COL_XV)), blk(fwd(0)), blk(fwd(0)), blk(fwd(0)), blk(fwd(0)),
                  blk(bwd(COL_XR)), blk(bwd(COL_XV)), blk(bwd(0)), blk(bwd(0)), blk(bwd(0)), blk(bwd(0))],
        out_specs=[blk(fwd(0)), blk(bwd(0))],
        out_shape=[out_sd, out_sd],
        scratch_shapes=[pltpu.VMEM((2, WKV_PAIRS_PER_STEP, LANES, LANES), F32)],
        compiler_params=pltpu.CompilerParams(
            dimension_semantics=("arbitrary", "arbitrary", "arbitrary"),
            vmem_limit_bytes=VMEM_LIMIT),
        name="wkv",
    )(proj, proj, kk, ke0, bb0, cin0, proj, proj, kk, ke1, bb1, cin1)


def _attn_kernel(q_ref, k_ref, v_ref, o_ref, kh_ref, vt_ref, *, tq):
    t = k_ref.shape[1]
    kb = ATTN_KEY_BLOCK
    heads_per_tile = ATTN_QUERY_TILE // tq
    ntile = A_GROUP // heads_per_tile

    @pl.when(pl.program_id(1) == 0)
    def _():
        r = lax.broadcasted_iota(jnp.int32, (A_KV_WIDTH, HEAD_DIM), 0)
        c = lax.broadcasted_iota(jnp.int32, (A_KV_WIDTH, HEAD_DIM), 1)
        kf = k_ref[0]
        vt = v_ref[0].astype(F32).T.astype(BF16)
        ones = jnp.ones((ATTN_VT_ROWS - HEAD_DIM, t), BF16)
        for hk in range(A_KV_HEADS):
            sel = jnp.where(r == hk * HEAD_DIM + c, 1.0, 0.0).astype(BF16)
            kh_ref[hk] = jnp.dot(kf, sel, preferred_element_type=F32).astype(BF16)
            vt_ref[hk] = jnp.concatenate([vt[hk * HEAD_DIM:(hk + 1) * HEAD_DIM, :], ones], axis=0)

    qt = q_ref[0].astype(F32).T.astype(BF16)

    def q_tile(hk, c):
        h0 = hk * A_GROUP + c * heads_per_tile
        return jnp.concatenate([qt[(h0 + e) * HEAD_DIM:(h0 + e + 1) * HEAD_DIM, :]
                                for e in range(heads_per_tile)], axis=1)

    def scores(hk, c, j):
        return jnp.dot(kh_ref[hk, j * kb:(j + 1) * kb, :], q_tile(hk, c), preferred_element_type=F32)

    tasks = [(hk, c, j) for hk in range(A_KV_HEADS) for j in range(t // kb) for c in range(ntile)]
    state = {}
    done = {}
    ahead = [scores(*task) for task in tasks[:ATTN_SCORE_LOOKAHEAD]]
    for i, (hk, c, j) in enumerate(tasks):
        s = ahead.pop(0)
        if i + ATTN_SCORE_LOOKAHEAD < len(tasks):
            ahead.append(scores(*tasks[i + ATTN_SCORE_LOOKAHEAD]))
        if j == 0:
            state[hk, c] = (jnp.full((1, ATTN_QUERY_TILE), -1e30, F32),
                            jnp.zeros((ATTN_VT_ROWS, ATTN_QUERY_TILE), F32))
        m, acc = state[hk, c]
        m_new = jnp.maximum(m, jnp.max(s, axis=0, keepdims=True))
        p = jnp.exp2(s - m_new).astype(BF16)
        acc = acc * jnp.exp2(m - m_new) + jnp.dot(vt_ref[hk, :, j * kb:(j + 1) * kb], p,
                                                  preferred_element_type=F32)
        state[hk, c] = (m_new, acc)
        if j == t // kb - 1:
            ot = acc[:HEAD_DIM] / acc[HEAD_DIM:HEAD_DIM + 1]
            for e in range(heads_per_tile):
                done[hk * A_GROUP + c * heads_per_tile + e] = ot[:, e * tq:(e + 1) * tq]
    rows = [done[h] for h in range(A_Q_HEADS)]
    o_ref[0] = jnp.concatenate(rows, axis=0).T.astype(o_ref.dtype)


def _attention(proj, tq=ATTN_QUERIES):
    b, t, _ = proj.shape
    kern = functools.partial(_attn_kernel, tq=tq)
    return pl.pallas_call(
        kern,
        grid=(b, t // tq),
        in_specs=[pl.BlockSpec((1, tq, A_WIDTH), lambda i, j: (i, j, COL_Q // A_WIDTH)),
                  pl.BlockSpec((1, t, A_KV_WIDTH), lambda i, j: (i, 0, COL_K // A_KV_WIDTH)),
                  pl.BlockSpec((1, t, A_KV_WIDTH), lambda i, j: (i, 0, COL_V // A_KV_WIDTH))],
        out_specs=pl.BlockSpec((1, tq, A_WIDTH), lambda i, j: (i, j, 0)),
        out_shape=jax.ShapeDtypeStruct((b, t, A_WIDTH), BF16),
        scratch_shapes=[pltpu.VMEM((A_KV_HEADS, t, HEAD_DIM), BF16),
                        pltpu.VMEM((A_KV_HEADS, ATTN_VT_ROWS, t), BF16)],
        compiler_params=pltpu.CompilerParams(dimension_semantics=("arbitrary", "arbitrary"),
                                             vmem_limit_bytes=VMEM_LIMIT),
        name="attention",
    )(proj, proj, proj)


def _post_kernel(x_ref, of_ref, ob_ref, cv_ref, zr_ref, at_ref, za_ref, gr_ref, ga_ref,
                 gnw_ref, gnb_ref, wbr_ref, wba_ref, wout_ref, fg_ref, o_ref):
    za = za_ref[...].astype(F32)
    o_a = at_ref[...].astype(F32) * (za * jax.nn.sigmoid(za))
    p_a = _bdot(o_a, wba_ref[...])
    gw = POST_GN_WIDTH
    ones = _head_ones(gw)
    wkv = of_ref[...].astype(F32) + ob_ref[...].astype(F32)
    groups = [slice(i * gw, (i + 1) * gw) for i in range(R_WIDTH // gw)]
    mus = [_dot_exact_lhs(wkv[:, sl], ones) * (1.0 / HEAD_DIM) for sl in groups]
    ycs = [wkv[:, sl] - mu for sl, mu in zip(groups, mus)]
    vrs = [_dot_exact_lhs(yc * yc, ones) * (1.0 / HEAD_DIM) for yc in ycs]
    gn = jnp.concatenate([yc * lax.rsqrt(var + GN_EPS) for yc, var in zip(ycs, vrs)], axis=1)
    gn = gn * gnw_ref[...] + gnb_ref[...]
    zr = zr_ref[...].astype(F32)
    o_r = (gn + cv_ref[...].astype(F32)) * (zr * jax.nn.sigmoid(zr))
    p_r = _bdot(o_r, wbr_ref[...])
    merged = (jax.nn.sigmoid(gr_ref[...].astype(F32)) * p_r
              + jax.nn.sigmoid(ga_ref[...].astype(F32)) * p_a)
    y = x_ref[...] + _bdot(merged, wout_ref[...])
    yn = y * lax.rsqrt(jnp.mean(y * y, axis=-1, keepdims=True) + NORM_EPS)
    o_ref[...] = yn * fg_ref[...]


def _post(x2d, of, ob, cv, proj2d, attn, gn_w, gn_b, w_br, w_ba, w_out, final_g, tm=POST_ROWS):
    n, d = x2d.shape
    row = lambda w, base: pl.BlockSpec((tm, w), lambda i: (i, base // w))
    const = lambda shape: pl.BlockSpec(shape, lambda i: (0, 0))
    return pl.pallas_call(
        _post_kernel,
        grid=(n // tm,),
        in_specs=[row(d, 0), row(R_WIDTH, 0), row(R_WIDTH, 0), row(R_WIDTH, 0),
                  row(R_WIDTH, COL_ZR), row(A_WIDTH, 0), row(A_WIDTH, COL_ZA),
                  row(d, COL_GR), row(d, COL_GA),
                  const((1, R_WIDTH)), const((1, R_WIDTH)),
                  const((R_WIDTH, d)), const((A_WIDTH, d)), const((d, d)), const((1, d))],
        out_specs=row(d, 0),
        out_shape=jax.ShapeDtypeStruct((n, d), F32),
        compiler_params=pltpu.CompilerParams(dimension_semantics=("arbitrary",),
                                             vmem_limit_bytes=VMEM_LIMIT),
        name="merge_out",
    )(x2d, of, ob, cv, proj2d, attn, proj2d, proj2d, proj2d,
      gn_w.reshape(1, -1), gn_b.reshape(1, -1), w_br, w_ba, w_out, final_g.reshape(1, -1))


def _column_segments():
    shift_w = 3 * R_WIDTH + 4 * LORA
    o_wd, o_ad = 3 * R_WIDTH, 3 * R_WIDTH + 2 * LORA
    o_zr = shift_w
    o_q = o_zr + R_WIDTH
    o_k = o_q + A_WIDTH
    o_v = o_k + A_KV_WIDTH
    o_za = o_v + A_KV_WIDTH
    o_g = o_za + A_WIDTH
    segs = [(0, 3 * R_WIDTH), (o_zr, R_WIDTH), (o_q, A_WIDTH), (o_za, A_WIDTH), (o_g, 2 * D_MODEL),
            (o_k, A_KV_WIDTH), (o_v, A_KV_WIDTH), (o_wd, 2 * LORA), (o_ad, 2 * LORA)]
    assert sum(w for _, w in segs) == D_IN
    return segs, shift_w


def _permute_columns(a, segs):
    return jnp.concatenate([a[..., s:s + w] for s, w in segs], axis=-1)


def _pack_lora(up):
    z = jnp.zeros((LORA, PAIRS, LANES), up.dtype)
    top = jnp.concatenate([up[0].reshape(LORA, PAIRS, LANES), z], axis=2)
    bot = jnp.concatenate([z, up[1].reshape(LORA, PAIRS, LANES)], axis=2)
    return jnp.concatenate([top, bot], axis=0).reshape(2 * LORA, 2 * R_WIDTH)


def _pack_dirs(v):
    return jnp.stack([v[0].reshape(PAIRS, LANES), v[1].reshape(PAIRS, LANES)], axis=1).reshape(1, -1)


def _layer(x, norm_g, w_in, shift_mu, w0, w_up, a0, a_up, k_k, k_a, r_k, gn_w, gn_b,
           q_norm_g, k_norm_g, w_branch_rwkv, w_branch_attn, w_out, out_g):
    b, t, d = x.shape
    assert t % (CHUNK * WKV_CHUNKS_PER_STEP) == 0 and t % GRID_W == 0 and d == D_MODEL and w_in.shape[1] == D_IN
    segs, shift_w = _column_segments()
    pad_cols = lambda a: jnp.pad(a, ((0, 0), (0, D_PROJ - D_IN)))
    w_perm = pad_cols(_permute_columns(w_in.astype(BF16), segs))
    mu_full = jnp.concatenate([shift_mu, jnp.zeros((2, D_IN - shift_w), F32)], axis=1)
    mu_perm = pad_cols(_permute_columns(mu_full, segs))
    g_perm = jnp.zeros((1, D_PROJ), F32)
    g_perm = g_perm.at[0, COL_Q:COL_Q + A_WIDTH].set(jnp.tile(q_norm_g * (HEAD_DIM ** -0.5 * LOG2_E), A_Q_HEADS))
    g_perm = g_perm.at[0, COL_K:COL_K + A_KV_WIDTH].set(jnp.tile(k_norm_g, A_KV_HEADS))

    x2d = x.reshape(b * t, d)
    h = _rmsnorm(x2d, norm_g, BF16).reshape(b, t, d)
    proj, lora = _inproj(h, w_perm, mu_perm, g_perm)
    kk, ke0, ke1, bb0, bb1, cin0, cin1, cv = _prep(
        proj, lora, _pack_dirs(w0), _pack_lora(w_up), _pack_dirs(a0), _pack_lora(a_up),
        k_k.reshape(1, -1), k_a.reshape(1, -1), r_k.reshape(1, -1))
    of, ob = _wkv(proj, kk, ke0, ke1, bb0, bb1, cin0, cin1)
    attn = _attention(proj)
    n = b * t
    out = _post(x2d, of.reshape(n, -1), ob.reshape(n, -1), cv.reshape(n, -1), proj.reshape(n, D_PROJ),
                attn.reshape(n, -1), gn_w, gn_b, w_branch_rwkv.astype(BF16), w_branch_attn.astype(BF16),
                w_out.astype(BF16), out_g)
    return out.reshape(b, t, d)


def kernel(x, norm_g, w_in, shift_mu, w0, w_up, a0, a_up, k_k, k_a, r_k, gn_w, gn_b, q_norm_g, k_norm_g,
           w_branch_rwkv, w_branch_attn, w_out, final_norm_g):
    assert norm_g.shape[0] == 1, "single-layer block"
    return _layer(x, norm_g[0], w_in[0], shift_mu[0], w0[0], w_up[0], a0[0], a_up[0], k_k[0], k_a[0],
                  r_k[0], gn_w[0], gn_b[0], q_norm_g[0], k_norm_g[0], w_branch_rwkv[0],
                  w_branch_attn[0], w_out[0], final_norm_g)
```

```python
import functools
import math

import jax
import jax.numpy as jnp
import numpy as np
from jax import lax
from jax.experimental import pallas as pl
from jax.experimental.pallas import tpu as pltpu

F32 = jnp.float32
BF16 = jnp.bfloat16

HEAD_DIM = 64
R_HEADS = 16
R_WIDTH = R_HEADS * HEAD_DIM
LORA = 64
A_Q_HEADS = 16
A_KV_HEADS = 4
A_GROUP = A_Q_HEADS // A_KV_HEADS
A_WIDTH = A_Q_HEADS * HEAD_DIM
A_KV_WIDTH = A_KV_HEADS * HEAD_DIM
GRID_W = 64
ROPE_THETA = 10000.0
NORM_EPS = 1e-6
GN_EPS = 64e-5
KK_EPS = 1e-24
DECAY_SCALE = math.exp(-0.5)

D_MODEL = 2048
LOG2_E = math.log2(math.e)

LANES = 128
SUBLANES = 8
BF16_SUBLANES = 16
MXU_WIDTH = 256
VMEM_LIMIT = 56 * 1024 * 1024

NORM_ROWS = 1024
PROJ_TN = 2 * MXU_WIDTH
PROJ_HALF = MXU_WIDTH
PROJ_PLAIN_ROWS = 2048
PROJ_SHIFT_ROWS = 1024
PROJ_ROPE_ROWS = 512
PROJ_SHIFT_PAD = SUBLANES
PREP_ROWS = 2048
PREP_WIDTH = MXU_WIDTH
PREP_ROW_BANDS = 2
CHUNK = 64
PAIRS = R_WIDTH // LANES
WKV_PAIRS_PER_STEP = 8
WKV_CHUNKS_PER_STEP = 2
ATTN_QUERIES = 256
ATTN_KEY_BLOCK = 128
ATTN_QUERY_TILE = MXU_WIDTH
ATTN_SCORE_LOOKAHEAD = 8
ATTN_VT_ROWS = HEAD_DIM + BF16_SUBLANES
POST_ROWS = 256
POST_GN_WIDTH = MXU_WIDTH

COL_XR, COL_XK, COL_XV = 0, R_WIDTH, 2 * R_WIDTH
COL_ZR = 3 * R_WIDTH
COL_Q = COL_ZR + R_WIDTH
COL_ZA = COL_Q + A_WIDTH
COL_GR = COL_ZA + A_WIDTH
COL_GA = COL_GR + D_MODEL
COL_K = COL_GA + D_MODEL
COL_V = COL_K + A_KV_WIDTH
COL_WD = COL_V + A_KV_WIDTH
COL_AD = COL_WD + 2 * LORA
D_IN = COL_AD + 2 * LORA
D_PROJ = -(-D_IN // PROJ_TN) * PROJ_TN


def _bdot(a, b):
    return jnp.dot(a.astype(BF16), b.astype(BF16), preferred_element_type=F32)


def _bdot_nt(a, b):
    return lax.dot_general(a.astype(BF16), b.astype(BF16), (((1,), (1,)), ((), ())),
                           preferred_element_type=F32)


def _bdot_tn(a, b):
    return lax.dot_general(a.astype(BF16), b.astype(BF16), (((0,), (0,)), ((), ())),
                           preferred_element_type=F32)


def _split2(x):
    hi = x.astype(BF16)
    lo = (x - hi.astype(F32)).astype(BF16)
    return hi, lo


def _dot_exact_rhs(a_bf16, x):
    hi, lo = _split2(x)
    d = lambda y: jnp.dot(a_bf16, y, preferred_element_type=F32)
    return d(hi) + d(lo)


def _dot_exact_lhs(x, b_bf16):
    hi, lo = _split2(x)
    d = lambda y: jnp.dot(y, b_bf16, preferred_element_type=F32)
    return d(hi) + d(lo)


def _dot_f32(x, w):
    xh = x.astype(BF16)
    xl = (x - xh.astype(F32)).astype(BF16)
    wh = w.astype(BF16)
    wl = (w - wh.astype(F32)).astype(BF16)
    d = lambda p, q: jnp.dot(p, q, preferred_element_type=F32)
    return d(xh, wh) + d(xl, wh) + d(xh, wl)


def _head_ones(width=LANES):
    r = lax.broadcasted_iota(jnp.int32, (width, width), 0) // HEAD_DIM
    c = lax.broadcasted_iota(jnp.int32, (width, width), 1) // HEAD_DIM
    return jnp.where(r == c, 1.0, 0.0).astype(BF16)


def _rmsnorm_kernel(x_ref, g_ref, o_ref):
    x = x_ref[...]
    y = x * lax.rsqrt(jnp.mean(x * x, axis=-1, keepdims=True) + NORM_EPS)
    o_ref[...] = (y * g_ref[...]).astype(o_ref.dtype)


def _rmsnorm(x2d, g, out_dtype, tm=NORM_ROWS):
    n, d = x2d.shape
    return pl.pallas_call(
        _rmsnorm_kernel,
        grid=(n // tm,),
        in_specs=[pl.BlockSpec((tm, d), lambda i: (i, 0)),
                  pl.BlockSpec((1, d), lambda i: (0, 0))],
        out_specs=pl.BlockSpec((tm, d), lambda i: (i, 0)),
        out_shape=jax.ShapeDtypeStruct((n, d), out_dtype),
        compiler_params=pltpu.CompilerParams(dimension_semantics=("arbitrary",),
                                             vmem_limit_bytes=VMEM_LIMIT),
        name="rmsnorm",
    )(x2d, g.reshape(1, d))


def _rope_tables(t):
    rows = t // GRID_W
    row = np.repeat(np.arange(rows, dtype=np.float32), GRID_W)
    colv = np.tile(np.arange(GRID_W, dtype=np.float32), rows)
    axis_dim = HEAD_DIM // 2
    freqs = jnp.asarray(ROPE_THETA, F32) ** (-jnp.arange(0, axis_dim, 2, dtype=F32) / axis_dim)
    ang = jnp.concatenate([jnp.asarray(row)[:, None] * freqs, jnp.asarray(colv)[:, None] * freqs], axis=-1)
    reps = PROJ_HALF // HEAD_DIM
    cos = jnp.tile(jnp.repeat(jnp.cos(ang), 2, axis=-1), (1, reps))
    sin = jnp.tile(jnp.repeat(jnp.sin(ang), 2, axis=-1), (1, reps))
    return cos, sin


def _inproj_tile_kinds():
    def kind(col):
        if col < 3 * R_WIDTH or COL_WD <= col < COL_WD + PROJ_HALF:
            return "shift"
        if COL_Q <= col < COL_Q + A_WIDTH or COL_K <= col < COL_K + A_KV_WIDTH:
            return "rope"
        return "plain"
    return [(kind(j * PROJ_TN), kind(j * PROJ_TN + PROJ_HALF)) for j in range(D_PROJ // PROJ_TN)]


def _inproj_kernel(h_ref, w_ref, mu_ref, g_ref, cos_ref, sin_ref, o_ref, lora_ref, acc_ref, *, kinds):
    j = pl.program_id(1)
    t = h_ref.shape[1]
    pad = PROJ_SHIFT_PAD
    hw = PROJ_HALF

    def run(pair, is_lora):
        rc = PROJ_ROPE_ROWS if "rope" in pair else PROJ_SHIFT_ROWS if "shift" in pair else PROJ_PLAIN_ROWS
        rc = min(rc, t)
        nchunk = t // rc

        def mm(c):
            return jnp.dot(h_ref[0, c * rc:(c + 1) * rc, :], w_ref[...], preferred_element_type=F32)

        def emit_plain(c, cols, acc):
            o_ref[0, c * rc:(c + 1) * rc, cols] = acc.astype(o_ref.dtype)

        def emit_rope(c, cols, acc):
            rows = slice(c * rc, (c + 1) * rc)
            ss = _dot_exact_lhs(acc * acc, _head_ones(hw)) * (1.0 / HEAD_DIM)
            y = acc * lax.rsqrt(ss + NORM_EPS) * g_ref[:, cols]
            lane = lax.broadcasted_iota(jnp.int32, (rc, hw), 1)
            nxt = pltpu.roll(y, hw - 1, axis=1)
            prv = pltpu.roll(y, 1, axis=1)
            o_ref[0, rows, cols] = (y * cos_ref[rows, :] + jnp.where((lane % 2) == 0, -nxt, prv)
                                    * sin_ref[rows, :]).astype(o_ref.dtype)

        def emit_shift(c, cols):
            lo = pad + c * rc
            mu = mu_ref[:, cols]
            cur = acc_ref[lo:lo + rc, cols]
            prev = acc_ref[lo - 1:lo - 1 + rc, cols]
            nxt = acc_ref[lo + 1:lo + 1 + rc, cols]
            y = cur + mu[0:1] * (prev - cur) + mu[1:2] * (nxt - cur)
            o_ref[0, c * rc:(c + 1) * rc, cols] = y.astype(o_ref.dtype)
            if is_lora:
                lora_ref[0, c * rc:(c + 1) * rc, :] = y

        halves = [(slice(i * hw, (i + 1) * hw), k) for i, k in enumerate(pair)]
        if "shift" in pair:
            acc_ref[0:pad, :] = jnp.zeros((pad, PROJ_TN), F32)
            acc_ref[pad + t:2 * pad + t, :] = jnp.zeros((pad, PROJ_TN), F32)
        ahead = mm(0)
        for c in range(nchunk):
            acc, ahead = ahead, (mm(c + 1) if c + 1 < nchunk else None)
            for cols, k in halves:
                if k == "shift":
                    acc_ref[pad + c * rc:pad + (c + 1) * rc, cols] = acc[:, cols]
                    if c > 0:
                        emit_shift(c - 1, cols)
                elif k == "rope":
                    emit_rope(c, cols, acc[:, cols])
                else:
                    emit_plain(c, cols, acc[:, cols])
        for cols, k in halves:
            if k == "shift":
                emit_shift(nchunk - 1, cols)

    lora_tile = COL_WD // PROJ_TN
    for pair in sorted(set(kinds)):
        tiles = [i for i, k in enumerate(kinds) if k == pair and i != lora_tile]
        if tiles:
            cond = functools.reduce(jnp.logical_or, [j == i for i in tiles])
            pl.when(cond)(functools.partial(run, pair, False))
    pl.when(j == lora_tile)(functools.partial(run, kinds[lora_tile], True))


def _inproj(h, w_perm, mu_perm, g_perm):
    b, t, d = h.shape
    nj = D_PROJ // PROJ_TN
    cos, sin = _rope_tables(t)
    kern = functools.partial(_inproj_kernel, kinds=_inproj_tile_kinds())
    return pl.pallas_call(
        kern,
        grid=(b, nj),
        in_specs=[pl.BlockSpec((1, t, d), lambda i, j: (i, 0, 0)),
                  pl.BlockSpec((d, PROJ_TN), lambda i, j: (0, j)),
                  pl.BlockSpec((2, PROJ_TN), lambda i, j: (0, j)),
                  pl.BlockSpec((1, PROJ_TN), lambda i, j: (0, j)),
                  pl.BlockSpec((t, PROJ_HALF), lambda i, j: (0, 0)),
                  pl.BlockSpec((t, PROJ_HALF), lambda i, j: (0, 0))],
        out_specs=[pl.BlockSpec((1, t, PROJ_TN), lambda i, j: (i, 0, j)),
                   pl.BlockSpec((1, t, PROJ_HALF), lambda i, j: (i, 0, 0))],
        out_shape=[jax.ShapeDtypeStruct((b, t, D_PROJ), BF16),
                   jax.ShapeDtypeStruct((b, t, PROJ_HALF), F32)],
        scratch_shapes=[pltpu.VMEM((t + 2 * PROJ_SHIFT_PAD, PROJ_TN), F32)],
        compiler_params=pltpu.CompilerParams(dimension_semantics=("arbitrary", "arbitrary"),
                                             vmem_limit_bytes=VMEM_LIMIT),
        name="inproj",
    )(h, w_perm, mu_perm, g_perm, cos, sin)


def _prep_rows(rows, refs, consts):
    (xr_ref, xk_ref, xv_ref, lora_ref, w0_ref, wup_ref, a0_ref, aup_ref, kkg_ref, ka_ref, rk_ref,
     kk_ref, ke_refs, bb_refs, cin_refs, cv_ref) = refs
    ones, tri = consts
    L = CHUNK
    npair = PREP_WIDTH // LANES
    r = xr_ref[0, rows, :].astype(F32)
    k = xk_ref[0, rows, :].astype(F32)
    v = xv_ref[0, rows, :].astype(F32)
    lora = lora_ref[0, rows, :]
    wd = jnp.tanh(lora[:, :LANES])
    ad = lora[:, LANES:]
    k_a = ka_ref[...]
    kkn = k * kkg_ref[...]
    ss = _dot_exact_lhs(kkn * kkn, ones)
    w_raw = w0_ref[...] + _dot_f32(wd, wup_ref[...])
    a_raw = a0_ref[...] + _bdot(ad, aup_ref[...])
    yield
    kk = kkn * lax.rsqrt(jnp.maximum(ss, KK_EPS))
    kk_ref[0, rows, :] = kk.astype(kk_ref.dtype)
    lw_all = -DECAY_SCALE * jax.nn.sigmoid(w_raw)
    a_all = jax.nn.sigmoid(a_raw)
    nchunk = k.shape[0] // L
    row0 = rows.start
    ke_sum = None
    cins = []
    for d in range(2):
        pick = lambda z: jnp.concatenate(
            [z[:, (2 * p + d) * LANES:(2 * p + d + 1) * LANES] for p in range(npair)], axis=1)
        lw, a = pick(lw_all), pick(a_all)
        wide = jnp.concatenate([lw[c * L:(c + 1) * L, :] for c in range(nchunk)], axis=1)
        cins.append(_dot_exact_rhs(tri[d], wide))
        bb_refs[d][0, rows, :] = (kk * a).astype(bb_refs[d].dtype)
        ke = k * (1.0 + (a - 1.0) * k_a)
        ke_refs[d][0, rows, :] = ke.astype(ke_refs[d].dtype)
        ke_sum = ke if ke_sum is None else ke_sum + ke
    coef = _dot_exact_lhs(r * ke_sum * rk_ref[...], ones)
    yield
    for d in range(2):
        for c in range(nchunk):
            cin_refs[d][0, row0 + c * L:row0 + (c + 1) * L, :] = cins[d][:, c * PREP_WIDTH:(c + 1) * PREP_WIDTH]
    cv_ref[0, rows, :] = (coef * v).astype(cv_ref.dtype)


def _prep_kernel(xr_ref, xk_ref, xv_ref, lora_ref, w0_ref, wup_ref, a0_ref, aup_ref,
                 kkg_ref, ka_ref, rk_ref,
                 kk_ref, ke0_ref, ke1_ref, bb0_ref, bb1_ref, cin0_ref, cin1_ref, cv_ref):
    L = CHUNK
    tt_ = lax.broadcasted_iota(jnp.int32, (L, L), 0)
    ss_ = lax.broadcasted_iota(jnp.int32, (L, L), 1)
    tri = (jnp.where(tt_ >= ss_, 1.0, 0.0).astype(BF16), jnp.where(tt_ <= ss_, 1.0, 0.0).astype(BF16))
    consts = (_head_ones(PREP_WIDTH), tri)
    refs = (xr_ref, xk_ref, xv_ref, lora_ref, w0_ref, wup_ref, a0_ref, aup_ref, kkg_ref, ka_ref, rk_ref,
            kk_ref, (ke0_ref, ke1_ref), (bb0_ref, bb1_ref), (cin0_ref, cin1_ref), cv_ref)
    tt = xr_ref.shape[1]
    band = tt // PREP_ROW_BANDS
    _interleave([_prep_rows(slice(i * band, (i + 1) * band), refs, consts) for i in range(PREP_ROW_BANDS)])


def _prep(proj, lora, w0, wup_pad, a0, aup_pad, k_k, k_a, r_k, tt=PREP_ROWS):
    b, t, _ = proj.shape
    tt = min(tt, t)
    w = PREP_WIDTH
    col = lambda base: (lambda i, s, p: (i, s, base // w + p))
    vec = lambda i, s, p: (0, p)
    tile = pl.BlockSpec((1, tt, w), lambda i, s, p: (i, s, p))
    sd = lambda dt: jax.ShapeDtypeStruct((b, t, R_WIDTH), dt)
    return pl.pallas_call(
        _prep_kernel,
        grid=(b, t // tt, R_WIDTH // w),
        in_specs=[pl.BlockSpec((1, tt, w), col(COL_XR)),
                  pl.BlockSpec((1, tt, w), col(COL_XK)),
                  pl.BlockSpec((1, tt, w), col(COL_XV)),
                  pl.BlockSpec((1, tt, 2 * LANES), lambda i, s, p: (i, s, 0)),
                  pl.BlockSpec((1, 2 * w), vec),
                  pl.BlockSpec((LANES, 2 * w), vec),
                  pl.BlockSpec((1, 2 * w), vec),
                  pl.BlockSpec((LANES, 2 * w), vec),
                  pl.BlockSpec((1, w), vec),
                  pl.BlockSpec((1, w), vec),
                  pl.BlockSpec((1, w), vec)],
        out_specs=[tile] * 8,
        out_shape=[sd(BF16), sd(BF16), sd(BF16), sd(BF16), sd(BF16), sd(F32), sd(F32), sd(BF16)],
        compiler_params=pltpu.CompilerParams(
            dimension_semantics=("arbitrary", "arbitrary", "arbitrary"),
            vmem_limit_bytes=VMEM_LIMIT),
        name="rwkv_prep",
    )(proj, proj, proj, lora, w0, wup_pad, a0, aup_pad, k_k, k_a, r_k)


def _wkv_consts(reverse):
    L = CHUNK
    t = lax.broadcasted_iota(jnp.int32, (L, 2 * L), 0)
    s = lax.broadcasted_iota(jnp.int32, (L, 2 * L), 1) % L
    if reverse:
        strict, incl = t < s, t <= s
    else:
        strict, incl = t > s, t >= s
    eye2 = jnp.where(t == s, 1.0, 0.0).astype(F32)
    rr = lax.broadcasted_iota(jnp.int32, (LANES, LANES), 0)
    cc = lax.broadcasted_iota(jnp.int32, (LANES, LANES), 1)
    bd = (rr // HEAD_DIM) == (cc // HEAD_DIM)
    eye = rr == cc
    row = lax.broadcasted_iota(jnp.int32, (L, LANES), 0)
    first = row == (L - 1 if reverse else 0)
    return strict, incl, eye2, bd, eye, first


def _wkv_chunk(load, get_state, finish, consts, reverse):
    L = CHUNK
    strict, incl, eye2, bd, eye, first = consts

    def bd2(x):
        return jnp.where(bd, jnp.concatenate([x, x], axis=0), 0.0)

    cin, kk, b, ke, v, r = load()
    a = -kk
    cex = jnp.where(first, 0.0, pltpu.roll(cin, L - 1 if reverse else 1, axis=0))
    cend = cin[0:1, :] if reverse else cin[L - 1:L, :]
    m = 0.5 * cend
    e_in = jnp.exp(cin - m)
    e_out = jnp.exp(m - cin)
    em = jnp.exp(m)
    at = a * jnp.exp(cex - m)
    rt = r * e_in
    bt = b * e_out
    kt = ke * e_out
    ap, rp, bh, kh = at * em, rt * em, bt * em, kt * em

    ar = jnp.concatenate([at, rt], axis=0)
    g = _bdot_nt(ar, jnp.concatenate([bd2(bt), bd2(kt)], axis=0))
    yield
    a_ab = jnp.where(strict, g[:L, :LANES], 0.0)
    a_ak = jnp.where(strict, g[:L, LANES:], 0.0)
    p_rbk = jnp.concatenate([jnp.where(incl, g[L:, :LANES], 0.0),
                             jnp.where(incl, g[L:, LANES:], 0.0)], axis=1)

    n = a_ab
    p = eye2 + n
    n = _bdot(n, bd2(n))
    vb = bd2(v)
    x = _bdot(a_ak, vb)
    yield
    steps = int(math.log2(L)) - 1
    for i in range(steps):
        nb = bd2(n)
        if i < steps - 1:
            np_ = _bdot(jnp.concatenate([n, p], axis=0), nb)
            n = np_[:L]
            p = p + np_[L:]
        else:
            p = p + _bdot(p, nb)
        yield
    tinv = p

    wu = _bdot(tinv, jnp.concatenate([bd2(ap), bd2(x)], axis=1))
    yield
    w, u0 = wu[:, :LANES], wu[:, LANES:]
    zero = jnp.zeros((LANES, LANES), F32)
    rhs = jnp.concatenate([jnp.concatenate([bd2(w), bd2(u0)], axis=1),
                           jnp.concatenate([zero, vb], axis=1)], axis=0)
    pwo = _bdot(p_rbk, rhs)
    q = rp + pwo[:, :LANES]
    o0 = pwo[:, LANES:]
    zl = jnp.zeros((L, LANES), F32)
    mc = _bdot_tn(jnp.concatenate([bh, kh], axis=0),
                  jnp.concatenate([wu, jnp.concatenate([zl, v], axis=1)], axis=0))
    m_off = jnp.where(bd, mc[:, :LANES], 0.0)
    c = jnp.where(bd, mc[:, LANES:], 0.0)
    decay_col = jnp.sum(jnp.where(eye, jnp.exp(cend), 0.0), axis=1, keepdims=True)
    yield
    h = get_state()
    qm = _bdot(jnp.concatenate([q, m_off], axis=0), h)
    finish(qm[:L] + o0, decay_col * h + qm[L:] + c)


def _interleave(gens):
    results = [None] * len(gens)
    live = list(range(len(gens)))
    while live:
        for i in list(live):
            try:
                next(gens[i])
            except StopIteration as stop:
                results[i] = stop.value
                live.remove(i)
    return results


def _wkv_kernel(rf_ref, vf_ref, kkf_ref, kef_ref, bbf_ref, cinf_ref,
                rb_ref, vb_ref, kkb_ref, keb_ref, bbb_ref, cinb_ref,
                of_ref, ob_ref, h_ref):
    @pl.when(pl.program_id(2) == 0)
    def _():
        h_ref[...] = jnp.zeros_like(h_ref)

    dirs = ((cinf_ref, kkf_ref, bbf_ref, kef_ref, vf_ref, rf_ref, of_ref),
            (cinb_ref, kkb_ref, bbb_ref, keb_ref, vb_ref, rb_ref, ob_ref))
    consts = (_wkv_consts(False), _wkv_consts(True))
    L = CHUNK
    state = {}
    gens = []
    for i in range(WKV_PAIRS_PER_STEP):
        sl = slice(i * LANES, (i + 1) * LANES)
        for d in range(2):
            refs = dirs[d]
            order = range(WKV_CHUNKS_PER_STEP) if d == 0 else reversed(range(WKV_CHUNKS_PER_STEP))
            for s in order:
                rows = slice(s * L, (s + 1) * L)

                def load(refs=refs, rows=rows, sl=sl):
                    return tuple(ref[0, rows, sl].astype(F32) for ref in refs[:6])

                def get_state(d=d, i=i):
                    return state[d, i] if (d, i) in state else h_ref[d, i]

                def finish(out, h_new, o_ref=refs[6], rows=rows, sl=sl, d=d, i=i):
                    o_ref[0, rows, sl] = out.astype(o_ref.dtype)
                    state[d, i] = h_new

                gens.append(_wkv_chunk(load, get_state, finish, consts[d], reverse=(d == 1)))
    _interleave(gens)
    for (d, i), h_new in state.items():
        h_ref[d, i] = h_new


def _wkv(proj, kk, ke0, ke1, bb0, bb1, cin0, cin1):
    b, t, _ = proj.shape
    nc = t // (CHUNK * WKV_CHUNKS_PER_STEP)
    L = CHUNK * WKV_CHUNKS_PER_STEP
    w = WKV_PAIRS_PER_STEP * LANES
    fwd = lambda base: (lambda i, p, c: (i, c, base // w + p))
    bwd = lambda base: (lambda i, p, c: (i, nc - 1 - c, base // w + p))
    blk = lambda im: pl.BlockSpec((1, L, w), im)
    out_sd = jax.ShapeDtypeStruct((b, t, R_WIDTH), BF16)
    return pl.pallas_call(
        _wkv_kernel,
        grid=(b, PAIRS // WKV_PAIRS_PER_STEP, nc),
        in_specs=[blk(fwd(COL_XR)), blk(fwd(COL_XV)), blk(fwd(0)), blk(fwd(0)), blk(fwd(0)), blk(fwd(0)),
                  blk(bwd(COL_XR)), blk(bwd(COL_XV)), blk(bwd(0)), blk(bwd(0)), blk(bwd(0)), blk(bwd(0))],
        out_specs=[blk(fwd(0)), blk(bwd(0))],
        out_shape=[out_sd, out_sd],
        scratch_shapes=[pltpu.VMEM((2, WKV_PAIRS_PER_STEP, LANES, LANES), F32)],
        compiler_params=pltpu.CompilerParams(
            dimension_semantics=("arbitrary", "arbitrary", "arbitrary"),
            vmem_limit_bytes=VMEM_LIMIT),
        name="wkv",
    )(proj, proj, kk, ke0, bb0, cin0, proj, proj, kk, ke1, bb1, cin1)


def _attn_kernel(q_ref, k_ref, v_ref, o_ref, kh_ref, vt_ref, *, tq):
    t = k_ref.shape[1]
    kb = ATTN_KEY_BLOCK
    heads_per_tile = ATTN_QUERY_TILE // tq
    ntile = A_GROUP // heads_per_tile

    @pl.when(pl.program_id(1) == 0)
    def _():
        r = lax.broadcasted_iota(jnp.int32, (A_KV_WIDTH, HEAD_DIM), 0)
        c = lax.broadcasted_iota(jnp.int32, (A_KV_WIDTH, HEAD_DIM), 1)
        kf = k_ref[0]
        vt = v_ref[0].astype(F32).T.astype(BF16)
        ones = jnp.ones((ATTN_VT_ROWS - HEAD_DIM, t), BF16)
        for hk in range(A_KV_HEADS):
            sel = jnp.where(r == hk * HEAD_DIM + c, 1.0, 0.0).astype(BF16)
            kh_ref[hk] = jnp.dot(kf, sel, preferred_element_type=F32).astype(BF16)
            vt_ref[hk] = jnp.concatenate([vt[hk * HEAD_DIM:(hk + 1) * HEAD_DIM, :], ones], axis=0)

    qt = q_ref[0].astype(F32).T.astype(BF16)

    def q_tile(hk, c):
        h0 = hk * A_GROUP + c * heads_per_tile
        return jnp.concatenate([qt[(h0 + e) * HEAD_DIM:(h0 + e + 1) * HEAD_DIM, :]
                                for e in range(heads_per_tile)], axis=1)

    def scores(hk, c, j):
        return jnp.dot(kh_ref[hk, j * kb:(j + 1) * kb, :], q_tile(hk, c), preferred_element_type=F32)

    tasks = [(hk, c, j) for hk in range(A_KV_HEADS) for j in range(t // kb) for c in range(ntile)]
    state = {}
    done = {}
    ahead = [scores(*task) for task in tasks[:ATTN_SCORE_LOOKAHEAD]]
    for i, (hk, c, j) in enumerate(tasks):
        s = ahead.pop(0)
        if i + ATTN_SCORE_LOOKAHEAD < len(tasks):
            ahead.append(scores(*tasks[i + ATTN_SCORE_LOOKAHEAD]))
        if j == 0:
            state[hk, c] = (jnp.full((1, ATTN_QUERY_TILE), -1e30, F32),
                            jnp.zeros((ATTN_VT_ROWS, ATTN_QUERY_TILE), F32))
        m, acc = state[hk, c]
        m_new = jnp.maximum(m, jnp.max(s, axis=0, keepdims=True))
        p = jnp.exp2(s - m_new).astype(BF16)
        acc = acc * jnp.exp2(m - m_new) + jnp.dot(vt_ref[hk, :, j * kb:(j + 1) * kb], p,
                                                  preferred_element_type=F32)
        state[hk, c] = (m_new, acc)
        if j == t // kb - 1:
            ot = acc[:HEAD_DIM] / acc[HEAD_DIM:HEAD_DIM + 1]
            for e in range(heads_per_tile):
                done[hk * A_GROUP + c * heads_per_tile + e] = ot[:, e * tq:(e + 1) * tq]
    rows = [done[h] for h in range(A_Q_HEADS)]
    o_ref[0] = jnp.concatenate(rows, axis=0).T.astype(o_ref.dtype)


def _attention(proj, tq=ATTN_QUERIES):
    b, t, _ = proj.shape
    kern = functools.partial(_attn_kernel, tq=tq)
    return pl.pallas_call(
        kern,
        grid=(b, t // tq),
        in_specs=[pl.BlockSpec((1, tq, A_WIDTH), lambda i, j: (i, j, COL_Q // A_WIDTH)),
                  pl.BlockSpec((1, t, A_KV_WIDTH), lambda i, j: (i, 0, COL_K // A_KV_WIDTH)),
                  pl.BlockSpec((1, t, A_KV_WIDTH), lambda i, j: (i, 0, COL_V // A_KV_WIDTH))],
        out_specs=pl.BlockSpec((1, tq, A_WIDTH), lambda i, j: (i, j, 0)),
        out_shape=jax.ShapeDtypeStruct((b, t, A_WIDTH), BF16),
        scratch_shapes=[pltpu.VMEM((A_KV_HEADS, t, HEAD_DIM), BF16),
                        pltpu.VMEM((A_KV_HEADS, ATTN_VT_ROWS, t), BF16)],
        compiler_params=pltpu.CompilerParams(dimension_semantics=("arbitrary", "arbitrary"),
                                             vmem_limit_bytes=VMEM_LIMIT),
        name="attention",
    )(proj, proj, proj)


def _post_kernel(x_ref, of_ref, ob_ref, cv_ref, zr_ref, at_ref, za_ref, gr_ref, ga_ref,
                 gnw_ref, gnb_ref, wbr_ref, wba_ref, wout_ref, fg_ref, o_ref):
    za = za_ref[...].astype(F32)
    o_a = at_ref[...].astype(F32) * (za * jax.nn.sigmoid(za))
    p_a = _bdot(o_a, wba_ref[...])
    gw = POST_GN_WIDTH
    ones = _head_ones(gw)
    wkv = of_ref[...].astype(F32) + ob_ref[...].astype(F32)
    groups = [slice(i * gw, (i + 1) * gw) for i in range(R_WIDTH // gw)]
    mus = [_dot_exact_lhs(wkv[:, sl], ones) * (1.0 / HEAD_DIM) for sl in groups]
    ycs = [wkv[:, sl] - mu for sl, mu in zip(groups, mus)]
    vrs = [_dot_exact_lhs(yc * yc, ones) * (1.0 / HEAD_DIM) for yc in ycs]
    gn = jnp.concatenate([yc * lax.rsqrt(var + GN_EPS) for yc, var in zip(ycs, vrs)], axis=1)
    gn = gn * gnw_ref[...] + gnb_ref[...]
    zr = zr_ref[...].astype(F32)
    o_r = (gn + cv_ref[...].astype(F32)) * (zr * jax.nn.sigmoid(zr))
    p_r = _bdot(o_r, wbr_ref[...])
    merged = (jax.nn.sigmoid(gr_ref[...].astype(F32)) * p_r
              + jax.nn.sigmoid(ga_ref[...].astype(F32)) * p_a)
    y = x_ref[...] + _bdot(merged, wout_ref[...])
    yn = y * lax.rsqrt(jnp.mean(y * y, axis=-1, keepdims=True) + NORM_EPS)
    o_ref[...] = yn * fg_ref[...]


def _post(x2d, of, ob, cv, proj2d, attn, gn_w, gn_b, w_br, w_ba, w_out, final_g, tm=POST_ROWS):
    n, d = x2d.shape
    row = lambda w, base: pl.BlockSpec((tm, w), lambda i: (i, base // w))
    const = lambda shape: pl.BlockSpec(shape, lambda i: (0, 0))
    return pl.pallas_call(
        _post_kernel,
        grid=(n // tm,),
        in_specs=[row(d, 0), row(R_WIDTH, 0), row(R_WIDTH, 0), row(R_WIDTH, 0),
                  row(R_WIDTH, COL_ZR), row(A_WIDTH, 0), row(A_WIDTH, COL_ZA),
                  row(d, COL_GR), row(d, COL_GA),
                  const((1, R_WIDTH)), const((1, R_WIDTH)),
                  const((R_WIDTH, d)), const((A_WIDTH, d)), const((d, d)), const((1, d))],
        out_specs=row(d, 0),
        out_shape=jax.ShapeDtypeStruct((n, d), F32),
        compiler_params=pltpu.CompilerParams(dimension_semantics=("arbitrary",),
                                             vmem_limit_bytes=VMEM_LIMIT),
        name="merge_out",
    )(x2d, of, ob, cv, proj2d, attn, proj2d, proj2d, proj2d,
      gn_w.reshape(1, -1), gn_b.reshape(1, -1), w_br, w_ba, w_out, final_g.reshape(1, -1))


def _column_segments():
    shift_w = 3 * R_WIDTH + 4 * LORA
    o_wd, o_ad = 3 * R_WIDTH, 3 * R_WIDTH + 2 * LORA
    o_zr = shift_w
    o_q = o_zr + R_WIDTH
    o_k = o_q + A_WIDTH
    o_v = o_k + A_KV_WIDTH
    o_za = o_v + A_KV_WIDTH
    o_g = o_za + A_WIDTH
    segs = [(0, 3 * R_WIDTH), (o_zr, R_WIDTH), (o_q, A_WIDTH), (o_za, A_WIDTH), (o_g, 2 * D_MODEL),
            (o_k, A_KV_WIDTH), (o_v, A_KV_WIDTH), (o_wd, 2 * LORA), (o_ad, 2 * LORA)]
    assert sum(w for _, w in segs) == D_IN
    return segs, shift_w


def _permute_columns(a, segs):
    return jnp.concatenate([a[..., s:s + w] for s, w in segs], axis=-1)


def _pack_lora(up):
    z = jnp.zeros((LORA, PAIRS, LANES), up.dtype)
    top = jnp.concatenate([up[0].reshape(LORA, PAIRS, LANES), z], axis=2)
    bot = jnp.concatenate([z, up[1].reshape(LORA, PAIRS, LANES)], axis=2)
    return jnp.concatenate([top, bot], axis=0).reshape(2 * LORA, 2 * R_WIDTH)


def _pack_dirs(v):
    return jnp.stack([v[0].reshape(PAIRS, LANES), v[1].reshape(PAIRS, LANES)], axis=1).reshape(1, -1)


def _layer(x, norm_g, w_in, shift_mu, w0, w_up, a0, a_up, k_k, k_a, r_k, gn_w, gn_b,
           q_norm_g, k_norm_g, w_branch_rwkv, w_branch_attn, w_out, out_g):
    b, t, d = x.shape
    assert t % (CHUNK * WKV_CHUNKS_PER_STEP) == 0 and t % GRID_W == 0 and d == D_MODEL and w_in.shape[1] == D_IN
    segs, shift_w = _column_segments()
    pad_cols = lambda a: jnp.pad(a, ((0, 0), (0, D_PROJ - D_IN)))
    w_perm = pad_cols(_permute_columns(w_in.astype(BF16), segs))
    mu_full = jnp.concatenate([shift_mu, jnp.zeros((2, D_IN - shift_w), F32)], axis=1)
    mu_perm = pad_cols(_permute_columns(mu_full, segs))
    g_perm = jnp.zeros((1, D_PROJ), F32)
    g_perm = g_perm.at[0, COL_Q:COL_Q + A_WIDTH].set(jnp.tile(q_norm_g * (HEAD_DIM ** -0.5 * LOG2_E), A_Q_HEADS))
    g_perm = g_perm.at[0, COL_K:COL_K + A_KV_WIDTH].set(jnp.tile(k_norm_g, A_KV_HEADS))

    x2d = x.reshape(b * t, d)
    h = _rmsnorm(x2d, norm_g, BF16).reshape(b, t, d)
    proj, lora = _inproj(h, w_perm, mu_perm, g_perm)
    kk, ke0, ke1, bb0, bb1, cin0, cin1, cv = _prep(
        proj, lora, _pack_dirs(w0), _pack_lora(w_up), _pack_dirs(a0), _pack_lora(a_up),
        k_k.reshape(1, -1), k_a.reshape(1, -1), r_k.reshape(1, -1))
    of, ob = _wkv(proj, kk, ke0, ke1, bb0, bb1, cin0, cin1)
    attn = _attention(proj)
    n = b * t
    out = _post(x2d, of.reshape(n, -1), ob.reshape(n, -1), cv.reshape(n, -1), proj.reshape(n, D_PROJ),
                attn.reshape(n, -1), gn_w, gn_b, w_branch_rwkv.astype(BF16), w_branch_attn.astype(BF16),
                w_out.astype(BF16), out_g)
    return out.reshape(b, t, d)


def kernel(x, norm_g, w_in, shift_mu, w0, w_up, a0, a_up, k_k, k_a, r_k, gn_w, gn_b, q_norm_g, k_norm_g,
           w_branch_rwkv, w_branch_attn, w_out, final_norm_g):
    assert norm_g.shape[0] == 1, "single-layer block"
    return _layer(x, norm_g[0], w_in[0], shift_mu[0], w0[0], w_up[0], a0[0], a_up[0], k_k[0], k_a[0],
                  r_k[0], gn_w[0], gn_b[0], q_norm_g[0], k_norm_g[0], w_branch_rwkv[0],
                  w_branch_attn[0], w_out[0], final_norm_g)
```

```python
import functools
import math

import jax
import jax.numpy as jnp
import numpy as np
from jax import lax
from jax.experimental import pallas as pl
from jax.experimental.pallas import tpu as pltpu

F32 = jnp.float32
BF16 = jnp.bfloat16

HEAD_DIM = 64
R_HEADS = 16
R_WIDTH = R_HEADS * HEAD_DIM
LORA = 64
A_Q_HEADS = 16
A_KV_HEADS = 4
A_GROUP = A_Q_HEADS // A_KV_HEADS
A_WIDTH = A_Q_HEADS * HEAD_DIM
A_KV_WIDTH = A_KV_HEADS * HEAD_DIM
GRID_W = 64
ROPE_THETA = 10000.0
NORM_EPS = 1e-6
GN_EPS = 64e-5
KK_EPS = 1e-24
DECAY_SCALE = math.exp(-0.5)

D_MODEL = 2048
LOG2_E = math.log2(math.e)

LANES = 128
SUBLANES = 8
BF16_SUBLANES = 16
MXU_WIDTH = 256
VMEM_LIMIT = 56 * 1024 * 1024

NORM_ROWS = 1024
PROJ_TN = 2 * MXU_WIDTH
PROJ_HALF = MXU_WIDTH
PROJ_PLAIN_ROWS = 2048
PROJ_SHIFT_ROWS = 1024
PROJ_ROPE_ROWS = 512
PROJ_SHIFT_PAD = SUBLANES
PREP_ROWS = 2048
PREP_WIDTH = MXU_WIDTH
PREP_ROW_BANDS = 2
CHUNK = 64
PAIRS = R_WIDTH // LANES
WKV_PAIRS_PER_STEP = 8
WKV_CHUNKS_PER_STEP = 4
ATTN_QUERIES = 256
ATTN_KEY_BLOCK = 128
ATTN_QUERY_TILE = MXU_WIDTH
ATTN_SCORE_LOOKAHEAD = 8
ATTN_VT_ROWS = HEAD_DIM + BF16_SUBLANES
POST_ROWS = 256
POST_GN_WIDTH = MXU_WIDTH

COL_XR, COL_XK, COL_XV = 0, R_WIDTH, 2 * R_WIDTH
COL_ZR = 3 * R_WIDTH
COL_Q = COL_ZR + R_WIDTH
COL_ZA = COL_Q + A_WIDTH
COL_GR = COL_ZA + A_WIDTH
COL_GA = COL_GR + D_MODEL
COL_K = COL_GA + D_MODEL
COL_V = COL_K + A_KV_WIDTH
COL_WD = COL_V + A_KV_WIDTH
COL_AD = COL_WD + 2 * LORA
D_IN = COL_AD + 2 * LORA
D_PROJ = -(-D_IN // PROJ_TN) * PROJ_TN


def _bdot(a, b):
    return jnp.dot(a.astype(BF16), b.astype(BF16), preferred_element_type=F32)


def _bdot_nt(a, b):
    return lax.dot_general(a.astype(BF16), b.astype(BF16), (((1,), (1,)), ((), ())),
                           preferred_element_type=F32)


def _bdot_tn(a, b):
    return lax.dot_general(a.astype(BF16), b.astype(BF16), (((0,), (0,)), ((), ())),
                           preferred_element_type=F32)


def _split2(x):
    hi = x.astype(BF16)
    lo = (x - hi.astype(F32)).astype(BF16)
    return hi, lo


def _dot_exact_rhs(a_bf16, x):
    hi, lo = _split2(x)
    d = lambda y: jnp.dot(a_bf16, y, preferred_element_type=F32)
    return d(hi) + d(lo)


def _dot_exact_lhs(x, b_bf16):
    hi, lo = _split2(x)
    d = lambda y: jnp.dot(y, b_bf16, preferred_element_type=F32)
    return d(hi) + d(lo)


def _dot_f32(x, w):
    xh = x.astype(BF16)
    xl = (x - xh.astype(F32)).astype(BF16)
    wh = w.astype(BF16)
    wl = (w - wh.astype(F32)).astype(BF16)
    d = lambda p, q: jnp.dot(p, q, preferred_element_type=F32)
    return d(xh, wh) + d(xl, wh) + d(xh, wl)


def _head_ones(width=LANES):
    r = lax.broadcasted_iota(jnp.int32, (width, width), 0) // HEAD_DIM
    c = lax.broadcasted_iota(jnp.int32, (width, width), 1) // HEAD_DIM
    return jnp.where(r == c, 1.0, 0.0).astype(BF16)


def _rmsnorm_kernel(x_ref, g_ref, o_ref):
    x = x_ref[...]
    y = x * lax.rsqrt(jnp.mean(x * x, axis=-1, keepdims=True) + NORM_EPS)
    o_ref[...] = (y * g_ref[...]).astype(o_ref.dtype)


def _rmsnorm(x2d, g, out_dtype, tm=NORM_ROWS):
    n, d = x2d.shape
    return pl.pallas_call(
        _rmsnorm_kernel,
        grid=(n // tm,),
        in_specs=[pl.BlockSpec((tm, d), lambda i: (i, 0)),
                  pl.BlockSpec((1, d), lambda i: (0, 0))],
        out_specs=pl.BlockSpec((tm, d), lambda i: (i, 0)),
        out_shape=jax.ShapeDtypeStruct((n, d), out_dtype),
        compiler_params=pltpu.CompilerParams(dimension_semantics=("arbitrary",),
                                             vmem_limit_bytes=VMEM_LIMIT),
        name="rmsnorm",
    )(x2d, g.reshape(1, d))


def _rope_tables(t):
    rows = t // GRID_W
    row = np.repeat(np.arange(rows, dtype=np.float32), GRID_W)
    colv = np.tile(np.arange(GRID_W, dtype=np.float32), rows)
    axis_dim = HEAD_DIM // 2
    freqs = jnp.asarray(ROPE_THETA, F32) ** (-jnp.arange(0, axis_dim, 2, dtype=F32) / axis_dim)
    ang = jnp.concatenate([jnp.asarray(row)[:, None] * freqs, jnp.asarray(colv)[:, None] * freqs], axis=-1)
    reps = PROJ_HALF // HEAD_DIM
    cos = jnp.tile(jnp.repeat(jnp.cos(ang), 2, axis=-1), (1, reps))
    sin = jnp.tile(jnp.repeat(jnp.sin(ang), 2, axis=-1), (1, reps))
    return cos, sin


def _inproj_tile_kinds():
    def kind(col):
        if col < 3 * R_WIDTH or COL_WD <= col < COL_WD + PROJ_HALF:
            return "shift"
        if COL_Q <= col < COL_Q + A_WIDTH or COL_K <= col < COL_K + A_KV_WIDTH:
            return "rope"
        return "plain"
    return [(kind(j * PROJ_TN), kind(j * PROJ_TN + PROJ_HALF)) for j in range(D_PROJ // PROJ_TN)]


def _inproj_kernel(h_ref, w_ref, mu_ref, g_ref, cos_ref, sin_ref, o_ref, lora_ref, acc_ref, *, kinds):
    j = pl.program_id(1)
    t = h_ref.shape[1]
    pad = PROJ_SHIFT_PAD
    hw = PROJ_HALF

    def run(pair, is_lora):
        rc = PROJ_ROPE_ROWS if "rope" in pair else PROJ_SHIFT_ROWS if "shift" in pair else PROJ_PLAIN_ROWS
        rc = min(rc, t)
        nchunk = t // rc

        def mm(c):
            return jnp.dot(h_ref[0, c * rc:(c + 1) * rc, :], w_ref[...], preferred_element_type=F32)

        def emit_plain(c, cols, acc):
            o_ref[0, c * rc:(c + 1) * rc, cols] = acc.astype(o_ref.dtype)

        def emit_rope(c, cols, acc):
            rows = slice(c * rc, (c + 1) * rc)
            ss = _dot_exact_lhs(acc * acc, _head_ones(hw)) * (1.0 / HEAD_DIM)
            y = acc * lax.rsqrt(ss + NORM_EPS) * g_ref[:, cols]
            lane = lax.broadcasted_iota(jnp.int32, (rc, hw), 1)
            nxt = pltpu.roll(y, hw - 1, axis=1)
            prv = pltpu.roll(y, 1, axis=1)
            o_ref[0, rows, cols] = (y * cos_ref[rows, :] + jnp.where((lane % 2) == 0, -nxt, prv)
                                    * sin_ref[rows, :]).astype(o_ref.dtype)

        def emit_shift(c, cols):
            lo = pad + c * rc
            mu = mu_ref[:, cols]
            cur = acc_ref[lo:lo + rc, cols]
            prev = acc_ref[lo - 1:lo - 1 + rc, cols]
            nxt = acc_ref[lo + 1:lo + 1 + rc, cols]
            y = cur + mu[0:1] * (prev - cur) + mu[1:2] * (nxt - cur)
            o_ref[0, c * rc:(c + 1) * rc, cols] = y.astype(o_ref.dtype)
            if is_lora:
                lora_ref[0, c * rc:(c + 1) * rc, :] = y

        halves = [(slice(i * hw, (i + 1) * hw), k) for i, k in enumerate(pair)]
        if "shift" in pair:
            acc_ref[0:pad, :] = jnp.zeros((pad, PROJ_TN), F32)
            acc_ref[pad + t:2 * pad + t, :] = jnp.zeros((pad, PROJ_TN), F32)
        ahead = mm(0)
        for c in range(nchunk):
            acc, ahead = ahead, (mm(c + 1) if c + 1 < nchunk else None)
            for cols, k in halves:
                if k == "shift":
                    acc_ref[pad + c * rc:pad + (c + 1) * rc, cols] = acc[:, cols]
                    if c > 0:
                        emit_shift(c - 1, cols)
                elif k == "rope":
                    emit_rope(c, cols, acc[:, cols])
                else:
                    emit_plain(c, cols, acc[:, cols])
        for cols, k in halves:
            if k == "shift":
                emit_shift(nchunk - 1, cols)

    lora_tile = COL_WD // PROJ_TN
    for pair in sorted(set(kinds)):
        tiles = [i for i, k in enumerate(kinds) if k == pair and i != lora_tile]
        if tiles:
            cond = functools.reduce(jnp.logical_or, [j == i for i in tiles])
            pl.when(cond)(functools.partial(run, pair, False))
    pl.when(j == lora_tile)(functools.partial(run, kinds[lora_tile], True))


def _inproj(h, w_perm, mu_perm, g_perm):
    b, t, d = h.shape
    nj = D_PROJ // PROJ_TN
    cos, sin = _rope_tables(t)
    kern = functools.partial(_inproj_kernel, kinds=_inproj_tile_kinds())
    return pl.pallas_call(
        kern,
        grid=(b, nj),
        in_specs=[pl.BlockSpec((1, t, d), lambda i, j: (i, 0, 0)),
                  pl.BlockSpec((d, PROJ_TN), lambda i, j: (0, j)),
                  pl.BlockSpec((2, PROJ_TN), lambda i, j: (0, j)),
                  pl.BlockSpec((1, PROJ_TN), lambda i, j: (0, j)),
                  pl.BlockSpec((t, PROJ_HALF), lambda i, j: (0, 0)),
                  pl.BlockSpec((t, PROJ_HALF), lambda i, j: (0, 0))],
        out_specs=[pl.BlockSpec((1, t, PROJ_TN), lambda i, j: (i, 0, j)),
                   pl.BlockSpec((1, t, PROJ_HALF), lambda i, j: (i, 0, 0))],
        out_shape=[jax.ShapeDtypeStruct((b, t, D_PROJ), BF16),
                   jax.ShapeDtypeStruct((b, t, PROJ_HALF), F32)],
        scratch_shapes=[pltpu.VMEM((t + 2 * PROJ_SHIFT_PAD, PROJ_TN), F32)],
        compiler_params=pltpu.CompilerParams(dimension_semantics=("arbitrary", "arbitrary"),
                                             vmem_limit_bytes=VMEM_LIMIT),
        name="inproj",
    )(h, w_perm, mu_perm, g_perm, cos, sin)


def _prep_rows(rows, refs, consts):
    (xr_ref, xk_ref, xv_ref, lora_ref, w0_ref, wup_ref, a0_ref, aup_ref, kkg_ref, ka_ref, rk_ref,
     kk_ref, ke_refs, bb_refs, cin_refs, cv_ref) = refs
    ones, tri = consts
    L = CHUNK
    npair = PREP_WIDTH // LANES
    r = xr_ref[0, rows, :].astype(F32)
    k = xk_ref[0, rows, :].astype(F32)
    v = xv_ref[0, rows, :].astype(F32)
    lora = lora_ref[0, rows, :]
    wd = jnp.tanh(lora[:, :LANES])
    ad = lora[:, LANES:]
    k_a = ka_ref[...]
    kkn = k * kkg_ref[...]
    ss = _dot_exact_lhs(kkn * kkn, ones)
    w_raw = w0_ref[...] + _dot_f32(wd, wup_ref[...])
    a_raw = a0_ref[...] + _bdot(ad, aup_ref[...])
    yield
    kk = kkn * lax.rsqrt(jnp.maximum(ss, KK_EPS))
    kk_ref[0, rows, :] = kk.astype(kk_ref.dtype)
    lw_all = -DECAY_SCALE * jax.nn.sigmoid(w_raw)
    a_all = jax.nn.sigmoid(a_raw)
    nchunk = k.shape[0] // L
    row0 = rows.start
    ke_sum = None
    cins = []
    for d in range(2):
        pick = lambda z: jnp.concatenate(
            [z[:, (2 * p + d) * LANES:(2 * p + d + 1) * LANES] for p in range(npair)], axis=1)
        lw, a = pick(lw_all), pick(a_all)
        wide = jnp.concatenate([lw[c * L:(c + 1) * L, :] for c in range(nchunk)], axis=1)
        cins.append(_dot_exact_rhs(tri[d], wide))
        bb_refs[d][0, rows, :] = (kk * a).astype(bb_refs[d].dtype)
        ke = k * (1.0 + (a - 1.0) * k_a)
        ke_refs[d][0, rows, :] = ke.astype(ke_refs[d].dtype)
        ke_sum = ke if ke_sum is None else ke_sum + ke
    coef = _dot_exact_lhs(r * ke_sum * rk_ref[...], ones)
    yield
    for d in range(2):
        for c in range(nchunk):
            cin_refs[d][0, row0 + c * L:row0 + (c + 1) * L, :] = cins[d][:, c * PREP_WIDTH:(c + 1) * PREP_WIDTH]
    cv_ref[0, rows, :] = (coef * v).astype(cv_ref.dtype)


def _prep_kernel(xr_ref, xk_ref, xv_ref, lora_ref, w0_ref, wup_ref, a0_ref, aup_ref,
                 kkg_ref, ka_ref, rk_ref,
                 kk_ref, ke0_ref, ke1_ref, bb0_ref, bb1_ref, cin0_ref, cin1_ref, cv_ref):
    L = CHUNK
    tt_ = lax.broadcasted_iota(jnp.int32, (L, L), 0)
    ss_ = lax.broadcasted_iota(jnp.int32, (L, L), 1)
    tri = (jnp.where(tt_ >= ss_, 1.0, 0.0).astype(BF16), jnp.where(tt_ <= ss_, 1.0, 0.0).astype(BF16))
    consts = (_head_ones(PREP_WIDTH), tri)
    refs = (xr_ref, xk_ref, xv_ref, lora_ref, w0_ref, wup_ref, a0_ref, aup_ref, kkg_ref, ka_ref, rk_ref,
            kk_ref, (ke0_ref, ke1_ref), (bb0_ref, bb1_ref), (cin0_ref, cin1_ref), cv_ref)
    tt = xr_ref.shape[1]
    band = tt // PREP_ROW_BANDS
    _interleave([_prep_rows(slice(i * band, (i + 1) * band), refs, consts) for i in range(PREP_ROW_BANDS)])


def _prep(proj, lora, w0, wup_pad, a0, aup_pad, k_k, k_a, r_k, tt=PREP_ROWS):
    b, t, _ = proj.shape
    tt = min(tt, t)
    w = PREP_WIDTH
    col = lambda base: (lambda i, s, p: (i, s, base // w + p))
    vec = lambda i, s, p: (0, p)
    tile = pl.BlockSpec((1, tt, w), lambda i, s, p: (i, s, p))
    sd = lambda dt: jax.ShapeDtypeStruct((b, t, R_WIDTH), dt)
    return pl.pallas_call(
        _prep_kernel,
        grid=(b, t // tt, R_WIDTH // w),
        in_specs=[pl.BlockSpec((1, tt, w), col(COL_XR)),
                  pl.BlockSpec((1, tt, w), col(COL_XK)),
                  pl.BlockSpec((1, tt, w), col(COL_XV)),
                  pl.BlockSpec((1, tt, 2 * LANES), lambda i, s, p: (i, s, 0)),
                  pl.BlockSpec((1, 2 * w), vec),
                  pl.BlockSpec((LANES, 2 * w), vec),
                  pl.BlockSpec((1, 2 * w), vec),
                  pl.BlockSpec((LANES, 2 * w), vec),
                  pl.BlockSpec((1, w), vec),
                  pl.BlockSpec((1, w), vec),
                  pl.BlockSpec((1, w), vec)],
        out_specs=[tile] * 8,
        out_shape=[sd(BF16), sd(BF16), sd(BF16), sd(BF16), sd(BF16), sd(F32), sd(F32), sd(BF16)],
        compiler_params=pltpu.CompilerParams(
            dimension_semantics=("arbitrary", "arbitrary", "arbitrary"),
            vmem_limit_bytes=VMEM_LIMIT),
        name="rwkv_prep",
    )(proj, proj, proj, lora, w0, wup_pad, a0, aup_pad, k_k, k_a, r_k)


def _wkv_consts(reverse):
    L = CHUNK
    t = lax.broadcasted_iota(jnp.int32, (L, 2 * L), 0)
    s = lax.broadcasted_iota(jnp.int32, (L, 2 * L), 1) % L
    if reverse:
        strict, incl = t < s, t <= s
    else:
        strict, incl = t > s, t >= s
    eye2 = jnp.where(t == s, 1.0, 0.0).astype(F32)
    rr = lax.broadcasted_iota(jnp.int32, (LANES, LANES), 0)
    cc = lax.broadcasted_iota(jnp.int32, (LANES, LANES), 1)
    bd = (rr // HEAD_DIM) == (cc // HEAD_DIM)
    eye = rr == cc
    row = lax.broadcasted_iota(jnp.int32, (L, LANES), 0)
    first = row == (L - 1 if reverse else 0)
    return strict, incl, eye2, bd, eye, first


def _wkv_chunk(load, get_state, finish, consts, reverse):
    L = CHUNK
    strict, incl, eye2, bd, eye, first = consts

    def bd2(x):
        return jnp.where(bd, jnp.concatenate([x, x], axis=0), 0.0)

    cin, kk, b, ke, v, r = load()
    a = -kk
    cex = jnp.where(first, 0.0, pltpu.roll(cin, L - 1 if reverse else 1, axis=0))
    cend = cin[0:1, :] if reverse else cin[L - 1:L, :]
    m = 0.5 * cend
    e_in = jnp.exp(cin - m)
    e_out = jnp.exp(m - cin)
    em = jnp.exp(m)
    at = a * jnp.exp(cex - m)
    rt = r * e_in
    bt = b * e_out
    kt = ke * e_out
    ap, rp, bh, kh = at * em, rt * em, bt * em, kt * em

    ar = jnp.concatenate([at, rt], axis=0)
    g = _bdot_nt(ar, jnp.concatenate([bd2(bt), bd2(kt)], axis=0))
    yield
    a_ab = jnp.where(strict, g[:L, :LANES], 0.0)
    a_ak = jnp.where(strict, g[:L, LANES:], 0.0)
    p_rbk = jnp.concatenate([jnp.where(incl, g[L:, :LANES], 0.0),
                             jnp.where(incl, g[L:, LANES:], 0.0)], axis=1)

    n = a_ab
    p = eye2 + n
    n = _bdot(n, bd2(n))
    vb = bd2(v)
    x = _bdot(a_ak, vb)
    yield
    steps = int(math.log2(L)) - 1
    for i in range(steps):
        nb = bd2(n)
        if i < steps - 1:
            np_ = _bdot(jnp.concatenate([n, p], axis=0), nb)
            n = np_[:L]
            p = p + np_[L:]
        else:
            p = p + _bdot(p, nb)
        yield
    tinv = p

    wu = _bdot(tinv, jnp.concatenate([bd2(ap), bd2(x)], axis=1))
    yield
    w, u0 = wu[:, :LANES], wu[:, LANES:]
    zero = jnp.zeros((LANES, LANES), F32)
    rhs = jnp.concatenate([jnp.concatenate([bd2(w), bd2(u0)], axis=1),
                           jnp.concatenate([zero, vb], axis=1)], axis=0)
    pwo = _bdot(p_rbk, rhs)
    q = rp + pwo[:, :LANES]
    o0 = pwo[:, LANES:]
    zl = jnp.zeros((L, LANES), F32)
    mc = _bdot_tn(jnp.concatenate([bh, kh], axis=0),
                  jnp.concatenate([wu, jnp.concatenate([zl, v], axis=1)], axis=0))
    m_off = jnp.where(bd, mc[:, :LANES], 0.0)
    c = jnp.where(bd, mc[:, LANES:], 0.0)
    decay_col = jnp.sum(jnp.where(eye, jnp.exp(cend), 0.0), axis=1, keepdims=True)
    yield
    h = get_state()
    qm = _bdot(jnp.concatenate([q, m_off], axis=0), h)
    finish(qm[:L] + o0, decay_col * h + qm[L:] + c)


def _interleave(gens):
    results = [None] * len(gens)
    live = list(range(len(gens)))
    while live:
        for i in list(live):
            try:
                next(gens[i])
            except StopIteration as stop:
                results[i] = stop.value
                live.remove(i)
    return results


def _wkv_kernel(rf_ref, vf_ref, kkf_ref, kef_ref, bbf_ref, cinf_ref,
                rb_ref, vb_ref, kkb_ref, keb_ref, bbb_ref, cinb_ref,
                of_ref, ob_ref, h_ref):
    @pl.when(pl.program_id(2) == 0)
    def _():
        h_ref[...] = jnp.zeros_like(h_ref)

    dirs = ((cinf_ref, kkf_ref, bbf_ref, kef_ref, vf_ref, rf_ref, of_ref),
            (cinb_ref, kkb_ref, bbb_ref, keb_ref, vb_ref, rb_ref, ob_ref))
    consts = (_wkv_consts(False), _wkv_consts(True))
    L = CHUNK
    state = {}
    gens = []
    for k in range(WKV_CHUNKS_PER_STEP):
        for i in range(WKV_PAIRS_PER_STEP):
            sl = slice(i * LANES, (i + 1) * LANES)
            for d in range(2):
                refs = dirs[d]
                s = k if d == 0 else WKV_CHUNKS_PER_STEP - 1 - k
                rows = slice(s * L, (s + 1) * L)

                def load(refs=refs, rows=rows, sl=sl):
                    return tuple(ref[0, rows, sl].astype(F32) for ref in refs[:6])

                def get_state(d=d, i=i):
                    return state[d, i] if (d, i) in state else h_ref[d, i]

                def finish(out, h_new, o_ref=refs[6], rows=rows, sl=sl, d=d, i=i):
                    o_ref[0, rows, sl] = out.astype(o_ref.dtype)
                    state[d, i] = h_new

                gens.append(_wkv_chunk(load, get_state, finish, consts[d], reverse=(d == 1)))
    _interleave(gens)
    for (d, i), h_new in state.items():
        h_ref[d, i] = h_new


def _wkv(proj, kk, ke0, ke1, bb0, bb1, cin0, cin1):
    b, t, _ = proj.shape
    nc = t // (CHUNK * WKV_CHUNKS_PER_STEP)
    L = CHUNK * WKV_CHUNKS_PER_STEP
    w = WKV_PAIRS_PER_STEP * LANES
    fwd = lambda base: (lambda i, p, c: (i, c, base // w + p))
    bwd = lambda base: (lambda i, p, c: (i, nc - 1 - c, base // w + p))
    blk = lambda im: pl.BlockSpec((1, L, w), im)
    out_sd = jax.ShapeDtypeStruct((b, t, R_WIDTH), BF16)
    return pl.pallas_call(
        _wkv_kernel,
        grid=(b, PAIRS // WKV_PAIRS_PER_STEP, nc),
        in_specs=[blk(fwd(COL_XR)), blk(fwd(COL_XV)), blk(fwd(0)), blk(fwd(0)), blk(fwd(0)), blk(fwd(0)),
                  blk(bwd(COL_XR)), blk(bwd(COL_XV)), blk(bwd(0)), blk(bwd(0)), blk(bwd(0)), blk(bwd(0))],
        out_specs=[blk(fwd(0)), blk(bwd(0))],
        out_shape=[out_sd, out_sd],
        scratch_shapes=[pltpu.VMEM((2, WKV_PAIRS_PER_STEP, LANES, LANES), F32)],
        compiler_params=pltpu.CompilerParams(
            dimension_semantics=("arbitrary", "arbitrary", "arbitrary"),
            vmem_limit_bytes=VMEM_LIMIT),
        name="wkv",
    )(proj, proj, kk, ke0, bb0, cin0, proj, proj, kk, ke1, bb1, cin1)


def _attn_kernel(q_ref, k_ref, v_ref, o_ref, kh_ref, vt_ref, *, tq):
    t = k_ref.shape[1]
    kb = ATTN_KEY_BLOCK
    heads_per_tile = ATTN_QUERY_TILE // tq
    ntile = A_GROUP // heads_per_tile

    @pl.when(pl.program_id(1) == 0)
    def _():
        r = lax.broadcasted_iota(jnp.int32, (A_KV_WIDTH, HEAD_DIM), 0)
        c = lax.broadcasted_iota(jnp.int32, (A_KV_WIDTH, HEAD_DIM), 1)
        kf = k_ref[0]
        vt = v_ref[0].astype(F32).T.astype(BF16)
        ones = jnp.ones((ATTN_VT_ROWS - HEAD_DIM, t), BF16)
        for hk in range(A_KV_HEADS):
            sel = jnp.where(r == hk * HEAD_DIM + c, 1.0, 0.0).astype(BF16)
            kh_ref[hk] = jnp.dot(kf, sel, preferred_element_type=F32).astype(BF16)
            vt_ref[hk] = jnp.concatenate([vt[hk * HEAD_DIM:(hk + 1) * HEAD_DIM, :], ones], axis=0)

    qt = q_ref[0].astype(F32).T.astype(BF16)

    def q_tile(hk, c):
        h0 = hk * A_GROUP + c * heads_per_tile
        return jnp.concatenate([qt[(h0 + e) * HEAD_DIM:(h0 + e + 1) * HEAD_DIM, :]
                                for e in range(heads_per_tile)], axis=1)

    def scores(hk, c, j):
        return jnp.dot(kh_ref[hk, j * kb:(j + 1) * kb, :], q_tile(hk, c), preferred_element_type=F32)

    tasks = [(hk, c, j) for hk in range(A_KV_HEADS) for j in range(t // kb) for c in range(ntile)]
    state = {}
    done = {}
    ahead = [scores(*task) for task in tasks[:ATTN_SCORE_LOOKAHEAD]]
    for i, (hk, c, j) in enumerate(tasks):
        s = ahead.pop(0)
        if i + ATTN_SCORE_LOOKAHEAD < len(tasks):
            ahead.append(scores(*tasks[i + ATTN_SCORE_LOOKAHEAD]))
        if j == 0:
            state[hk, c] = (jnp.full((1, ATTN_QUERY_TILE), -1e30, F32),
                            jnp.zeros((ATTN_VT_ROWS, ATTN_QUERY_TILE), F32))
        m, acc = state[hk, c]
        m_new = jnp.maximum(m, jnp.max(s, axis=0, keepdims=True))
        p = jnp.exp2(s - m_new).astype(BF16)
        acc = acc * jnp.exp2(m - m_new) + jnp.dot(vt_ref[hk, :, j * kb:(j + 1) * kb], p,
                                                  preferred_element_type=F32)
        state[hk, c] = (m_new, acc)
        if j == t // kb - 1:
            ot = acc[:HEAD_DIM] / acc[HEAD_DIM:HEAD_DIM + 1]
            for e in range(heads_per_tile):
                done[hk * A_GROUP + c * heads_per_tile + e] = ot[:, e * tq:(e + 1) * tq]
    rows = [done[h] for h in range(A_Q_HEADS)]
    o_ref[0] = jnp.concatenate(rows, axis=0).T.astype(o_ref.dtype)


def _attention(proj, tq=ATTN_QUERIES):
    b, t, _ = proj.shape
    kern = functools.partial(_attn_kernel, tq=tq)
    return pl.pallas_call(
        kern,
        grid=(b, t // tq),
        in_specs=[pl.BlockSpec((1, tq, A_WIDTH), lambda i, j: (i, j, COL_Q // A_WIDTH)),
                  pl.BlockSpec((1, t, A_KV_WIDTH), lambda i, j: (i, 0, COL_K // A_KV_WIDTH)),
                  pl.BlockSpec((1, t, A_KV_WIDTH), lambda i, j: (i, 0, COL_V // A_KV_WIDTH))],
        out_specs=pl.BlockSpec((1, tq, A_WIDTH), lambda i, j: (i, j, 0)),
        out_shape=jax.ShapeDtypeStruct((b, t, A_WIDTH), BF16),
        scratch_shapes=[pltpu.VMEM((A_KV_HEADS, t, HEAD_DIM), BF16),
                        pltpu.VMEM((A_KV_HEADS, ATTN_VT_ROWS, t), BF16)],
        compiler_params=pltpu.CompilerParams(dimension_semantics=("arbitrary", "arbitrary"),
                                             vmem_limit_bytes=VMEM_LIMIT),
        name="attention",
    )(proj, proj, proj)


def _post_kernel(x_ref, of_ref, ob_ref, cv_ref, zr_ref, at_ref, za_ref, gr_ref, ga_ref,
                 gnw_ref, gnb_ref, wbr_ref, wba_ref, wout_ref, fg_ref, o_ref):
    za = za_ref[...].astype(F32)
    o_a = at_ref[...].astype(F32) * (za * jax.nn.sigmoid(za))
    p_a = _bdot(o_a, wba_ref[...])
    gw = POST_GN_WIDTH
    ones = _head_ones(gw)
    wkv = of_ref[...].astype(F32) + ob_ref[...].astype(F32)
    groups = [slice(i * gw, (i + 1) * gw) for i in range(R_WIDTH // gw)]
    mus = [_dot_exact_lhs(wkv[:, sl], ones) * (1.0 / HEAD_DIM) for sl in groups]
    ycs = [wkv[:, sl] - mu for sl, mu in zip(groups, mus)]
    vrs = [_dot_exact_lhs(yc * yc, ones) * (1.0 / HEAD_DIM) for yc in ycs]
    gn = jnp.concatenate([yc * lax.rsqrt(var + GN_EPS) for yc, var in zip(ycs, vrs)], axis=1)
    gn = gn * gnw_ref[...] + gnb_ref[...]
    zr = zr_ref[...].astype(F32)
    o_r = (gn + cv_ref[...].astype(F32)) * (zr * jax.nn.sigmoid(zr))
    p_r = _bdot(o_r, wbr_ref[...])
    merged = (jax.nn.sigmoid(gr_ref[...].astype(F32)) * p_r
              + jax.nn.sigmoid(ga_ref[...].astype(F32)) * p_a)
    y = x_ref[...] + _bdot(merged, wout_ref[...])
    yn = y * lax.rsqrt(jnp.mean(y * y, axis=-1, keepdims=True) + NORM_EPS)
    o_ref[...] = yn * fg_ref[...]


def _post(x2d, of, ob, cv, proj2d, attn, gn_w, gn_b, w_br, w_ba, w_out, final_g, tm=POST_ROWS):
    n, d = x2d.shape
    row = lambda w, base: pl.BlockSpec((tm, w), lambda i: (i, base // w))
    const = lambda shape: pl.BlockSpec(shape, lambda i: (0, 0))
    return pl.pallas_call(
        _post_kernel,
        grid=(n // tm,),
        in_specs=[row(d, 0), row(R_WIDTH, 0), row(R_WIDTH, 0), row(R_WIDTH, 0),
                  row(R_WIDTH, COL_ZR), row(A_WIDTH, 0), row(A_WIDTH, COL_ZA),
                  row(d, COL_GR), row(d, COL_GA),
                  const((1, R_WIDTH)), const((1, R_WIDTH)),
                  const((R_WIDTH, d)), const((A_WIDTH, d)), const((d, d)), const((1, d))],
        out_specs=row(d, 0),
        out_shape=jax.ShapeDtypeStruct((n, d), F32),
        compiler_params=pltpu.CompilerParams(dimension_semantics=("arbitrary",),
                                             vmem_limit_bytes=VMEM_LIMIT),
        name="merge_out",
    )(x2d, of, ob, cv, proj2d, attn, proj2d, proj2d, proj2d,
      gn_w.reshape(1, -1), gn_b.reshape(1, -1), w_br, w_ba, w_out, final_g.reshape(1, -1))


def _column_segments():
    shift_w = 3 * R_WIDTH + 4 * LORA
    o_wd, o_ad = 3 * R_WIDTH, 3 * R_WIDTH + 2 * LORA
    o_zr = shift_w
    o_q = o_zr + R_WIDTH
    o_k = o_q + A_WIDTH
    o_v = o_k + A_KV_WIDTH
    o_za = o_v + A_KV_WIDTH
    o_g = o_za + A_WIDTH
    segs = [(0, 3 * R_WIDTH), (o_zr, R_WIDTH), (o_q, A_WIDTH), (o_za, A_WIDTH), (o_g, 2 * D_MODEL),
            (o_k, A_KV_WIDTH), (o_v, A_KV_WIDTH), (o_wd, 2 * LORA), (o_ad, 2 * LORA)]
    assert sum(w for _, w in segs) == D_IN
    return segs, shift_w


def _permute_columns(a, segs):
    return jnp.concatenate([a[..., s:s + w] for s, w in segs], axis=-1)


def _pack_lora(up):
    z = jnp.zeros((LORA, PAIRS, LANES), up.dtype)
    top = jnp.concatenate([up[0].reshape(LORA, PAIRS, LANES), z], axis=2)
    bot = jnp.concatenate([z, up[1].reshape(LORA, PAIRS, LANES)], axis=2)
    return jnp.concatenate([top, bot], axis=0).reshape(2 * LORA, 2 * R_WIDTH)


def _pack_dirs(v):
    return jnp.stack([v[0].reshape(PAIRS, LANES), v[1].reshape(PAIRS, LANES)], axis=1).reshape(1, -1)


def _layer(x, norm_g, w_in, shift_mu, w0, w_up, a0, a_up, k_k, k_a, r_k, gn_w, gn_b,
           q_norm_g, k_norm_g, w_branch_rwkv, w_branch_attn, w_out, out_g):
    b, t, d = x.shape
    assert t % (CHUNK * WKV_CHUNKS_PER_STEP) == 0 and t % GRID_W == 0 and d == D_MODEL and w_in.shape[1] == D_IN
    segs, shift_w = _column_segments()
    pad_cols = lambda a: jnp.pad(a, ((0, 0), (0, D_PROJ - D_IN)))
    w_perm = pad_cols(_permute_columns(w_in.astype(BF16), segs))
    mu_full = jnp.concatenate([shift_mu, jnp.zeros((2, D_IN - shift_w), F32)], axis=1)
    mu_perm = pad_cols(_permute_columns(mu_full, segs))
    g_perm = jnp.zeros((1, D_PROJ), F32)
    g_perm = g_perm.at[0, COL_Q:COL_Q + A_WIDTH].set(jnp.tile(q_norm_g * (HEAD_DIM ** -0.5 * LOG2_E), A_Q_HEADS))
    g_perm = g_perm.at[0, COL_K:COL_K + A_KV_WIDTH].set(jnp.tile(k_norm_g, A_KV_HEADS))

    x2d = x.reshape(b * t, d)
    h = _rmsnorm(x2d, norm_g, BF16).reshape(b, t, d)
    proj, lora = _inproj(h, w_perm, mu_perm, g_perm)
    kk, ke0, ke1, bb0, bb1, cin0, cin1, cv = _prep(
        proj, lora, _pack_dirs(w0), _pack_lora(w_up), _pack_dirs(a0), _pack_lora(a_up),
        k_k.reshape(1, -1), k_a.reshape(1, -1), r_k.reshape(1, -1))
    of, ob = _wkv(proj, kk, ke0, ke1, bb0, bb1, cin0, cin1)
    attn = _attention(proj)
    n = b * t
    out = _post(x2d, of.reshape(n, -1), ob.reshape(n, -1), cv.reshape(n, -1), proj.reshape(n, D_PROJ),
                attn.reshape(n, -1), gn_w, gn_b, w_branch_rwkv.astype(BF16), w_branch_attn.astype(BF16),
                w_out.astype(BF16), out_g)
    return out.reshape(b, t, d)


def kernel(x, norm_g, w_in, shift_mu, w0, w_up, a0, a_up, k_k, k_a, r_k, gn_w, gn_b, q_norm_g, k_norm_g,
           w_branch_rwkv, w_branch_attn, w_out, final_norm_g):
    assert norm_g.shape[0] == 1, "single-layer block"
    return _layer(x, norm_g[0], w_in[0], shift_mu[0], w0[0], w_up[0], a0[0], a_up[0], k_k[0], k_a[0],
                  r_k[0], gn_w[0], gn_b[0], q_norm_g[0], k_norm_g[0], w_branch_rwkv[0],
                  w_branch_attn[0], w_out[0], final_norm_g)
```

```python
import functools
import math

import jax
import jax.numpy as jnp
import numpy as np
from jax import lax
from jax.experimental import pallas as pl
from jax.experimental.pallas import tpu as pltpu

F32 = jnp.float32
BF16 = jnp.bfloat16

HEAD_DIM = 64
R_HEADS = 16
R_WIDTH = R_HEADS * HEAD_DIM
LORA = 64
A_Q_HEADS = 16
A_KV_HEADS = 4
A_GROUP = A_Q_HEADS // A_KV_HEADS
A_WIDTH = A_Q_HEADS * HEAD_DIM
A_KV_WIDTH = A_KV_HEADS * HEAD_DIM
GRID_W = 64
ROPE_THETA = 10000.0
NORM_EPS = 1e-6
GN_EPS = 64e-5
KK_EPS = 1e-24
DECAY_SCALE = math.exp(-0.5)

D_MODEL = 2048
LOG2_E = math.log2(math.e)

LANES = 128
SUBLANES = 8
BF16_SUBLANES = 16
MXU_WIDTH = 256
VMEM_LIMIT = 56 * 1024 * 1024

NORM_ROWS = 1024
PROJ_TN = 2 * MXU_WIDTH
PROJ_HALF = MXU_WIDTH
PROJ_PLAIN_ROWS = 2048
PROJ_SHIFT_ROWS = 1024
PROJ_ROPE_ROWS = 512
PROJ_SHIFT_PAD = SUBLANES
PREP_ROWS = 2048
PREP_WIDTH = MXU_WIDTH
PREP_ROW_BANDS = 2
CHUNK = 64
PAIRS = R_WIDTH // LANES
WKV_PAIRS_PER_STEP = 8
WKV_CHUNKS_PER_STEP = 4
ATTN_QUERIES = 512
ATTN_KEY_BLOCK = 128
ATTN_QUERY_TILE = MXU_WIDTH
ATTN_SCORE_LOOKAHEAD = 8
ATTN_VT_ROWS = HEAD_DIM + BF16_SUBLANES
POST_ROWS = 256
POST_GN_WIDTH = MXU_WIDTH

COL_XR, COL_XK, COL_XV = 0, R_WIDTH, 2 * R_WIDTH
COL_ZR = 3 * R_WIDTH
COL_Q = COL_ZR + R_WIDTH
COL_ZA = COL_Q + A_WIDTH
COL_GR = COL_ZA + A_WIDTH
COL_GA = COL_GR + D_MODEL
COL_K = COL_GA + D_MODEL
COL_V = COL_K + A_KV_WIDTH
COL_WD = COL_V + A_KV_WIDTH
COL_AD = COL_WD + 2 * LORA
D_IN = COL_AD + 2 * LORA
D_PROJ = -(-D_IN // PROJ_TN) * PROJ_TN


def _bdot(a, b):
    return jnp.dot(a.astype(BF16), b.astype(BF16), preferred_element_type=F32)


def _bdot_nt(a, b):
    return lax.dot_general(a.astype(BF16), b.astype(BF16), (((1,), (1,)), ((), ())),
                           preferred_element_type=F32)


def _bdot_tn(a, b):
    return lax.dot_general(a.astype(BF16), b.astype(BF16), (((0,), (0,)), ((), ())),
                           preferred_element_type=F32)


def _split2(x):
    hi = x.astype(BF16)
    lo = (x - hi.astype(F32)).astype(BF16)
    return hi, lo


def _dot_exact_rhs(a_bf16, x):
    hi, lo = _split2(x)
    d = lambda y: jnp.dot(a_bf16, y, preferred_element_type=F32)
    return d(hi) + d(lo)


def _dot_exact_lhs(x, b_bf16):
    hi, lo = _split2(x)
    d = lambda y: jnp.dot(y, b_bf16, preferred_element_type=F32)
    return d(hi) + d(lo)


def _dot_f32(x, w):
    xh = x.astype(BF16)
    xl = (x - xh.astype(F32)).astype(BF16)
    wh = w.astype(BF16)
    wl = (w - wh.astype(F32)).astype(BF16)
    d = lambda p, q: jnp.dot(p, q, preferred_element_type=F32)
    return d(xh, wh) + d(xl, wh) + d(xh, wl)


def _head_ones(width=LANES):
    r = lax.broadcasted_iota(jnp.int32, (width, width), 0) // HEAD_DIM
    c = lax.broadcasted_iota(jnp.int32, (width, width), 1) // HEAD_DIM
    return jnp.where(r == c, 1.0, 0.0).astype(BF16)


def _rmsnorm_kernel(x_ref, g_ref, o_ref):
    x = x_ref[...]
    y = x * lax.rsqrt(jnp.mean(x * x, axis=-1, keepdims=True) + NORM_EPS)
    o_ref[...] = (y * g_ref[...]).astype(o_ref.dtype)


def _rmsnorm(x2d, g, out_dtype, tm=NORM_ROWS):
    n, d = x2d.shape
    return pl.pallas_call(
        _rmsnorm_kernel,
        grid=(n // tm,),
        in_specs=[pl.BlockSpec((tm, d), lambda i: (i, 0)),
                  pl.BlockSpec((1, d), lambda i: (0, 0))],
        out_specs=pl.BlockSpec((tm, d), lambda i: (i, 0)),
        out_shape=jax.ShapeDtypeStruct((n, d), out_dtype),
        compiler_params=pltpu.CompilerParams(dimension_semantics=("arbitrary",),
                                             vmem_limit_bytes=VMEM_LIMIT),
        name="rmsnorm",
    )(x2d, g.reshape(1, d))


def _rope_tables(t):
    rows = t // GRID_W
    row = np.repeat(np.arange(rows, dtype=np.float32), GRID_W)
    colv = np.tile(np.arange(GRID_W, dtype=np.float32), rows)
    axis_dim = HEAD_DIM // 2
    freqs = jnp.asarray(ROPE_THETA, F32) ** (-jnp.arange(0, axis_dim, 2, dtype=F32) / axis_dim)
    ang = jnp.concatenate([jnp.asarray(row)[:, None] * freqs, jnp.asarray(colv)[:, None] * freqs], axis=-1)
    reps = PROJ_HALF // HEAD_DIM
    cos = jnp.tile(jnp.repeat(jnp.cos(ang), 2, axis=-1), (1, reps))
    sin = jnp.tile(jnp.repeat(jnp.sin(ang), 2, axis=-1), (1, reps))
    return cos, sin


def _inproj_tile_kinds():
    def kind(col):
        if col < 3 * R_WIDTH or COL_WD <= col < COL_WD + PROJ_HALF:
            return "shift"
        if COL_Q <= col < COL_Q + A_WIDTH or COL_K <= col < COL_K + A_KV_WIDTH:
            return "rope"
        return "plain"
    return [(kind(j * PROJ_TN), kind(j * PROJ_TN + PROJ_HALF)) for j in range(D_PROJ // PROJ_TN)]


def _inproj_kernel(h_ref, w_ref, mu_ref, g_ref, cos_ref, sin_ref, o_ref, lora_ref, acc_ref, *, kinds):
    j = pl.program_id(1)
    t = h_ref.shape[1]
    pad = PROJ_SHIFT_PAD
    hw = PROJ_HALF

    def run(pair, is_lora):
        rc = PROJ_ROPE_ROWS if "rope" in pair else PROJ_SHIFT_ROWS if "shift" in pair else PROJ_PLAIN_ROWS
        rc = min(rc, t)
        nchunk = t // rc

        def mm(c):
            return jnp.dot(h_ref[0, c * rc:(c + 1) * rc, :], w_ref[...], preferred_element_type=F32)

        def emit_plain(c, cols, acc):
            o_ref[0, c * rc:(c + 1) * rc, cols] = acc.astype(o_ref.dtype)

        def emit_rope(c, cols, acc):
            rows = slice(c * rc, (c + 1) * rc)
            ss = _dot_exact_lhs(acc * acc, _head_ones(hw)) * (1.0 / HEAD_DIM)
            y = acc * lax.rsqrt(ss + NORM_EPS) * g_ref[:, cols]
            lane = lax.broadcasted_iota(jnp.int32, (rc, hw), 1)
            nxt = pltpu.roll(y, hw - 1, axis=1)
            prv = pltpu.roll(y, 1, axis=1)
            o_ref[0, rows, cols] = (y * cos_ref[rows, :] + jnp.where((lane % 2) == 0, -nxt, prv)
                                    * sin_ref[rows, :]).astype(o_ref.dtype)

        def emit_shift(c, cols):
            lo = pad + c * rc
            mu = mu_ref[:, cols]
            cur = acc_ref[lo:lo + rc, cols]
            prev = acc_ref[lo - 1:lo - 1 + rc, cols]
            nxt = acc_ref[lo + 1:lo + 1 + rc, cols]
            y = cur + mu[0:1] * (prev - cur) + mu[1:2] * (nxt - cur)
            o_ref[0, c * rc:(c + 1) * rc, cols] = y.astype(o_ref.dtype)
            if is_lora:
                lora_ref[0, c * rc:(c + 1) * rc, :] = y

        halves = [(slice(i * hw, (i + 1) * hw), k) for i, k in enumerate(pair)]
        if "shift" in pair:
            acc_ref[0:pad, :] = jnp.zeros((pad, PROJ_TN), F32)
            acc_ref[pad + t:2 * pad + t, :] = jnp.zeros((pad, PROJ_TN), F32)
        ahead = mm(0)
        for c in range(nchunk):
            acc, ahead = ahead, (mm(c + 1) if c + 1 < nchunk else None)
            for cols, k in halves:
                if k == "shift":
                    acc_ref[pad + c * rc:pad + (c + 1) * rc, cols] = acc[:, cols]
                    if c > 0:
                        emit_shift(c - 1, cols)
                elif k == "rope":
                    emit_rope(c, cols, acc[:, cols])
                else:
                    emit_plain(c, cols, acc[:, cols])
        for cols, k in halves:
            if k == "shift":
                emit_shift(nchunk - 1, cols)

    lora_tile = COL_WD // PROJ_TN
    for pair in sorted(set(kinds)):
        tiles = [i for i, k in enumerate(kinds) if k == pair and i != lora_tile]
        if tiles:
            cond = functools.reduce(jnp.logical_or, [j == i for i in tiles])
            pl.when(cond)(functools.partial(run, pair, False))
    pl.when(j == lora_tile)(functools.partial(run, kinds[lora_tile], True))


def _inproj(h, w_perm, mu_perm, g_perm):
    b, t, d = h.shape
    nj = D_PROJ // PROJ_TN
    cos, sin = _rope_tables(t)
    kern = functools.partial(_inproj_kernel, kinds=_inproj_tile_kinds())
    return pl.pallas_call(
        kern,
        grid=(b, nj),
        in_specs=[pl.BlockSpec((1, t, d), lambda i, j: (i, 0, 0)),
                  pl.BlockSpec((d, PROJ_TN), lambda i, j: (0, j)),
                  pl.BlockSpec((2, PROJ_TN), lambda i, j: (0, j)),
                  pl.BlockSpec((1, PROJ_TN), lambda i, j: (0, j)),
                  pl.BlockSpec((t, PROJ_HALF), lambda i, j: (0, 0)),
                  pl.BlockSpec((t, PROJ_HALF), lambda i, j: (0, 0))],
        out_specs=[pl.BlockSpec((1, t, PROJ_TN), lambda i, j: (i, 0, j)),
                   pl.BlockSpec((1, t, PROJ_HALF), lambda i, j: (i, 0, 0))],
        out_shape=[jax.ShapeDtypeStruct((b, t, D_PROJ), BF16),
                   jax.ShapeDtypeStruct((b, t, PROJ_HALF), F32)],
        scratch_shapes=[pltpu.VMEM((t + 2 * PROJ_SHIFT_PAD, PROJ_TN), F32)],
        compiler_params=pltpu.CompilerParams(dimension_semantics=("arbitrary", "arbitrary"),
                                             vmem_limit_bytes=VMEM_LIMIT),
        name="inproj",
    )(h, w_perm, mu_perm, g_perm, cos, sin)


def _prep_rows(rows, refs, consts):
    (xr_ref, xk_ref, xv_ref, lora_ref, w0_ref, wup_ref, a0_ref, aup_ref, kkg_ref, ka_ref, rk_ref,
     kk_ref, ke_refs, bb_refs, cin_refs, cv_ref) = refs
    ones, tri = consts
    L = CHUNK
    npair = PREP_WIDTH // LANES
    r = xr_ref[0, rows, :].astype(F32)
    k = xk_ref[0, rows, :].astype(F32)
    v = xv_ref[0, rows, :].astype(F32)
    lora = lora_ref[0, rows, :]
    wd = jnp.tanh(lora[:, :LANES])
    ad = lora[:, LANES:]
    k_a = ka_ref[...]
    kkn = k * kkg_ref[...]
    ss = _dot_exact_lhs(kkn * kkn, ones)
    w_raw = w0_ref[...] + _dot_f32(wd, wup_ref[...])
    a_raw = a0_ref[...] + _bdot(ad, aup_ref[...])
    yield
    kk = kkn * lax.rsqrt(jnp.maximum(ss, KK_EPS))
    kk_ref[0, rows, :] = kk.astype(kk_ref.dtype)
    lw_all = -DECAY_SCALE * jax.nn.sigmoid(w_raw)
    a_all = jax.nn.sigmoid(a_raw)
    nchunk = k.shape[0] // L
    row0 = rows.start
    ke_sum = None
    cins = []
    for d in range(2):
        pick = lambda z: jnp.concatenate(
            [z[:, (2 * p + d) * LANES:(2 * p + d + 1) * LANES] for p in range(npair)], axis=1)
        lw, a = pick(lw_all), pick(a_all)
        wide = jnp.concatenate([lw[c * L:(c + 1) * L, :] for c in range(nchunk)], axis=1)
        cins.append(_dot_exact_rhs(tri[d], wide))
        bb_refs[d][0, rows, :] = (kk * a).astype(bb_refs[d].dtype)
        ke = k * (1.0 + (a - 1.0) * k_a)
        ke_refs[d][0, rows, :] = ke.astype(ke_refs[d].dtype)
        ke_sum = ke if ke_sum is None else ke_sum + ke
    coef = _dot_exact_lhs(r * ke_sum * rk_ref[...], ones)
    yield
    for d in range(2):
        for c in range(nchunk):
            cin_refs[d][0, row0 + c * L:row0 + (c + 1) * L, :] = cins[d][:, c * PREP_WIDTH:(c + 1) * PREP_WIDTH]
    cv_ref[0, rows, :] = (coef * v).astype(cv_ref.dtype)


def _prep_kernel(xr_ref, xk_ref, xv_ref, lora_ref, w0_ref, wup_ref, a0_ref, aup_ref,
                 kkg_ref, ka_ref, rk_ref,
                 kk_ref, ke0_ref, ke1_ref, bb0_ref, bb1_ref, cin0_ref, cin1_ref, cv_ref):
    L = CHUNK
    tt_ = lax.broadcasted_iota(jnp.int32, (L, L), 0)
    ss_ = lax.broadcasted_iota(jnp.int32, (L, L), 1)
    tri = (jnp.where(tt_ >= ss_, 1.0, 0.0).astype(BF16), jnp.where(tt_ <= ss_, 1.0, 0.0).astype(BF16))
    consts = (_head_ones(PREP_WIDTH), tri)
    refs = (xr_ref, xk_ref, xv_ref, lora_ref, w0_ref, wup_ref, a0_ref, aup_ref, kkg_ref, ka_ref, rk_ref,
            kk_ref, (ke0_ref, ke1_ref), (bb0_ref, bb1_ref), (cin0_ref, cin1_ref), cv_ref)
    tt = xr_ref.shape[1]
    band = tt // PREP_ROW_BANDS
    _interleave([_prep_rows(slice(i * band, (i + 1) * band), refs, consts) for i in range(PREP_ROW_BANDS)])


def _prep(proj, lora, w0, wup_pad, a0, aup_pad, k_k, k_a, r_k, tt=PREP_ROWS):
    b, t, _ = proj.shape
    tt = min(tt, t)
    w = PREP_WIDTH
    col = lambda base: (lambda i, s, p: (i, s, base // w + p))
    vec = lambda i, s, p: (0, p)
    tile = pl.BlockSpec((1, tt, w), lambda i, s, p: (i, s, p))
    sd = lambda dt: jax.ShapeDtypeStruct((b, t, R_WIDTH), dt)
    return pl.pallas_call(
        _prep_kernel,
        grid=(b, t // tt, R_WIDTH // w),
        in_specs=[pl.BlockSpec((1, tt, w), col(COL_XR)),
                  pl.BlockSpec((1, tt, w), col(COL_XK)),
                  pl.BlockSpec((1, tt, w), col(COL_XV)),
                  pl.BlockSpec((1, tt, 2 * LANES), lambda i, s, p: (i, s, 0)),
                  pl.BlockSpec((1, 2 * w), vec),
                  pl.BlockSpec((LANES, 2 * w), vec),
                  pl.BlockSpec((1, 2 * w), vec),
                  pl.BlockSpec((LANES, 2 * w), vec),
                  pl.BlockSpec((1, w), vec),
                  pl.BlockSpec((1, w), vec),
                  pl.BlockSpec((1, w), vec)],
        out_specs=[tile] * 8,
        out_shape=[sd(BF16), sd(BF16), sd(BF16), sd(BF16), sd(BF16), sd(F32), sd(F32), sd(BF16)],
        compiler_params=pltpu.CompilerParams(
            dimension_semantics=("arbitrary", "arbitrary", "arbitrary"),
            vmem_limit_bytes=VMEM_LIMIT),
        name="rwkv_prep",
    )(proj, proj, proj, lora, w0, wup_pad, a0, aup_pad, k_k, k_a, r_k)


def _wkv_consts(reverse):
    L = CHUNK
    t = lax.broadcasted_iota(jnp.int32, (L, 2 * L), 0)
    s = lax.broadcasted_iota(jnp.int32, (L, 2 * L), 1) % L
    if reverse:
        strict, incl = t < s, t <= s
    else:
        strict, incl = t > s, t >= s
    eye2 = jnp.where(t == s, 1.0, 0.0).astype(F32)
    rr = lax.broadcasted_iota(jnp.int32, (LANES, LANES), 0)
    cc = lax.broadcasted_iota(jnp.int32, (LANES, LANES), 1)
    bd = (rr // HEAD_DIM) == (cc // HEAD_DIM)
    eye = rr == cc
    row = lax.broadcasted_iota(jnp.int32, (L, LANES), 0)
    first = row == (L - 1 if reverse else 0)
    return strict, incl, eye2, bd, eye, first


def _wkv_chunk(load, get_state, finish, consts, reverse):
    L = CHUNK
    strict, incl, eye2, bd, eye, first = consts

    def bd2(x):
        return jnp.where(bd, jnp.concatenate([x, x], axis=0), 0.0)

    cin, kk, b, ke, v, r = load()
    a = -kk
    cex = jnp.where(first, 0.0, pltpu.roll(cin, L - 1 if reverse else 1, axis=0))
    cend = cin[0:1, :] if reverse else cin[L - 1:L, :]
    m = 0.5 * cend
    e_in = jnp.exp(cin - m)
    e_out = jnp.exp(m - cin)
    em = jnp.exp(m)
    at = a * jnp.exp(cex - m)
    rt = r * e_in
    bt = b * e_out
    kt = ke * e_out
    ap, rp, bh, kh = at * em, rt * em, bt * em, kt * em

    ar = jnp.concatenate([at, rt], axis=0)
    g = _bdot_nt(ar, jnp.concatenate([bd2(bt), bd2(kt)], axis=0))
    yield
    a_ab = jnp.where(strict, g[:L, :LANES], 0.0)
    a_ak = jnp.where(strict, g[:L, LANES:], 0.0)
    p_rbk = jnp.concatenate([jnp.where(incl, g[L:, :LANES], 0.0),
                             jnp.where(incl, g[L:, LANES:], 0.0)], axis=1)

    n = a_ab
    p = eye2 + n
    n = _bdot(n, bd2(n))
    vb = bd2(v)
    x = _bdot(a_ak, vb)
    yield
    steps = int(math.log2(L)) - 1
    for i in range(steps):
        nb = bd2(n)
        if i < steps - 1:
            np_ = _bdot(jnp.concatenate([n, p], axis=0), nb)
            n = np_[:L]
            p = p + np_[L:]
        else:
            p = p + _bdot(p, nb)
        yield
    tinv = p

    wu = _bdot(tinv, jnp.concatenate([bd2(ap), bd2(x)], axis=1))
    yield
    w, u0 = wu[:, :LANES], wu[:, LANES:]
    zero = jnp.zeros((LANES, LANES), F32)
    rhs = jnp.concatenate([jnp.concatenate([bd2(w), bd2(u0)], axis=1),
                           jnp.concatenate([zero, vb], axis=1)], axis=0)
    pwo = _bdot(p_rbk, rhs)
    q = rp + pwo[:, :LANES]
    o0 = pwo[:, LANES:]
    zl = jnp.zeros((L, LANES), F32)
    mc = _bdot_tn(jnp.concatenate([bh, kh], axis=0),
                  jnp.concatenate([wu, jnp.concatenate([zl, v], axis=1)], axis=0))
    m_off = jnp.where(bd, mc[:, :LANES], 0.0)
    c = jnp.where(bd, mc[:, LANES:], 0.0)
    decay_col = jnp.sum(jnp.where(eye, jnp.exp(cend), 0.0), axis=1, keepdims=True)
    yield
    h = get_state()
    qm = _bdot(jnp.concatenate([q, m_off], axis=0), h)
    finish(qm[:L] + o0, decay_col * h + qm[L:] + c)


def _interleave(gens):
    results = [None] * len(gens)
    live = list(range(len(gens)))
    while live:
        for i in list(live):
            try:
                next(gens[i])
            except StopIteration as stop:
                results[i] = stop.value
                live.remove(i)
    return results


def _wkv_kernel(rf_ref, vf_ref, kkf_ref, kef_ref, bbf_ref, cinf_ref,
                rb_ref, vb_ref, kkb_ref, keb_ref, bbb_ref, cinb_ref,
                of_ref, ob_ref, h_ref):
    @pl.when(pl.program_id(2) == 0)
    def _():
        h_ref[...] = jnp.zeros_like(h_ref)

    dirs = ((cinf_ref, kkf_ref, bbf_ref, kef_ref, vf_ref, rf_ref, of_ref),
            (cinb_ref, kkb_ref, bbb_ref, keb_ref, vb_ref, rb_ref, ob_ref))
    consts = (_wkv_consts(False), _wkv_consts(True))
    L = CHUNK
    state = {}
    gens = []
    for k in range(WKV_CHUNKS_PER_STEP):
        for i in range(WKV_PAIRS_PER_STEP):
            sl = slice(i * LANES, (i + 1) * LANES)
            for d in range(2):
                refs = dirs[d]
                s = k if d == 0 else WKV_CHUNKS_PER_STEP - 1 - k
                rows = slice(s * L, (s + 1) * L)

                def load(refs=refs, rows=rows, sl=sl):
                    return tuple(ref[0, rows, sl].astype(F32) for ref in refs[:6])

                def get_state(d=d, i=i):
                    return state[d, i] if (d, i) in state else h_ref[d, i]

                def finish(out, h_new, o_ref=refs[6], rows=rows, sl=sl, d=d, i=i):
                    o_ref[0, rows, sl] = out.astype(o_ref.dtype)
                    state[d, i] = h_new

                gens.append(_wkv_chunk(load, get_state, finish, consts[d], reverse=(d == 1)))
    _interleave(gens)
    for (d, i), h_new in state.items():
        h_ref[d, i] = h_new


def _wkv(proj, kk, ke0, ke1, bb0, bb1, cin0, cin1):
    b, t, _ = proj.shape
    nc = t // (CHUNK * WKV_CHUNKS_PER_STEP)
    L = CHUNK * WKV_CHUNKS_PER_STEP
    w = WKV_PAIRS_PER_STEP * LANES
    fwd = lambda base: (lambda i, p, c: (i, c, base // w + p))
    bwd = lambda base: (lambda i, p, c: (i, nc - 1 - c, base // w + p))
    blk = lambda im: pl.BlockSpec((1, L, w), im)
    out_sd = jax.ShapeDtypeStruct((b, t, R_WIDTH), BF16)
    return pl.pallas_call(
        _wkv_kernel,
        grid=(b, PAIRS // WKV_PAIRS_PER_STEP, nc),
        in_specs=[blk(fwd(COL_XR)), blk(fwd(COL_XV)), blk(fwd(0)), blk(fwd(0)), blk(fwd(0)), blk(fwd(0)),
                  blk(bwd(COL_XR)), blk(bwd(COL_XV)), blk(bwd(0)), blk(bwd(0)), blk(bwd(0)), blk(bwd(0))],
        out_specs=[blk(fwd(0)), blk(bwd(0))],
        out_shape=[out_sd, out_sd],
        scratch_shapes=[pltpu.VMEM((2, WKV_PAIRS_PER_STEP, LANES, LANES), F32)],
        compiler_params=pltpu.CompilerParams(
            dimension_semantics=("arbitrary", "arbitrary", "arbitrary"),
            vmem_limit_bytes=VMEM_LIMIT),
        name="wkv",
    )(proj, proj, kk, ke0, bb0, cin0, proj, proj, kk, ke1, bb1, cin1)


def _attn_kernel(q_ref, k_ref, v_ref, o_ref, kh_ref, vt_ref, *, tq):
    t = k_ref.shape[1]
    kb = ATTN_KEY_BLOCK
    tile = ATTN_QUERY_TILE
    parts = tq // tile
    ntile = A_GROUP * parts

    @pl.when(pl.program_id(1) == 0)
    def _():
        r = lax.broadcasted_iota(jnp.int32, (A_KV_WIDTH, HEAD_DIM), 0)
        c = lax.broadcasted_iota(jnp.int32, (A_KV_WIDTH, HEAD_DIM), 1)
        kf = k_ref[0]
        vt = v_ref[0].astype(F32).T.astype(BF16)
        ones = jnp.ones((ATTN_VT_ROWS - HEAD_DIM, t), BF16)
        for hk in range(A_KV_HEADS):
            sel = jnp.where(r == hk * HEAD_DIM + c, 1.0, 0.0).astype(BF16)
            kh_ref[hk] = jnp.dot(kf, sel, preferred_element_type=F32).astype(BF16)
            vt_ref[hk] = jnp.concatenate([vt[hk * HEAD_DIM:(hk + 1) * HEAD_DIM, :], ones], axis=0)

    qt = q_ref[0].astype(F32).T.astype(BF16)

    def q_tile(hk, c):
        head, seg = hk * A_GROUP + c // parts, c % parts
        return qt[head * HEAD_DIM:(head + 1) * HEAD_DIM, seg * tile:(seg + 1) * tile]

    def scores(hk, c, j):
        return jnp.dot(kh_ref[hk, j * kb:(j + 1) * kb, :], q_tile(hk, c), preferred_element_type=F32)

    tasks = [(hk, c, j) for hk in range(A_KV_HEADS) for j in range(t // kb) for c in range(ntile)]
    state = {}
    done = {}
    ahead = [scores(*task) for task in tasks[:ATTN_SCORE_LOOKAHEAD]]
    for i, (hk, c, j) in enumerate(tasks):
        s = ahead.pop(0)
        if i + ATTN_SCORE_LOOKAHEAD < len(tasks):
            ahead.append(scores(*tasks[i + ATTN_SCORE_LOOKAHEAD]))
        if j == 0:
            state[hk, c] = (jnp.full((1, ATTN_QUERY_TILE), -1e30, F32),
                            jnp.zeros((ATTN_VT_ROWS, ATTN_QUERY_TILE), F32))
        m, acc = state[hk, c]
        m_new = jnp.maximum(m, jnp.max(s, axis=0, keepdims=True))
        p = jnp.exp2(s - m_new).astype(BF16)
        acc = acc * jnp.exp2(m - m_new) + jnp.dot(vt_ref[hk, :, j * kb:(j + 1) * kb], p,
                                                  preferred_element_type=F32)
        state[hk, c] = (m_new, acc)
        if j == t // kb - 1:
            done[hk * A_GROUP + c // parts, c % parts] = acc[:HEAD_DIM] / acc[HEAD_DIM:HEAD_DIM + 1]
    rows = [jnp.concatenate([done[h, seg] for seg in range(parts)], axis=1) for h in range(A_Q_HEADS)]
    o_ref[0] = jnp.concatenate(rows, axis=0).T.astype(o_ref.dtype)


def _attention(proj, tq=ATTN_QUERIES):
    b, t, _ = proj.shape
    kern = functools.partial(_attn_kernel, tq=tq)
    return pl.pallas_call(
        kern,
        grid=(b, t // tq),
        in_specs=[pl.BlockSpec((1, tq, A_WIDTH), lambda i, j: (i, j, COL_Q // A_WIDTH)),
                  pl.BlockSpec((1, t, A_KV_WIDTH), lambda i, j: (i, 0, COL_K // A_KV_WIDTH)),
                  pl.BlockSpec((1, t, A_KV_WIDTH), lambda i, j: (i, 0, COL_V // A_KV_WIDTH))],
        out_specs=pl.BlockSpec((1, tq, A_WIDTH), lambda i, j: (i, j, 0)),
        out_shape=jax.ShapeDtypeStruct((b, t, A_WIDTH), BF16),
        scratch_shapes=[pltpu.VMEM((A_KV_HEADS, t, HEAD_DIM), BF16),
                        pltpu.VMEM((A_KV_HEADS, ATTN_VT_ROWS, t), BF16)],
        compiler_params=pltpu.CompilerParams(dimension_semantics=("arbitrary", "arbitrary"),
                                             vmem_limit_bytes=VMEM_LIMIT),
        name="attention",
    )(proj, proj, proj)


def _post_kernel(x_ref, of_ref, ob_ref, cv_ref, zr_ref, at_ref, za_ref, gr_ref, ga_ref,
                 gnw_ref, gnb_ref, wbr_ref, wba_ref, wout_ref, fg_ref, o_ref):
    za = za_ref[...].astype(F32)
    o_a = at_ref[...].astype(F32) * (za * jax.nn.sigmoid(za))
    p_a = _bdot(o_a, wba_ref[...])
    gw = POST_GN_WIDTH
    ones = _head_ones(gw)
    wkv = of_ref[...].astype(F32) + ob_ref[...].astype(F32)
    groups = [slice(i * gw, (i + 1) * gw) for i in range(R_WIDTH // gw)]
    mus = [_dot_exact_lhs(wkv[:, sl], ones) * (1.0 / HEAD_DIM) for sl in groups]
    ycs = [wkv[:, sl] - mu for sl, mu in zip(groups, mus)]
    vrs = [_dot_exact_lhs(yc * yc, ones) * (1.0 / HEAD_DIM) for yc in ycs]
    gn = jnp.concatenate([yc * lax.rsqrt(var + GN_EPS) for yc, var in zip(ycs, vrs)], axis=1)
    gn = gn * gnw_ref[...] + gnb_ref[...]
    zr = zr_ref[...].astype(F32)
    o_r = (gn + cv_ref[...].astype(F32)) * (zr * jax.nn.sigmoid(zr))
    p_r = _bdot(o_r, wbr_ref[...])
    merged = (jax.nn.sigmoid(gr_ref[...].astype(F32)) * p_r
              + jax.nn.sigmoid(ga_ref[...].astype(F32)) * p_a)
    y = x_ref[...] + _bdot(merged, wout_ref[...])
    yn = y * lax.rsqrt(jnp.mean(y * y, axis=-1, keepdims=True) + NORM_EPS)
    o_ref[...] = yn * fg_ref[...]


def _post(x2d, of, ob, cv, proj2d, attn, gn_w, gn_b, w_br, w_ba, w_out, final_g, tm=POST_ROWS):
    n, d = x2d.shape
    row = lambda w, base: pl.BlockSpec((tm, w), lambda i: (i, base // w))
    const = lambda shape: pl.BlockSpec(shape, lambda i: (0, 0))
    return pl.pallas_call(
        _post_kernel,
        grid=(n // tm,),
        in_specs=[row(d, 0), row(R_WIDTH, 0), row(R_WIDTH, 0), row(R_WIDTH, 0),
                  row(R_WIDTH, COL_ZR), row(A_WIDTH, 0), row(A_WIDTH, COL_ZA),
                  row(d, COL_GR), row(d, COL_GA),
                  const((1, R_WIDTH)), const((1, R_WIDTH)),
                  const((R_WIDTH, d)), const((A_WIDTH, d)), const((d, d)), const((1, d))],
        out_specs=row(d, 0),
        out_shape=jax.ShapeDtypeStruct((n, d), F32),
        compiler_params=pltpu.CompilerParams(dimension_semantics=("arbitrary",),
                                             vmem_limit_bytes=VMEM_LIMIT),
        name="merge_out",
    )(x2d, of, ob, cv, proj2d, attn, proj2d, proj2d, proj2d,
      gn_w.reshape(1, -1), gn_b.reshape(1, -1), w_br, w_ba, w_out, final_g.reshape(1, -1))


def _column_segments():
    shift_w = 3 * R_WIDTH + 4 * LORA
    o_wd, o_ad = 3 * R_WIDTH, 3 * R_WIDTH + 2 * LORA
    o_zr = shift_w
    o_q = o_zr + R_WIDTH
    o_k = o_q + A_WIDTH
    o_v = o_k + A_KV_WIDTH
    o_za = o_v + A_KV_WIDTH
    o_g = o_za + A_WIDTH
    segs = [(0, 3 * R_WIDTH), (o_zr, R_WIDTH), (o_q, A_WIDTH), (o_za, A_WIDTH), (o_g, 2 * D_MODEL),
            (o_k, A_KV_WIDTH), (o_v, A_KV_WIDTH), (o_wd, 2 * LORA), (o_ad, 2 * LORA)]
    assert sum(w for _, w in segs) == D_IN
    return segs, shift_w


def _permute_columns(a, segs):
    return jnp.concatenate([a[..., s:s + w] for s, w in segs], axis=-1)


def _pack_lora(up):
    z = jnp.zeros((LORA, PAIRS, LANES), up.dtype)
    top = jnp.concatenate([up[0].reshape(LORA, PAIRS, LANES), z], axis=2)
    bot = jnp.concatenate([z, up[1].reshape(LORA, PAIRS, LANES)], axis=2)
    return jnp.concatenate([top, bot], axis=0).reshape(2 * LORA, 2 * R_WIDTH)


def _pack_dirs(v):
    return jnp.stack([v[0].reshape(PAIRS, LANES), v[1].reshape(PAIRS, LANES)], axis=1).reshape(1, -1)


def _layer(x, norm_g, w_in, shift_mu, w0, w_up, a0, a_up, k_k, k_a, r_k, gn_w, gn_b,
           q_norm_g, k_norm_g, w_branch_rwkv, w_branch_attn, w_out, out_g):
    b, t, d = x.shape
    assert t % (CHUNK * WKV_CHUNKS_PER_STEP) == 0 and t % GRID_W == 0 and d == D_MODEL and w_in.shape[1] == D_IN
    segs, shift_w = _column_segments()
    pad_cols = lambda a: jnp.pad(a, ((0, 0), (0, D_PROJ - D_IN)))
    w_perm = pad_cols(_permute_columns(w_in.astype(BF16), segs))
    mu_full = jnp.concatenate([shift_mu, jnp.zeros((2, D_IN - shift_w), F32)], axis=1)
    mu_perm = pad_cols(_permute_columns(mu_full, segs))
    g_perm = jnp.zeros((1, D_PROJ), F32)
    g_perm = g_perm.at[0, COL_Q:COL_Q + A_WIDTH].set(jnp.tile(q_norm_g * (HEAD_DIM ** -0.5 * LOG2_E), A_Q_HEADS))
    g_perm = g_perm.at[0, COL_K:COL_K + A_KV_WIDTH].set(jnp.tile(k_norm_g, A_KV_HEADS))

    x2d = x.reshape(b * t, d)
    h = _rmsnorm(x2d, norm_g, BF16).reshape(b, t, d)
    proj, lora = _inproj(h, w_perm, mu_perm, g_perm)
    kk, ke0, ke1, bb0, bb1, cin0, cin1, cv = _prep(
        proj, lora, _pack_dirs(w0), _pack_lora(w_up), _pack_dirs(a0), _pack_lora(a_up),
        k_k.reshape(1, -1), k_a.reshape(1, -1), r_k.reshape(1, -1))
    of, ob = _wkv(proj, kk, ke0, ke1, bb0, bb1, cin0, cin1)
    attn = _attention(proj)
    n = b * t
    out = _post(x2d, of.reshape(n, -1), ob.reshape(n, -1), cv.reshape(n, -1), proj.reshape(n, D_PROJ),
                attn.reshape(n, -1), gn_w, gn_b, w_branch_rwkv.astype(BF16), w_branch_attn.astype(BF16),
                w_out.astype(BF16), out_g)
    return out.reshape(b, t, d)


def kernel(x, norm_g, w_in, shift_mu, w0, w_up, a0, a_up, k_k, k_a, r_k, gn_w, gn_b, q_norm_g, k_norm_g,
           w_branch_rwkv, w_branch_attn, w_out, final_norm_g):
    assert norm_g.shape[0] == 1, "single-layer block"
    return _layer(x, norm_g[0], w_in[0], shift_mu[0], w0[0], w_up[0], a0[0], a_up[0], k_k[0], k_a[0],
                  r_k[0], gn_w[0], gn_b[0], q_norm_g[0], k_norm_g[0], w_branch_rwkv[0],
                  w_branch_attn[0], w_out[0], final_norm_g)
```

```python
import functools
import math

import jax
import jax.numpy as jnp
import numpy as np
from jax import lax
from jax.experimental import pallas as pl
from jax.experimental.pallas import tpu as pltpu

F32 = jnp.float32
BF16 = jnp.bfloat16

HEAD_DIM = 64
R_HEADS = 16
R_WIDTH = R_HEADS * HEAD_DIM
LORA = 64
A_Q_HEADS = 16
A_KV_HEADS = 4
A_GROUP = A_Q_HEADS // A_KV_HEADS
A_WIDTH = A_Q_HEADS * HEAD_DIM
A_KV_WIDTH = A_KV_HEADS * HEAD_DIM
GRID_W = 64
ROPE_THETA = 10000.0
NORM_EPS = 1e-6
GN_EPS = 64e-5
KK_EPS = 1e-24
DECAY_SCALE = math.exp(-0.5)

D_MODEL = 2048
LOG2_E = math.log2(math.e)

LANES = 128
SUBLANES = 8
BF16_SUBLANES = 16
MXU_WIDTH = 256
VMEM_LIMIT = 56 * 1024 * 1024

NORM_ROWS = 1024
PROJ_TN = 2 * MXU_WIDTH
PROJ_HALF = MXU_WIDTH
PROJ_PLAIN_ROWS = 2048
PROJ_SHIFT_ROWS = 1024
PROJ_ROPE_ROWS = 512
PROJ_SHIFT_PAD = SUBLANES
PREP_ROWS = 2048
PREP_WIDTH = MXU_WIDTH
PREP_ROW_BANDS = 2
CHUNK = 64
PAIRS = R_WIDTH // LANES
WKV_PAIRS_PER_STEP = 8
WKV_CHUNKS_PER_STEP = 4
ATTN_QUERIES = 512
ATTN_KEY_BLOCK = 128
ATTN_QUERY_TILE = MXU_WIDTH
ATTN_SCORE_LOOKAHEAD = 8
ATTN_VT_ROWS = HEAD_DIM + BF16_SUBLANES
POST_ROWS = 512
POST_VMEM_LIMIT = 62 * 1024 * 1024
POST_GN_WIDTH = MXU_WIDTH

COL_XR, COL_XK, COL_XV = 0, R_WIDTH, 2 * R_WIDTH
COL_ZR = 3 * R_WIDTH
COL_Q = COL_ZR + R_WIDTH
COL_ZA = COL_Q + A_WIDTH
COL_GR = COL_ZA + A_WIDTH
COL_GA = COL_GR + D_MODEL
COL_K = COL_GA + D_MODEL
COL_V = COL_K + A_KV_WIDTH
COL_WD = COL_V + A_KV_WIDTH
COL_AD = COL_WD + 2 * LORA
D_IN = COL_AD + 2 * LORA
D_PROJ = -(-D_IN // PROJ_TN) * PROJ_TN


def _bdot(a, b):
    return jnp.dot(a.astype(BF16), b.astype(BF16), preferred_element_type=F32)


def _bdot_nt(a, b):
    return lax.dot_general(a.astype(BF16), b.astype(BF16), (((1,), (1,)), ((), ())),
                           preferred_element_type=F32)


def _bdot_tn(a, b):
    return lax.dot_general(a.astype(BF16), b.astype(BF16), (((0,), (0,)), ((), ())),
                           preferred_element_type=F32)


def _split2(x):
    hi = x.astype(BF16)
    lo = (x - hi.astype(F32)).astype(BF16)
    return hi, lo


def _dot_exact_rhs(a_bf16, x):
    hi, lo = _split2(x)
    d = lambda y: jnp.dot(a_bf16, y, preferred_element_type=F32)
    return d(hi) + d(lo)


def _dot_exact_lhs(x, b_bf16):
    hi, lo = _split2(x)
    d = lambda y: jnp.dot(y, b_bf16, preferred_element_type=F32)
    return d(hi) + d(lo)


def _dot_f32(x, w):
    xh = x.astype(BF16)
    xl = (x - xh.astype(F32)).astype(BF16)
    wh = w.astype(BF16)
    wl = (w - wh.astype(F32)).astype(BF16)
    d = lambda p, q: jnp.dot(p, q, preferred_element_type=F32)
    return d(xh, wh) + d(xl, wh) + d(xh, wl)


def _head_ones(width=LANES):
    r = lax.broadcasted_iota(jnp.int32, (width, width), 0) // HEAD_DIM
    c = lax.broadcasted_iota(jnp.int32, (width, width), 1) // HEAD_DIM
    return jnp.where(r == c, 1.0, 0.0).astype(BF16)


def _rmsnorm_kernel(x_ref, g_ref, o_ref):
    x = x_ref[...]
    y = x * lax.rsqrt(jnp.mean(x * x, axis=-1, keepdims=True) + NORM_EPS)
    o_ref[...] = (y * g_ref[...]).astype(o_ref.dtype)


def _rmsnorm(x2d, g, out_dtype, tm=NORM_ROWS):
    n, d = x2d.shape
    return pl.pallas_call(
        _rmsnorm_kernel,
        grid=(n // tm,),
        in_specs=[pl.BlockSpec((tm, d), lambda i: (i, 0)),
                  pl.BlockSpec((1, d), lambda i: (0, 0))],
        out_specs=pl.BlockSpec((tm, d), lambda i: (i, 0)),
        out_shape=jax.ShapeDtypeStruct((n, d), out_dtype),
        compiler_params=pltpu.CompilerParams(dimension_semantics=("arbitrary",),
                                             vmem_limit_bytes=VMEM_LIMIT),
        name="rmsnorm",
    )(x2d, g.reshape(1, d))


def _rope_tables(t):
    rows = t // GRID_W
    row = np.repeat(np.arange(rows, dtype=np.float32), GRID_W)
    colv = np.tile(np.arange(GRID_W, dtype=np.float32), rows)
    axis_dim = HEAD_DIM // 2
    freqs = jnp.asarray(ROPE_THETA, F32) ** (-jnp.arange(0, axis_dim, 2, dtype=F32) / axis_dim)
    ang = jnp.concatenate([jnp.asarray(row)[:, None] * freqs, jnp.asarray(colv)[:, None] * freqs], axis=-1)
    reps = PROJ_HALF // HEAD_DIM
    cos = jnp.tile(jnp.repeat(jnp.cos(ang), 2, axis=-1), (1, reps))
    sin = jnp.tile(jnp.repeat(jnp.sin(ang), 2, axis=-1), (1, reps))
    return cos, sin


def _inproj_tile_kinds():
    def kind(col):
        if col < 3 * R_WIDTH or COL_WD <= col < COL_WD + PROJ_HALF:
            return "shift"
        if COL_Q <= col < COL_Q + A_WIDTH or COL_K <= col < COL_K + A_KV_WIDTH:
            return "rope"
        return "plain"
    return [(kind(j * PROJ_TN), kind(j * PROJ_TN + PROJ_HALF)) for j in range(D_PROJ // PROJ_TN)]


def _inproj_kernel(h_ref, w_ref, mu_ref, g_ref, cos_ref, sin_ref, o_ref, lora_ref, acc_ref, *, kinds):
    j = pl.program_id(1)
    t = h_ref.shape[1]
    pad = PROJ_SHIFT_PAD
    hw = PROJ_HALF

    def run(pair, is_lora):
        rc = PROJ_ROPE_ROWS if "rope" in pair else PROJ_SHIFT_ROWS if "shift" in pair else PROJ_PLAIN_ROWS
        rc = min(rc, t)
        nchunk = t // rc

        def mm(c):
            return jnp.dot(h_ref[0, c * rc:(c + 1) * rc, :], w_ref[...], preferred_element_type=F32)

        def emit_plain(c, cols, acc):
            o_ref[0, c * rc:(c + 1) * rc, cols] = acc.astype(o_ref.dtype)

        def emit_rope(c, cols, acc):
            rows = slice(c * rc, (c + 1) * rc)
            ss = _dot_exact_lhs(acc * acc, _head_ones(hw)) * (1.0 / HEAD_DIM)
            y = acc * lax.rsqrt(ss + NORM_EPS) * g_ref[:, cols]
            lane = lax.broadcasted_iota(jnp.int32, (rc, hw), 1)
            nxt = pltpu.roll(y, hw - 1, axis=1)
            prv = pltpu.roll(y, 1, axis=1)
            o_ref[0, rows, cols] = (y * cos_ref[rows, :] + jnp.where((lane % 2) == 0, -nxt, prv)
                                    * sin_ref[rows, :]).astype(o_ref.dtype)

        def emit_shift(c, cols):
            lo = pad + c * rc
            mu = mu_ref[:, cols]
            cur = acc_ref[lo:lo + rc, cols]
            prev = acc_ref[lo - 1:lo - 1 + rc, cols]
            nxt = acc_ref[lo + 1:lo + 1 + rc, cols]
            y = cur + mu[0:1] * (prev - cur) + mu[1:2] * (nxt - cur)
            o_ref[0, c * rc:(c + 1) * rc, cols] = y.astype(o_ref.dtype)
            if is_lora:
                lora_ref[0, c * rc:(c + 1) * rc, :] = y

        halves = [(slice(i * hw, (i + 1) * hw), k) for i, k in enumerate(pair)]
        if "shift" in pair:
            acc_ref[0:pad, :] = jnp.zeros((pad, PROJ_TN), F32)
            acc_ref[pad + t:2 * pad + t, :] = jnp.zeros((pad, PROJ_TN), F32)
        ahead = mm(0)
        for c in range(nchunk):
            acc, ahead = ahead, (mm(c + 1) if c + 1 < nchunk else None)
            for cols, k in halves:
                if k == "shift":
                    acc_ref[pad + c * rc:pad + (c + 1) * rc, cols] = acc[:, cols]
                    if c > 0:
                        emit_shift(c - 1, cols)
                elif k == "rope":
                    emit_rope(c, cols, acc[:, cols])
                else:
                    emit_plain(c, cols, acc[:, cols])
        for cols, k in halves:
            if k == "shift":
                emit_shift(nchunk - 1, cols)

    lora_tile = COL_WD // PROJ_TN
    for pair in sorted(set(kinds)):
        tiles = [i for i, k in enumerate(kinds) if k == pair and i != lora_tile]
        if tiles:
            cond = functools.reduce(jnp.logical_or, [j == i for i in tiles])
            pl.when(cond)(functools.partial(run, pair, False))
    pl.when(j == lora_tile)(functools.partial(run, kinds[lora_tile], True))


def _inproj(h, w_perm, mu_perm, g_perm):
    b, t, d = h.shape
    nj = D_PROJ // PROJ_TN
    cos, sin = _rope_tables(t)
    kern = functools.partial(_inproj_kernel, kinds=_inproj_tile_kinds())
    return pl.pallas_call(
        kern,
        grid=(b, nj),
        in_specs=[pl.BlockSpec((1, t, d), lambda i, j: (i, 0, 0)),
                  pl.BlockSpec((d, PROJ_TN), lambda i, j: (0, j)),
                  pl.BlockSpec((2, PROJ_TN), lambda i, j: (0, j)),
                  pl.BlockSpec((1, PROJ_TN), lambda i, j: (0, j)),
                  pl.BlockSpec((t, PROJ_HALF), lambda i, j: (0, 0)),
                  pl.BlockSpec((t, PROJ_HALF), lambda i, j: (0, 0))],
        out_specs=[pl.BlockSpec((1, t, PROJ_TN), lambda i, j: (i, 0, j)),
                   pl.BlockSpec((1, t, PROJ_HALF), lambda i, j: (i, 0, 0))],
        out_shape=[jax.ShapeDtypeStruct((b, t, D_PROJ), BF16),
                   jax.ShapeDtypeStruct((b, t, PROJ_HALF), F32)],
        scratch_shapes=[pltpu.VMEM((t + 2 * PROJ_SHIFT_PAD, PROJ_TN), F32)],
        compiler_params=pltpu.CompilerParams(dimension_semantics=("arbitrary", "arbitrary"),
                                             vmem_limit_bytes=VMEM_LIMIT),
        name="inproj",
    )(h, w_perm, mu_perm, g_perm, cos, sin)


def _prep_rows(rows, refs, consts):
    (xr_ref, xk_ref, xv_ref, lora_ref, w0_ref, wup_ref, a0_ref, aup_ref, kkg_ref, ka_ref, rk_ref,
     kk_ref, ke_refs, bb_refs, cin_refs, cv_ref) = refs
    ones, tri = consts
    L = CHUNK
    npair = PREP_WIDTH // LANES
    r = xr_ref[0, rows, :].astype(F32)
    k = xk_ref[0, rows, :].astype(F32)
    v = xv_ref[0, rows, :].astype(F32)
    lora = lora_ref[0, rows, :]
    wd = jnp.tanh(lora[:, :LANES])
    ad = lora[:, LANES:]
    k_a = ka_ref[...]
    kkn = k * kkg_ref[...]
    ss = _dot_exact_lhs(kkn * kkn, ones)
    w_raw = w0_ref[...] + _dot_f32(wd, wup_ref[...])
    a_raw = a0_ref[...] + _bdot(ad, aup_ref[...])
    yield
    kk = kkn * lax.rsqrt(jnp.maximum(ss, KK_EPS))
    kk_ref[0, rows, :] = kk.astype(kk_ref.dtype)
    lw_all = -DECAY_SCALE * jax.nn.sigmoid(w_raw)
    a_all = jax.nn.sigmoid(a_raw)
    nchunk = k.shape[0] // L
    row0 = rows.start
    ke_sum = None
    cins = []
    for d in range(2):
        pick = lambda z: jnp.concatenate(
            [z[:, (2 * p + d) * LANES:(2 * p + d + 1) * LANES] for p in range(npair)], axis=1)
        lw, a = pick(lw_all), pick(a_all)
        wide = jnp.concatenate([lw[c * L:(c + 1) * L, :] for c in range(nchunk)], axis=1)
        cins.append(_dot_exact_rhs(tri[d], wide))
        bb_refs[d][0, rows, :] = (kk * a).astype(bb_refs[d].dtype)
        ke = k * (1.0 + (a - 1.0) * k_a)
        ke_refs[d][0, rows, :] = ke.astype(ke_refs[d].dtype)
        ke_sum = ke if ke_sum is None else ke_sum + ke
    coef = _dot_exact_lhs(r * ke_sum * rk_ref[...], ones)
    yield
    for d in range(2):
        for c in range(nchunk):
            cin_refs[d][0, row0 + c * L:row0 + (c + 1) * L, :] = cins[d][:, c * PREP_WIDTH:(c + 1) * PREP_WIDTH]
    cv_ref[0, rows, :] = (coef * v).astype(cv_ref.dtype)


def _prep_kernel(xr_ref, xk_ref, xv_ref, lora_ref, w0_ref, wup_ref, a0_ref, aup_ref,
                 kkg_ref, ka_ref, rk_ref,
                 kk_ref, ke0_ref, ke1_ref, bb0_ref, bb1_ref, cin0_ref, cin1_ref, cv_ref):
    L = CHUNK
    tt_ = lax.broadcasted_iota(jnp.int32, (L, L), 0)
    ss_ = lax.broadcasted_iota(jnp.int32, (L, L), 1)
    tri = (jnp.where(tt_ >= ss_, 1.0, 0.0).astype(BF16), jnp.where(tt_ <= ss_, 1.0, 0.0).astype(BF16))
    consts = (_head_ones(PREP_WIDTH), tri)
    refs = (xr_ref, xk_ref, xv_ref, lora_ref, w0_ref, wup_ref, a0_ref, aup_ref, kkg_ref, ka_ref, rk_ref,
            kk_ref, (ke0_ref, ke1_ref), (bb0_ref, bb1_ref), (cin0_ref, cin1_ref), cv_ref)
    tt = xr_ref.shape[1]
    band = tt // PREP_ROW_BANDS
    _interleave([_prep_rows(slice(i * band, (i + 1) * band), refs, consts) for i in range(PREP_ROW_BANDS)])


def _prep(proj, lora, w0, wup_pad, a0, aup_pad, k_k, k_a, r_k, tt=PREP_ROWS):
    b, t, _ = proj.shape
    tt = min(tt, t)
    w = PREP_WIDTH
    col = lambda base: (lambda i, s, p: (i, s, base // w + p))
    vec = lambda i, s, p: (0, p)
    tile = pl.BlockSpec((1, tt, w), lambda i, s, p: (i, s, p))
    sd = lambda dt: jax.ShapeDtypeStruct((b, t, R_WIDTH), dt)
    return pl.pallas_call(
        _prep_kernel,
        grid=(b, t // tt, R_WIDTH // w),
        in_specs=[pl.BlockSpec((1, tt, w), col(COL_XR)),
                  pl.BlockSpec((1, tt, w), col(COL_XK)),
                  pl.BlockSpec((1, tt, w), col(COL_XV)),
                  pl.BlockSpec((1, tt, 2 * LANES), lambda i, s, p: (i, s, 0)),
                  pl.BlockSpec((1, 2 * w), vec),
                  pl.BlockSpec((LANES, 2 * w), vec),
                  pl.BlockSpec((1, 2 * w), vec),
                  pl.BlockSpec((LANES, 2 * w), vec),
                  pl.BlockSpec((1, w), vec),
                  pl.BlockSpec((1, w), vec),
                  pl.BlockSpec((1, w), vec)],
        out_specs=[tile] * 8,
        out_shape=[sd(BF16), sd(BF16), sd(BF16), sd(BF16), sd(BF16), sd(F32), sd(F32), sd(BF16)],
        compiler_params=pltpu.CompilerParams(
            dimension_semantics=("arbitrary", "arbitrary", "arbitrary"),
            vmem_limit_bytes=VMEM_LIMIT),
        name="rwkv_prep",
    )(proj, proj, proj, lora, w0, wup_pad, a0, aup_pad, k_k, k_a, r_k)


def _wkv_consts(reverse):
    L = CHUNK
    t = lax.broadcasted_iota(jnp.int32, (L, 2 * L), 0)
    s = lax.broadcasted_iota(jnp.int32, (L, 2 * L), 1) % L
    if reverse:
        strict, incl = t < s, t <= s
    else:
        strict, incl = t > s, t >= s
    eye2 = jnp.where(t == s, 1.0, 0.0).astype(F32)
    rr = lax.broadcasted_iota(jnp.int32, (LANES, LANES), 0)
    cc = lax.broadcasted_iota(jnp.int32, (LANES, LANES), 1)
    bd = (rr // HEAD_DIM) == (cc // HEAD_DIM)
    eye = rr == cc
    row = lax.broadcasted_iota(jnp.int32, (L, LANES), 0)
    first = row == (L - 1 if reverse else 0)
    return strict, incl, eye2, bd, eye, first


def _wkv_chunk(load, get_state, finish, consts, reverse):
    L = CHUNK
    strict, incl, eye2, bd, eye, first = consts

    def bd2(x):
        return jnp.where(bd, jnp.concatenate([x, x], axis=0), 0.0)

    cin, kk, b, ke, v, r = load()
    a = -kk
    cex = jnp.where(first, 0.0, pltpu.roll(cin, L - 1 if reverse else 1, axis=0))
    cend = cin[0:1, :] if reverse else cin[L - 1:L, :]
    m = 0.5 * cend
    e_in = jnp.exp(cin - m)
    e_out = jnp.exp(m - cin)
    em = jnp.exp(m)
    at = a * jnp.exp(cex - m)
    rt = r * e_in
    bt = b * e_out
    kt = ke * e_out
    ap, rp, bh, kh = at * em, rt * em, bt * em, kt * em

    ar = jnp.concatenate([at, rt], axis=0)
    g = _bdot_nt(ar, jnp.concatenate([bd2(bt), bd2(kt)], axis=0))
    yield
    a_ab = jnp.where(strict, g[:L, :LANES], 0.0)
    a_ak = jnp.where(strict, g[:L, LANES:], 0.0)
    p_rbk = jnp.concatenate([jnp.where(incl, g[L:, :LANES], 0.0),
                             jnp.where(incl, g[L:, LANES:], 0.0)], axis=1)

    n = a_ab
    p = eye2 + n
    n = _bdot(n, bd2(n))
    vb = bd2(v)
    x = _bdot(a_ak, vb)
    yield
    steps = int(math.log2(L)) - 1
    for i in range(steps):
        nb = bd2(n)
        if i < steps - 1:
            np_ = _bdot(jnp.concatenate([n, p], axis=0), nb)
            n = np_[:L]
            p = p + np_[L:]
        else:
            p = p + _bdot(p, nb)
        yield
    tinv = p

    wu = _bdot(tinv, jnp.concatenate([bd2(ap), bd2(x)], axis=1))
    yield
    w, u0 = wu[:, :LANES], wu[:, LANES:]
    zero = jnp.zeros((LANES, LANES), F32)
    rhs = jnp.concatenate([jnp.concatenate([bd2(w), bd2(u0)], axis=1),
                           jnp.concatenate([zero, vb], axis=1)], axis=0)
    pwo = _bdot(p_rbk, rhs)
    q = rp + pwo[:, :LANES]
    o0 = pwo[:, LANES:]
    zl = jnp.zeros((L, LANES), F32)
    mc = _bdot_tn(jnp.concatenate([bh, kh], axis=0),
                  jnp.concatenate([wu, jnp.concatenate([zl, v], axis=1)], axis=0))
    m_off = jnp.where(bd, mc[:, :LANES], 0.0)
    c = jnp.where(bd, mc[:, LANES:], 0.0)
    decay_col = jnp.sum(jnp.where(eye, jnp.exp(cend), 0.0), axis=1, keepdims=True)
    yield
    h = get_state()
    qm = _bdot(jnp.concatenate([q, m_off], axis=0), h)
    finish(qm[:L] + o0, decay_col * h + qm[L:] + c)


def _interleave(gens):
    results = [None] * len(gens)
    live = list(range(len(gens)))
    while live:
        for i in list(live):
            try:
                next(gens[i])
            except StopIteration as stop:
                results[i] = stop.value
                live.remove(i)
    return results


def _wkv_kernel(rf_ref, vf_ref, kkf_ref, kef_ref, bbf_ref, cinf_ref,
                rb_ref, vb_ref, kkb_ref, keb_ref, bbb_ref, cinb_ref,
                of_ref, ob_ref, h_ref):
    @pl.when(pl.program_id(2) == 0)
    def _():
        h_ref[...] = jnp.zeros_like(h_ref)

    dirs = ((cinf_ref, kkf_ref, bbf_ref, kef_ref, vf_ref, rf_ref, of_ref),
            (cinb_ref, kkb_ref, bbb_ref, keb_ref, vb_ref, rb_ref, ob_ref))
    consts = (_wkv_consts(False), _wkv_consts(True))
    L = CHUNK
    state = {}
    gens = []
    for k in range(WKV_CHUNKS_PER_STEP):
        for i in range(WKV_PAIRS_PER_STEP):
            sl = slice(i * LANES, (i + 1) * LANES)
            for d in range(2):
                refs = dirs[d]
                s = k if d == 0 else WKV_CHUNKS_PER_STEP - 1 - k
                rows = slice(s * L, (s + 1) * L)

                def load(refs=refs, rows=rows, sl=sl):
                    return tuple(ref[0, rows, sl].astype(F32) for ref in refs[:6])

                def get_state(d=d, i=i):
                    return state[d, i] if (d, i) in state else h_ref[d, i]

                def finish(out, h_new, o_ref=refs[6], rows=rows, sl=sl, d=d, i=i):
                    o_ref[0, rows, sl] = out.astype(o_ref.dtype)
                    state[d, i] = h_new

                gens.append(_wkv_chunk(load, get_state, finish, consts[d], reverse=(d == 1)))
    _interleave(gens)
    for (d, i), h_new in state.items():
        h_ref[d, i] = h_new


def _wkv(proj, kk, ke0, ke1, bb0, bb1, cin0, cin1):
    b, t, _ = proj.shape
    nc = t // (CHUNK * WKV_CHUNKS_PER_STEP)
    L = CHUNK * WKV_CHUNKS_PER_STEP
    w = WKV_PAIRS_PER_STEP * LANES
    fwd = lambda base: (lambda i, p, c: (i, c, base // w + p))
    bwd = lambda base: (lambda i, p, c: (i, nc - 1 - c, base // w + p))
    blk = lambda im: pl.BlockSpec((1, L, w), im)
    out_sd = jax.ShapeDtypeStruct((b, t, R_WIDTH), BF16)
    return pl.pallas_call(
        _wkv_kernel,
        grid=(b, PAIRS // WKV_PAIRS_PER_STEP, nc),
        in_specs=[blk(fwd(COL_XR)), blk(fwd(COL_XV)), blk(fwd(0)), blk(fwd(0)), blk(fwd(0)), blk(fwd(0)),
                  blk(bwd(COL_XR)), blk(bwd(COL_XV)), blk(bwd(0)), blk(bwd(0)), blk(bwd(0)), blk(bwd(0))],
        out_specs=[blk(fwd(0)), blk(bwd(0))],
        out_shape=[out_sd, out_sd],
        scratch_shapes=[pltpu.VMEM((2, WKV_PAIRS_PER_STEP, LANES, LANES), F32)],
        compiler_params=pltpu.CompilerParams(
            dimension_semantics=("arbitrary", "arbitrary", "arbitrary"),
            vmem_limit_bytes=VMEM_LIMIT),
        name="wkv",
    )(proj, proj, kk, ke0, bb0, cin0, proj, proj, kk, ke1, bb1, cin1)


def _attn_kernel(q_ref, k_ref, v_ref, o_ref, kh_ref, vt_ref, *, tq):
    t = k_ref.shape[1]
    kb = ATTN_KEY_BLOCK
    tile = ATTN_QUERY_TILE
    parts = tq // tile
    ntile = A_GROUP * parts

    @pl.when(pl.program_id(1) == 0)
    def _():
        r = lax.broadcasted_iota(jnp.int32, (A_KV_WIDTH, HEAD_DIM), 0)
        c = lax.broadcasted_iota(jnp.int32, (A_KV_WIDTH, HEAD_DIM), 1)
        kf = k_ref[0]
        vt = v_ref[0].astype(F32).T.astype(BF16)
        ones = jnp.ones((ATTN_VT_ROWS - HEAD_DIM, t), BF16)
        for hk in range(A_KV_HEADS):
            sel = jnp.where(r == hk * HEAD_DIM + c, 1.0, 0.0).astype(BF16)
            kh_ref[hk] = jnp.dot(kf, sel, preferred_element_type=F32).astype(BF16)
            vt_ref[hk] = jnp.concatenate([vt[hk * HEAD_DIM:(hk + 1) * HEAD_DIM, :], ones], axis=0)

    qt = q_ref[0].astype(F32).T.astype(BF16)

    def q_tile(hk, c):
        head, seg = hk * A_GROUP + c // parts, c % parts
        return qt[head * HEAD_DIM:(head + 1) * HEAD_DIM, seg * tile:(seg + 1) * tile]

    def scores(hk, c, j):
        return jnp.dot(kh_ref[hk, j * kb:(j + 1) * kb, :], q_tile(hk, c), preferred_element_type=F32)

    tasks = [(hk, c, j) for hk in range(A_KV_HEADS) for j in range(t // kb) for c in range(ntile)]
    state = {}
    done = {}
    ahead = [scores(*task) for task in tasks[:ATTN_SCORE_LOOKAHEAD]]
    for i, (hk, c, j) in enumerate(tasks):
        s = ahead.pop(0)
        if i + ATTN_SCORE_LOOKAHEAD < len(tasks):
            ahead.append(scores(*tasks[i + ATTN_SCORE_LOOKAHEAD]))
        if j == 0:
            state[hk, c] = (jnp.full((1, ATTN_QUERY_TILE), -1e30, F32),
                            jnp.zeros((ATTN_VT_ROWS, ATTN_QUERY_TILE), F32))
        m, acc = state[hk, c]
        m_new = jnp.maximum(m, jnp.max(s, axis=0, keepdims=True))
        p = jnp.exp2(s - m_new).astype(BF16)
        acc = acc * jnp.exp2(m - m_new) + jnp.dot(vt_ref[hk, :, j * kb:(j + 1) * kb], p,
                                                  preferred_element_type=F32)
        state[hk, c] = (m_new, acc)
        if j == t // kb - 1:
            done[hk * A_GROUP + c // parts, c % parts] = acc[:HEAD_DIM] / acc[HEAD_DIM:HEAD_DIM + 1]
    rows = [jnp.concatenate([done[h, seg] for seg in range(parts)], axis=1) for h in range(A_Q_HEADS)]
    o_ref[0] = jnp.concatenate(rows, axis=0).T.astype(o_ref.dtype)


def _attention(proj, tq=ATTN_QUERIES):
    b, t, _ = proj.shape
    kern = functools.partial(_attn_kernel, tq=tq)
    return pl.pallas_call(
        kern,
        grid=(b, t // tq),
        in_specs=[pl.BlockSpec((1, tq, A_WIDTH), lambda i, j: (i, j, COL_Q // A_WIDTH)),
                  pl.BlockSpec((1, t, A_KV_WIDTH), lambda i, j: (i, 0, COL_K // A_KV_WIDTH)),
                  pl.BlockSpec((1, t, A_KV_WIDTH), lambda i, j: (i, 0, COL_V // A_KV_WIDTH))],
        out_specs=pl.BlockSpec((1, tq, A_WIDTH), lambda i, j: (i, j, 0)),
        out_shape=jax.ShapeDtypeStruct((b, t, A_WIDTH), BF16),
        scratch_shapes=[pltpu.VMEM((A_KV_HEADS, t, HEAD_DIM), BF16),
                        pltpu.VMEM((A_KV_HEADS, ATTN_VT_ROWS, t), BF16)],
        compiler_params=pltpu.CompilerParams(dimension_semantics=("arbitrary", "arbitrary"),
                                             vmem_limit_bytes=VMEM_LIMIT),
        name="attention",
    )(proj, proj, proj)


def _post_kernel(x_ref, of_ref, ob_ref, cv_ref, zr_ref, at_ref, za_ref, gr_ref, ga_ref,
                 gnw_ref, gnb_ref, wbr_ref, wba_ref, wout_ref, fg_ref, o_ref):
    za = za_ref[...].astype(F32)
    o_a = at_ref[...].astype(F32) * (za * jax.nn.sigmoid(za))
    p_a = _bdot(o_a, wba_ref[...])
    gw = POST_GN_WIDTH
    ones = _head_ones(gw)
    wkv = of_ref[...].astype(F32) + ob_ref[...].astype(F32)
    groups = [slice(i * gw, (i + 1) * gw) for i in range(R_WIDTH // gw)]
    mus = [_dot_exact_lhs(wkv[:, sl], ones) * (1.0 / HEAD_DIM) for sl in groups]
    ycs = [wkv[:, sl] - mu for sl, mu in zip(groups, mus)]
    vrs = [_dot_exact_lhs(yc * yc, ones) * (1.0 / HEAD_DIM) for yc in ycs]
    gn = jnp.concatenate([yc * lax.rsqrt(var + GN_EPS) for yc, var in zip(ycs, vrs)], axis=1)
    gn = gn * gnw_ref[...] + gnb_ref[...]
    zr = zr_ref[...].astype(F32)
    o_r = (gn + cv_ref[...].astype(F32)) * (zr * jax.nn.sigmoid(zr))
    p_r = _bdot(o_r, wbr_ref[...])
    merged = (jax.nn.sigmoid(gr_ref[...].astype(F32)) * p_r
              + jax.nn.sigmoid(ga_ref[...].astype(F32)) * p_a)
    y = x_ref[...] + _bdot(merged, wout_ref[...])
    yn = y * lax.rsqrt(jnp.mean(y * y, axis=-1, keepdims=True) + NORM_EPS)
    o_ref[...] = yn * fg_ref[...]


def _post(x2d, of, ob, cv, proj2d, attn, gn_w, gn_b, w_br, w_ba, w_out, final_g, tm=POST_ROWS):
    n, d = x2d.shape
    row = lambda w, base: pl.BlockSpec((tm, w), lambda i: (i, base // w))
    const = lambda shape: pl.BlockSpec(shape, lambda i: (0, 0), pipeline_mode=pl.Buffered(1))
    return pl.pallas_call(
        _post_kernel,
        grid=(n // tm,),
        in_specs=[row(d, 0), row(R_WIDTH, 0), row(R_WIDTH, 0), row(R_WIDTH, 0),
                  row(R_WIDTH, COL_ZR), row(A_WIDTH, 0), row(A_WIDTH, COL_ZA),
                  row(d, COL_GR), row(d, COL_GA),
                  const((1, R_WIDTH)), const((1, R_WIDTH)),
                  const((R_WIDTH, d)), const((A_WIDTH, d)), const((d, d)), const((1, d))],
        out_specs=row(d, 0),
        out_shape=jax.ShapeDtypeStruct((n, d), F32),
        compiler_params=pltpu.CompilerParams(dimension_semantics=("arbitrary",),
                                             vmem_limit_bytes=POST_VMEM_LIMIT),
        name="merge_out",
    )(x2d, of, ob, cv, proj2d, attn, proj2d, proj2d, proj2d,
      gn_w.reshape(1, -1), gn_b.reshape(1, -1), w_br, w_ba, w_out, final_g.reshape(1, -1))


def _column_segments():
    shift_w = 3 * R_WIDTH + 4 * LORA
    o_wd, o_ad = 3 * R_WIDTH, 3 * R_WIDTH + 2 * LORA
    o_zr = shift_w
    o_q = o_zr + R_WIDTH
    o_k = o_q + A_WIDTH
    o_v = o_k + A_KV_WIDTH
    o_za = o_v + A_KV_WIDTH
    o_g = o_za + A_WIDTH
    segs = [(0, 3 * R_WIDTH), (o_zr, R_WIDTH), (o_q, A_WIDTH), (o_za, A_WIDTH), (o_g, 2 * D_MODEL),
            (o_k, A_KV_WIDTH), (o_v, A_KV_WIDTH), (o_wd, 2 * LORA), (o_ad, 2 * LORA)]
    assert sum(w for _, w in segs) == D_IN
    return segs, shift_w


def _permute_columns(a, segs):
    return jnp.concatenate([a[..., s:s + w] for s, w in segs], axis=-1)


def _pack_lora(up):
    z = jnp.zeros((LORA, PAIRS, LANES), up.dtype)
    top = jnp.concatenate([up[0].reshape(LORA, PAIRS, LANES), z], axis=2)
    bot = jnp.concatenate([z, up[1].reshape(LORA, PAIRS, LANES)], axis=2)
    return jnp.concatenate([top, bot], axis=0).reshape(2 * LORA, 2 * R_WIDTH)


def _pack_dirs(v):
    return jnp.stack([v[0].reshape(PAIRS, LANES), v[1].reshape(PAIRS, LANES)], axis=1).reshape(1, -1)


def _layer(x, norm_g, w_in, shift_mu, w0, w_up, a0, a_up, k_k, k_a, r_k, gn_w, gn_b,
           q_norm_g, k_norm_g, w_branch_rwkv, w_branch_attn, w_out, out_g):
    b, t, d = x.shape
    assert t % (CHUNK * WKV_CHUNKS_PER_STEP) == 0 and t % GRID_W == 0 and d == D_MODEL and w_in.shape[1] == D_IN
    segs, shift_w = _column_segments()
    pad_cols = lambda a: jnp.pad(a, ((0, 0), (0, D_PROJ - D_IN)))
    w_perm = pad_cols(_permute_columns(w_in.astype(BF16), segs))
    mu_full = jnp.concatenate([shift_mu, jnp.zeros((2, D_IN - shift_w), F32)], axis=1)
    mu_perm = pad_cols(_permute_columns(mu_full, segs))
    g_perm = jnp.zeros((1, D_PROJ), F32)
    g_perm = g_perm.at[0, COL_Q:COL_Q + A_WIDTH].set(jnp.tile(q_norm_g * (HEAD_DIM ** -0.5 * LOG2_E), A_Q_HEADS))
    g_perm = g_perm.at[0, COL_K:COL_K + A_KV_WIDTH].set(jnp.tile(k_norm_g, A_KV_HEADS))

    x2d = x.reshape(b * t, d)
    h = _rmsnorm(x2d, norm_g, BF16).reshape(b, t, d)
    proj, lora = _inproj(h, w_perm, mu_perm, g_perm)
    kk, ke0, ke1, bb0, bb1, cin0, cin1, cv = _prep(
        proj, lora, _pack_dirs(w0), _pack_lora(w_up), _pack_dirs(a0), _pack_lora(a_up),
        k_k.reshape(1, -1), k_a.reshape(1, -1), r_k.reshape(1, -1))
    of, ob = _wkv(proj, kk, ke0, ke1, bb0, bb1, cin0, cin1)
    attn = _attention(proj)
    n = b * t
    out = _post(x2d, of.reshape(n, -1), ob.reshape(n, -1), cv.reshape(n, -1), proj.reshape(n, D_PROJ),
                attn.reshape(n, -1), gn_w, gn_b, w_branch_rwkv.astype(BF16), w_branch_attn.astype(BF16),
                w_out.astype(BF16), out_g)
    return out.reshape(b, t, d)


def kernel(x, norm_g, w_in, shift_mu, w0, w_up, a0, a_up, k_k, k_a, r_k, gn_w, gn_b, q_norm_g, k_norm_g,
           w_branch_rwkv, w_branch_attn, w_out, final_norm_g):
    assert norm_g.shape[0] == 1, "single-layer block"
    return _layer(x, norm_g[0], w_in[0], shift_mu[0], w0[0], w_up[0], a0[0], a_up[0], k_k[0], k_a[0],
                  r_k[0], gn_w[0], gn_b[0], q_norm_g[0], k_norm_g[0], w_branch_rwkv[0],
                  w_branch_attn[0], w_out[0], final_norm_g)
```
